```python
import math
import jax
import jax.numpy as jnp
from jax import lax
import numpy as np

D_MODEL = 2048
BATCH = 4
SEQ = 2048
DEPTH = 2

HEAD_DIM = 128
N_HEADS = D_MODEL // HEAD_DIM
GLA_HEADS = N_HEADS // 4
DIFF_HEADS = (N_HEADS - GLA_HEADS) // 2
MOBA_HEADS = N_HEADS - GLA_HEADS - DIFF_HEADS
DIFF_QK_DIM = HEAD_DIM // 2
DIFF_Q_BLOCK = 128
GLA_DK = HEAD_DIM // 2
GLA_DV = HEAD_DIM
GLA_GATE_RANK = 16
GLA_TAU = 16.0
GLA_CHUNK = 64
MOBA_BLOCK = 256
MOBA_TOPK = 3
MOBA_Q_CHUNK = 32
ROPE_THETA = 10000.0
D_FF = ((8 * D_MODEL // 3 + 255) // 256) * 256
PLE_DIM = 256
LN_EPS = 1e-5
DEEPNORM_ALPHA = (2 * DEPTH) ** 0.25
DEEPNORM_BETA = (8 * DEPTH) ** -0.25

_IN_WIDTHS = (
    DIFF_HEADS * 2 * DIFF_QK_DIM,
    DIFF_HEADS * 2 * DIFF_QK_DIM,
    DIFF_HEADS * HEAD_DIM,
    GLA_HEADS * GLA_DK,
    GLA_HEADS * GLA_DK,
    GLA_HEADS * GLA_DV,
    GLA_HEADS * GLA_DV,
    GLA_GATE_RANK,
    MOBA_HEADS * HEAD_DIM,
    MOBA_HEADS * HEAD_DIM,
    MOBA_HEADS * HEAD_DIM,
)
D_IN = int(sum(_IN_WIDTHS))
IN_SPLITS = tuple(int(v) for v in np.cumsum(_IN_WIDTHS)[:-1])
MIX_WIDTH = (DIFF_HEADS + GLA_HEADS + MOBA_HEADS) * HEAD_DIM

kernel_name = "hymba_style_diff_gla_moba_macaron_deepnorm"


def layer_norm(x, g, b):
    xf = x.astype(jnp.float32)
    mu = jnp.mean(xf, axis=-1, keepdims=True)
    var = jnp.mean(jnp.square(xf - mu), axis=-1, keepdims=True)
    return ((xf - mu) * lax.rsqrt(var + LN_EPS) * g + b).astype(x.dtype)


def rms_norm(x, g):
    xf = x.astype(jnp.float32)
    return (xf * lax.rsqrt(jnp.mean(xf * xf, axis=-1, keepdims=True) + LN_EPS) * g).astype(x.dtype)


def swiglu(h, w_gate, w_up, w_down):
    return (jax.nn.silu(h @ w_gate) * (h @ w_up)) @ w_down


def to_heads(t, n, d):
    b, s, _ = t.shape
    return t.reshape(b, s, n, d).transpose(0, 2, 1, 3)


def rope(t, positions):
    d = t.shape[-1]
    inv = ROPE_THETA ** (-jnp.arange(0, d, 2, dtype=jnp.float32) / d)
    ang = positions.astype(jnp.float32)[:, None, :, None] * inv
    cos, sin = jnp.cos(ang), jnp.sin(ang)
    tf = t.astype(jnp.float32)
    t1, t2 = tf[..., : d // 2], tf[..., d // 2:]
    return jnp.concatenate([t1 * cos - t2 * sin, t2 * cos + t1 * sin], axis=-1).astype(t.dtype)


def diff_attention(q1, q2, k1, k2, v, lam):
    b, h, s, dqk = q1.shape
    scale = dqk ** -0.5
    key_pos = jnp.arange(s)

    def block(i):
        qs = i * DIFF_Q_BLOCK
        q1b = lax.dynamic_slice_in_dim(q1, qs, DIFF_Q_BLOCK, axis=2)
        q2b = lax.dynamic_slice_in_dim(q2, qs, DIFF_Q_BLOCK, axis=2)
        mask = (qs + jnp.arange(DIFF_Q_BLOCK))[:, None] >= key_pos[None, :]
        s1 = jnp.einsum('bhqd,bhkd->bhqk', q1b, k1).astype(jnp.float32) * scale
        s2 = jnp.einsum('bhqd,bhkd->bhqk', q2b, k2).astype(jnp.float32) * scale
        a1 = jax.nn.softmax(jnp.where(mask, s1, -jnp.inf), axis=-1)
        a2 = jax.nn.softmax(jnp.where(mask, s2, -jnp.inf), axis=-1)
        a = (a1 - lam * a2).astype(v.dtype)
        return jnp.einsum('bhqk,bhkd->bhqd', a, v)

    out = lax.map(block, jnp.arange(s // DIFF_Q_BLOCK))
    return jnp.moveaxis(out, 0, 2).reshape(b, h, s, v.shape[-1])


def gla_chunked(q, k, v, log_a):
    dtype = v.dtype
    q, k, v, log_a = (t.astype(jnp.float32) for t in (q, k, v, log_a))
    b, h, s, dk = q.shape
    dv = v.shape[-1]
    n = s // GLA_CHUNK
    chunk = lambda t: jnp.moveaxis(t.reshape(b, h, n, GLA_CHUNK, t.shape[-1]), 2, 0)
    causal = jnp.tril(jnp.ones((GLA_CHUNK, GLA_CHUNK), dtype=bool))

    def step(state, inp):
        qc, kc, vc, gc = inp
        cum = jnp.cumsum(gc, axis=-2)
        inter = jnp.einsum('bhik,bhkv->bhiv', qc * jnp.exp(cum), state)
        rel = cum[:, :, :, None, :] - cum[:, :, None, :, :]
        decay = jnp.exp(jnp.where(causal[:, :, None], rel, -jnp.inf))
        att = jnp.einsum('bhik,bhjk,bhijk->bhij', qc, kc, decay)
        out = inter + jnp.einsum('bhij,bhjv->bhiv', att, vc)
        last = cum[:, :, -1:, :]
        new_state = jnp.exp(last[:, :, 0, :])[..., None] * state + jnp.einsum(
            'bhjk,bhjv->bhkv', kc * jnp.exp(last - cum), vc)
        return new_state, out

    state0 = jnp.zeros((b, h, dk, dv), jnp.float32)
    _, out = lax.scan(step, state0, (chunk(q), chunk(k), chunk(v), chunk(log_a)))
    return jnp.moveaxis(out, 0, 2).reshape(b, h, s, dv).astype(dtype)


def moba_attention(q, k, v):
    b, h, s, d = q.shape
    scale = d ** -0.5
    nb = -(-s // MOBA_BLOCK)
    pad = nb * MOBA_BLOCK - s
    kp = jnp.pad(k, ((0, 0), (0, 0), (0, pad), (0, 0)))
    vp = jnp.pad(v, ((0, 0), (0, 0), (0, pad), (0, 0)))
    kb = kp.reshape(b, h, nb, MOBA_BLOCK, d)
    vb = vp.reshape(b, h, nb, MOBA_BLOCK, d)
    kmean = jnp.mean(kb.astype(jnp.float32), axis=3)
    kt = min(MOBA_TOPK, nb)
    bi = jnp.arange(b)[:, None, None, None]
    hi = jnp.arange(h)[None, :, None, None]

    def chunk(c):
        qs = c * MOBA_Q_CHUNK
        own = qs // MOBA_BLOCK
        qc = lax.dynamic_slice_in_dim(q, qs, MOBA_Q_CHUNK, axis=2)
        gate = jnp.einsum('bhqd,bhnd->bhqn', qc.astype(jnp.float32), kmean)
        gate = jnp.where(jnp.arange(nb) < own, gate, -jnp.inf)
        _, idx = lax.top_k(gate, kt)
        valid = jnp.repeat(jnp.arange(kt) < own, MOBA_BLOCK)
        ksel = kb[bi, hi, idx]
        vsel = vb[bi, hi, idx]
        s_sel = jnp.einsum('bhqd,bhqrkd->bhqrk', qc, ksel).astype(jnp.float32) * scale
        s_sel = jnp.where(valid, s_sel.reshape(b, h, MOBA_Q_CHUNK, kt * MOBA_BLOCK), -jnp.inf)
        kown = lax.dynamic_slice_in_dim(kp, own * MOBA_BLOCK, MOBA_BLOCK, axis=2)
        vown = lax.dynamic_slice_in_dim(vp, own * MOBA_BLOCK, MOBA_BLOCK, axis=2)
        own_mask = (own * MOBA_BLOCK + jnp.arange(MOBA_BLOCK))[None, :] <= (qs + jnp.arange(MOBA_Q_CHUNK))[:, None]
        s_own = jnp.einsum('bhqd,bhkd->bhqk', qc, kown).astype(jnp.float32) * scale
        s_own = jnp.where(own_mask, s_own, -jnp.inf)
        probs = jax.nn.softmax(jnp.concatenate([s_sel, s_own], axis=-1), axis=-1).astype(v.dtype)
        p_sel = probs[..., : kt * MOBA_BLOCK].reshape(b, h, MOBA_Q_CHUNK, kt, MOBA_BLOCK)
        p_own = probs[..., kt * MOBA_BLOCK:]
        return (jnp.einsum('bhqrk,bhqrkd->bhqd', p_sel, vsel)
                + jnp.einsum('bhqk,bhkd->bhqd', p_own, vown))

    out = lax.map(chunk, jnp.arange(s // MOBA_Q_CHUNK))
    return jnp.moveaxis(out, 0, 2).reshape(b, h, s, d)


def token_mix(h, positions, w_in, w_out, diff_lambda, diff_norm_g, gla_gate_up, gla_gate_b,
              gla_norm_g, layer):
    b, s, _ = h.shape
    proj = h @ w_in
    dq, dk, dv, gq, gk, gv, gr, gg, mq, mk, mv = jnp.split(proj, IN_SPLITS, axis=-1)

    lam_init = 0.8 - 0.6 * math.exp(-0.3 * layer)
    lf = diff_lambda.astype(jnp.float32)
    lam = jnp.exp(jnp.sum(lf[0] * lf[1])) - jnp.exp(jnp.sum(lf[2] * lf[3])) + lam_init
    dq = dq.reshape(b, s, DIFF_HEADS, 2, DIFF_QK_DIM).transpose(0, 2, 3, 1, 4)
    dk = dk.reshape(b, s, DIFF_HEADS, 2, DIFF_QK_DIM).transpose(0, 2, 3, 1, 4)
    q1, q2 = rope(dq[:, :, 0], positions), rope(dq[:, :, 1], positions)
    k1, k2 = rope(dk[:, :, 0], positions), rope(dk[:, :, 1], positions)
    o_diff = diff_attention(q1, q2, k1, k2, to_heads(dv, DIFF_HEADS, HEAD_DIM), lam)
    o_diff = rms_norm(o_diff.transpose(0, 2, 1, 3), diff_norm_g) * (1.0 - lam_init)
    o_diff = o_diff.reshape(b, s, DIFF_HEADS * HEAD_DIM)

    log_a = jax.nn.log_sigmoid(gg @ gla_gate_up + gla_gate_b) / GLA_TAU
    o_gla = gla_chunked(to_heads(gq, GLA_HEADS, GLA_DK) * (GLA_DK ** -0.5),
                        to_heads(gk, GLA_HEADS, GLA_DK),
                        to_heads(gv, GLA_HEADS, GLA_DV),
                        to_heads(log_a, GLA_HEADS, GLA_DK))
    o_gla = rms_norm(o_gla.transpose(0, 2, 1, 3), gla_norm_g).reshape(b, s, GLA_HEADS * GLA_DV)
    o_gla = o_gla * jax.nn.silu(gr)

    o_moba = moba_attention(rope(to_heads(mq, MOBA_HEADS, HEAD_DIM), positions),
                            rope(to_heads(mk, MOBA_HEADS, HEAD_DIM), positions),
                            to_heads(mv, MOBA_HEADS, HEAD_DIM))
    o_moba = o_moba.transpose(0, 2, 1, 3).reshape(b, s, MOBA_HEADS * HEAD_DIM)

    return jnp.concatenate([o_diff, o_gla, o_moba], axis=-1) @ w_out


def setup_inputs(seed: int = 0) -> dict:
    key = jax.random.key(seed)
    ks = jax.random.split(key, 21)
    nrm = lambda k, shape, scale: jax.random.normal(k, shape, jnp.float32) * scale
    beta = DEEPNORM_BETA
    return {
        "x": nrm(ks[0], (BATCH, SEQ, D_MODEL), 1.0),
        "p": nrm(ks[1], (DEPTH, BATCH, SEQ, PLE_DIM), 1.0),
        "positions": jnp.broadcast_to(jnp.arange(SEQ, dtype=jnp.int32)[None, :], (BATCH, SEQ)),
        "w_in": nrm(ks[2], (DEPTH, D_MODEL, D_IN), D_MODEL ** -0.5),
        "w_out": nrm(ks[3], (DEPTH, MIX_WIDTH, D_MODEL), beta * MIX_WIDTH ** -0.5),
        "diff_lambda": nrm(ks[4], (DEPTH, 4, DIFF_QK_DIM), 0.1),
        "diff_norm_g": 1.0 + nrm(ks[5], (DEPTH, HEAD_DIM), 0.02),
        "gla_gate_up": nrm(ks[6], (DEPTH, GLA_GATE_RANK, GLA_HEADS * GLA_DK), GLA_GATE_RANK ** -0.5),
        "gla_gate_b": nrm(ks[7], (DEPTH, GLA_HEADS * GLA_DK), 0.1),
        "gla_norm_g": 1.0 + nrm(ks[8], (DEPTH, GLA_DV), 0.02),
        "ffn1_gate": nrm(ks[9], (DEPTH, D_MODEL, D_FF), D_MODEL ** -0.5),
        "ffn1_up": nrm(ks[10], (DEPTH, D_MODEL, D_FF), D_MODEL ** -0.5),
        "ffn1_down": nrm(ks[11], (DEPTH, D_FF, D_MODEL), beta * D_FF ** -0.5),
        "ffn2_gate": nrm(ks[12], (DEPTH, D_MODEL, D_FF), D_MODEL ** -0.5),
        "ffn2_up": nrm(ks[13], (DEPTH, D_MODEL, D_FF), D_MODEL ** -0.5),
        "ffn2_down": nrm(ks[14], (DEPTH, D_FF, D_MODEL), beta * D_FF ** -0.5),
        "w_pe": nrm(ks[15], (DEPTH, PLE_DIM, D_MODEL), beta * PLE_DIM ** -0.5),
        "w_pg": nrm(ks[16], (DEPTH, D_MODEL, D_MODEL), D_MODEL ** -0.5),
        "ln_g": 1.0 + nrm(ks[17], (DEPTH, 4, D_MODEL), 0.02),
        "ln_b": nrm(ks[18], (DEPTH, 4, D_MODEL), 0.02),
    }


def reference(x, p, positions, w_in, w_out, diff_lambda, diff_norm_g, gla_gate_up, gla_gate_b,
              gla_norm_g, ffn1_gate, ffn1_up, ffn1_down, ffn2_gate, ffn2_up, ffn2_down,
              w_pe, w_pg, ln_g, ln_b):
    a = DEEPNORM_ALPHA
    for i in range(DEPTH):
        x = layer_norm(a * x + 0.5 * swiglu(x, ffn1_gate[i], ffn1_up[i], ffn1_down[i]), ln_g[i, 0], ln_b[i, 0])
        x = layer_norm(a * x + token_mix(x, positions, w_in[i], w_out[i], diff_lambda[i], diff_norm_g[i],
                                         gla_gate_up[i], gla_gate_b[i], gla_norm_g[i], i),
                       ln_g[i, 1], ln_b[i, 1])
        x = layer_norm(a * x + 0.5 * swiglu(x, ffn2_gate[i], ffn2_up[i], ffn2_down[i]), ln_g[i, 2], ln_b[i, 2])
        e = (p[i] @ w_pe[i]) * jax.nn.sigmoid(x @ w_pg[i])
        x = layer_norm(a * x + e, ln_g[i, 3], ln_b[i, 3])
    return x
```

```python
import functools
import math

import numpy as np
import jax
import jax.numpy as jnp
from jax import lax
from jax.experimental import pallas as pl
from jax.experimental.pallas import tpu as pltpu

F32 = jnp.float32
BF16 = jnp.bfloat16

DEPTH = 2
HEAD_DIM = 128
DIFF_HEADS = 6
GLA_HEADS = 4
MOBA_HEADS = 6
DIFF_QK_DIM = 64
GLA_DK = 64
GLA_DV = 128
GLA_GATE_RANK = 16
GLA_TAU = 16.0
GLA_CHUNK = 64
MOBA_BLOCK = 256
MOBA_TOPK = 3
ROPE_THETA = 10000.0
LN_EPS = 1e-5
DEEPNORM_ALPHA = (2 * DEPTH) ** 0.25

LANES = 128
SUBLANES = 8
VMEM_LIMIT_BYTES = 56 * 1024 * 1024

PROJ_TN = 256
_CB = LANES
COL_DQ = 0
COL_DK = COL_DQ + DIFF_HEADS
COL_MQ = COL_DK + DIFF_HEADS
COL_MK = COL_MQ + MOBA_HEADS
COL_DV = COL_MK + MOBA_HEADS
COL_MV = COL_DV + DIFF_HEADS
COL_GQ = COL_MV + MOBA_HEADS
COL_GK = COL_GQ + GLA_HEADS
COL_GV = COL_GK + GLA_HEADS
COL_GR = COL_GV + GLA_HEADS
COL_GG = COL_GR + GLA_HEADS
GG_WIDTH = 2 * _CB
N_COL_BLOCKS = COL_GG + 2
PROJ_WIDTH = N_COL_BLOCKS * _CB
assert PROJ_WIDTH % PROJ_TN == 0

NEG_INF = float("-inf")


def _cparams(sem):
    return pltpu.CompilerParams(dimension_semantics=sem, vmem_limit_bytes=VMEM_LIMIT_BYTES)


def _layer_norm(y, g, b):
    mu = jnp.mean(y, axis=-1, keepdims=True)
    d = y - mu
    var = jnp.mean(d * d, axis=-1, keepdims=True)
    return d * lax.rsqrt(var + LN_EPS) * g + b


def _dot_nt(a, b):
    return lax.dot_general(a, b, (((1,), (1,)), ((), ())), preferred_element_type=F32)


def _dot(a, b):
    return jnp.dot(a, b, preferred_element_type=F32)


def _split3(x):
    hi = x.astype(BF16)
    r1 = x - hi.astype(F32)
    mid = r1.astype(BF16)
    lo = (r1 - mid.astype(F32)).astype(BF16)
    return hi, mid, lo


def _rope_table_kernel(pos_ref, inv_a_ref, inv_b_ref, sgn_a_ref, sgn_b_ref,
                       cos_a_ref, sin_a_ref, cos_b_ref, sin_b_ref):
    pos = pos_ref[...].astype(F32)
    ang_a = pos * inv_a_ref[...]
    ang_b = pos * inv_b_ref[...]
    cos_a_ref[...] = jnp.cos(ang_a)
    sin_a_ref[...] = jnp.sin(ang_a) * sgn_a_ref[...]
    cos_b_ref[...] = jnp.cos(ang_b)
    sin_b_ref[...] = jnp.sin(ang_b) * sgn_b_ref[...]


def _rope_tables(pos):
    t = pos.shape[0]
    tm = next(c for c in (1024, 512, MOBA_BLOCK) if t % c == 0)

    def pattern(d):
        inv = ROPE_THETA ** (-jnp.arange(0, d, 2, dtype=F32) / d)
        inv = jnp.tile(inv, 2 * LANES // d)[None, :]
        sgn = jnp.tile(jnp.concatenate([-jnp.ones(d // 2, F32), jnp.ones(d // 2, F32)]), LANES // d)[None, :]
        return inv, sgn

    inv_a, sgn_a = pattern(DIFF_QK_DIM)
    inv_b, sgn_b = pattern(HEAD_DIM)
    row = pl.BlockSpec((1, LANES), lambda i: (0, 0))
    out = pl.BlockSpec((tm, LANES), lambda i: (i, 0))
    return pl.pallas_call(
        _rope_table_kernel,
        grid=(t // tm,),
        in_specs=[pl.BlockSpec((tm, 1), lambda i: (i, 0)), row, row, row, row],
        out_specs=[out, out, out, out],
        out_shape=[jax.ShapeDtypeStruct((t, LANES), F32)] * 4,
        compiler_params=_cparams(("parallel",)),
        name="rope_tables",
    )(pos, inv_a, inv_b, sgn_a, sgn_b)


def _ffn_kernel(xf_ref, xb_ref, wg_ref, wu_ref, wd_ref, g_ref, b_ref, of_ref, ob_ref, acc_ref, *, nf):
    j = pl.program_id(1)

    @pl.when(j == 0)
    def _():
        acc_ref[...] = jnp.zeros_like(acc_ref)

    x = xb_ref[...]
    gt = _dot(x, wg_ref[...])
    ut = _dot(x, wu_ref[...])
    h = (gt * jax.nn.sigmoid(gt) * ut).astype(BF16)
    acc_ref[...] += _dot(h, wd_ref[...])

    @pl.when(j == nf - 1)
    def _():
        y = DEEPNORM_ALPHA * xf_ref[...] + 0.5 * acc_ref[...]
        out = _layer_norm(y, g_ref[...], b_ref[...])
        of_ref[...] = out
        ob_ref[...] = out.astype(BF16)


def _ffn_tiles(t, f):
    tm = 512 if t % 512 == 0 else t
    tf = 512 if f % 512 == 0 else f
    return tm, tf


def _ffn(xf, xb, wg, wu, wd, g, b):
    t, d = xf.shape
    f = wg.shape[1]
    tm, tf = _ffn_tiles(t, f)
    nf = f // tf
    row = pl.BlockSpec((tm, d), lambda i, j: (i, 0))
    vec = pl.BlockSpec((1, d), lambda i, j: (0, 0))
    return pl.pallas_call(
        functools.partial(_ffn_kernel, nf=nf),
        grid=(t // tm, nf),
        in_specs=[row, row,
                  pl.BlockSpec((d, tf), lambda i, j: (0, j)),
                  pl.BlockSpec((d, tf), lambda i, j: (0, j)),
                  pl.BlockSpec((tf, d), lambda i, j: (j, 0)),
                  vec, vec],
        out_specs=[row, row],
        out_shape=[jax.ShapeDtypeStruct((t, d), F32), jax.ShapeDtypeStruct((t, d), BF16)],
        scratch_shapes=[pltpu.VMEM((tm, d), F32)],
        compiler_params=_cparams(("parallel", "arbitrary")),
        name="ffn",
    )(xf, xb, wg, wu, wd, g, b)


def _proj_kernel(xb_ref, w_ref, ca_ref, sa_ref, cb_ref, sb_ref, p_ref, gg_ref, *, nj):
    j = pl.program_id(1)
    acc = _dot(xb_ref[...], w_ref[...])
    halves = PROJ_TN // LANES
    blk = lambda c: c * LANES // PROJ_TN
    is_a = j < blk(COL_MQ)
    is_b = (j >= blk(COL_MQ)) & (j < blk(COL_DV))
    scale = jnp.where(j < blk(COL_DK), DIFF_QK_DIM ** -0.5,
                      jnp.where((j >= blk(COL_MQ)) & (j < blk(COL_MK)), HEAD_DIM ** -0.5,
                                jnp.where((j >= blk(COL_GQ)) & (j < blk(COL_GK)), GLA_DK ** -0.5, 1.0))).astype(F32)

    @pl.when(is_a)
    def _():
        for hh in range(halves):
            t = acc[:, hh * LANES:(hh + 1) * LANES]
            lane = lax.broadcasted_iota(jnp.int32, t.shape, 1)
            q = DIFF_QK_DIM // 2
            rot = jnp.where((lane // q) % 2 == 0, pltpu.roll(t, LANES - q, 1), pltpu.roll(t, q, 1))
            o = (t * ca_ref[...] + rot * sa_ref[...]) * scale
            p_ref[:, hh * LANES:(hh + 1) * LANES] = o.astype(BF16)

    @pl.when(is_b)
    def _():
        for hh in range(halves):
            t = acc[:, hh * LANES:(hh + 1) * LANES]
            rot = pltpu.roll(t, HEAD_DIM // 2, 1)
            o = (t * cb_ref[...] + rot * sb_ref[...]) * scale
            p_ref[:, hh * LANES:(hh + 1) * LANES] = o.astype(BF16)

    @pl.when(jnp.logical_not(is_a | is_b))
    def _():
        p_ref[...] = (acc * scale).astype(BF16)

    @pl.when(j == nj - 1)
    def _():
        gg_ref[...] = acc


def _proj(xb, w, tables):
    t, d = xb.shape
    tm = 2048 if t % 2048 == 0 else t
    nj = PROJ_WIDTH // PROJ_TN
    tab = pl.BlockSpec((tm, LANES), lambda i, j: (i, 0))
    return pl.pallas_call(
        functools.partial(_proj_kernel, nj=nj),
        grid=(t // tm, nj),
        in_specs=[pl.BlockSpec((tm, d), lambda i, j: (i, 0)),
                  pl.BlockSpec((d, PROJ_TN), lambda i, j: (0, j)),
                  tab, tab, tab, tab],
        out_specs=[pl.BlockSpec((tm, PROJ_TN), lambda i, j: (i, j)),
                   pl.BlockSpec((tm, GG_WIDTH), lambda i, j: (i, 0))],
        out_shape=[jax.ShapeDtypeStruct((t, PROJ_WIDTH), BF16), jax.ShapeDtypeStruct((t, GG_WIDTH), F32)],
        compiler_params=_cparams(("parallel", "arbitrary")),
        name="in_proj",
    )(xb, w, *tables)


def _reorder_w_in(w_in):
    d = w_in.shape[0]
    widths = (DIFF_HEADS * 128, DIFF_HEADS * 128, DIFF_HEADS * 128, GLA_HEADS * GLA_DK, GLA_HEADS * GLA_DK,
              GLA_HEADS * GLA_DV, GLA_HEADS * GLA_DV, GLA_GATE_RANK, MOBA_HEADS * 128, MOBA_HEADS * 128,
              MOBA_HEADS * 128)
    splits = tuple(int(v) for v in np.cumsum(widths)[:-1])
    dq, dk, dv, gq, gk, gv, gr, gg, mq, mk, mv = jnp.split(w_in, splits, axis=1)

    def pad_heads(w):
        w = w.reshape(d, GLA_HEADS, GLA_DK)
        return jnp.pad(w, ((0, 0), (0, 0), (0, LANES - GLA_DK))).reshape(d, GLA_HEADS * LANES)

    gg = jnp.pad(gg, ((0, 0), (0, GG_WIDTH - GLA_GATE_RANK)))
    out = jnp.concatenate([dq, dk, mq, mk, dv, mv, pad_heads(gq), pad_heads(gk), gv, gr, gg], axis=1)
    assert out.shape[1] == PROJ_WIDTH
    return out.astype(BF16)


def _softmax_step(s_t, vt, m, l, acc_ref):
    m_new = jnp.maximum(m, jnp.max(s_t, axis=0, keepdims=True))
    alpha = jnp.exp(m - m_new)
    p = jnp.exp(s_t - m_new)
    l_new = alpha * l + jnp.sum(p, axis=0, keepdims=True)
    acc_ref[...] = alpha * acc_ref[...] + _dot(vt, p.astype(BF16))
    return m_new, l_new


def _store_v_transposed(v_ref, vt_ref, tk):
    for blk in range(v_ref.shape[0] // tk):
        vt_ref[blk] = v_ref[blk * tk:(blk + 1) * tk, :].astype(F32).T.astype(BF16)


def _diff_kernel(lam_ref, g_ref, q_ref, k_ref, v_ref, o_ref, vt_ref, acc_ref, *, tq, lam_init):
    qi = pl.program_id(2)

    @pl.when(qi == 0)
    def _():
        _store_v_transposed(v_ref, vt_ref, tq)

    q = q_ref[...].astype(F32)
    lane = lax.broadcasted_iota(jnp.int32, q.shape, 1)
    qq = jnp.concatenate([jnp.where(lane < DIFF_QK_DIM, q, 0.0),
                          jnp.where(lane >= DIFF_QK_DIM, q, 0.0)], axis=0).astype(BF16)
    nq = 2 * tq
    acc_ref[...] = jnp.zeros_like(acc_ref)
    m0 = jnp.full((1, nq), NEG_INF, F32)
    l0 = jnp.zeros((1, nq), F32)

    start = pl.multiple_of(qi * tq, tq)
    s_t = _dot_nt(k_ref[pl.ds(start, tq), :], qq)
    key = lax.broadcasted_iota(jnp.int32, s_t.shape, 0)
    qcol = lax.broadcasted_iota(jnp.int32, s_t.shape, 1) % tq
    s_t = jnp.where(key <= qcol, s_t, NEG_INF)
    m, l = _softmax_step(s_t, vt_ref[qi], m0, l0, acc_ref)

    def body(t, carry):
        off = pl.multiple_of(t * tq, tq)
        s = _dot_nt(k_ref[pl.ds(off, tq), :], qq)
        return _softmax_step(s, vt_ref[t], *carry, acc_ref)

    m, l = lax.fori_loop(0, qi, body, (m, l))

    lf = lam_ref[...]
    lam = (jnp.exp(jnp.sum(lf[0:1] * lf[1:2], axis=1, keepdims=True))
           - jnp.exp(jnp.sum(lf[2:3] * lf[3:4], axis=1, keepdims=True)) + lam_init)
    o_t = acc_ref[...] / l
    o = (o_t[:, :tq] - lam * o_t[:, tq:]).T
    o = o * lax.rsqrt(jnp.mean(o * o, axis=-1, keepdims=True) + LN_EPS) * g_ref[...] * (1.0 - lam_init)
    o_ref[...] = o.astype(BF16)


def _diff_attention(p, lam, g, batch, seq, layer):
    tq = 256
    nq = seq // tq
    lam_init = 0.8 - 0.6 * math.exp(-0.3 * layer)
    full = lambda c0: pl.BlockSpec((seq, LANES), lambda b, h, i: (b, c0 + h))
    return pl.pallas_call(
        functools.partial(_diff_kernel, tq=tq, lam_init=lam_init),
        grid=(batch, DIFF_HEADS, nq),
        in_specs=[pl.BlockSpec((4, DIFF_QK_DIM), lambda b, h, i: (0, 0)),
                  pl.BlockSpec((1, LANES), lambda b, h, i: (0, 0)),
                  pl.BlockSpec((tq, LANES), lambda b, h, i: (b * nq + i, COL_DQ + h)),
                  full(COL_DK), full(COL_DV)],
        out_specs=pl.BlockSpec((tq, LANES), lambda b, h, i: (b * nq + i, h)),
        out_shape=jax.ShapeDtypeStruct((batch * seq, DIFF_HEADS * LANES), BF16),
        scratch_shapes=[pltpu.VMEM((nq, LANES, tq), BF16), pltpu.VMEM((LANES, 2 * tq), F32)],
        compiler_params=_cparams(("parallel", "parallel", "arbitrary")),
        name="diff_attn",
    )(lam, g, p, p, p)


def _moba_kernel(q_ref, k_ref, v_ref, o_ref, vt_ref, km_ref, sel_ref, acc_ref, *, nb):
    tq = MOBA_BLOCK
    qi = pl.program_id(2)

    @pl.when(qi == 0)
    def _():
        _store_v_transposed(v_ref, vt_ref, tq)
        km_ref[...] = jnp.zeros_like(km_ref)
        for blk in range(nb):
            kb = k_ref[blk * tq:(blk + 1) * tq, :].astype(F32)
            km_ref[blk:blk + 1, :] = jnp.mean(kb, axis=0, keepdims=True)

    q = q_ref[...]
    km = km_ref[...]
    km_hi = km.astype(BF16)
    km_lo = (km - km_hi.astype(F32)).astype(BF16)
    gate = _dot_nt(km_hi, q) + _dot_nt(km_lo, q)
    n_idx = lax.broadcasted_iota(jnp.int32, gate.shape, 0)
    rank = jnp.zeros(gate.shape, jnp.int32)
    for mb in range(nb):
        gm = gate[mb:mb + 1, :]
        beats = (gm > gate) | ((gm == gate) & (mb < n_idx))
        rank = rank + jnp.where(beats & (mb < qi), 1, 0)
    sel = (n_idx < qi) & (rank < MOBA_TOPK)
    sel_ref[...] = sel.astype(F32)

    acc_ref[...] = jnp.zeros_like(acc_ref)
    m0 = jnp.full((1, tq), NEG_INF, F32)
    l0 = jnp.zeros((1, tq), F32)

    start = pl.multiple_of(qi * tq, tq)
    s_t = _dot_nt(k_ref[pl.ds(start, tq), :], q)
    key = lax.broadcasted_iota(jnp.int32, s_t.shape, 0)
    qcol = lax.broadcasted_iota(jnp.int32, s_t.shape, 1)
    s_t = jnp.where(key <= qcol, s_t, NEG_INF)
    m, l = _softmax_step(s_t, vt_ref[qi], m0, l0, acc_ref)

    def body(n, carry):
        off = pl.multiple_of(n * tq, tq)
        s = _dot_nt(k_ref[pl.ds(off, tq), :], q)
        s = jnp.where(sel_ref[pl.ds(n, 1), :] > 0.0, s, NEG_INF)
        return _softmax_step(s, vt_ref[n], *carry, acc_ref)

    m, l = lax.fori_loop(0, qi, body, (m, l))
    o_ref[...] = (acc_ref[...] / l).T.astype(BF16)


def _moba_attention(p, batch, seq):
    tq = MOBA_BLOCK
    assert seq % tq == 0
    nb = seq // tq
    nbp = -(-nb // SUBLANES) * SUBLANES
    full = lambda c0: pl.BlockSpec((seq, LANES), lambda b, h, i: (b, c0 + h))
    return pl.pallas_call(
        functools.partial(_moba_kernel, nb=nb),
        grid=(batch, MOBA_HEADS, nb),
        in_specs=[pl.BlockSpec((tq, LANES), lambda b, h, i: (b * nb + i, COL_MQ + h)),
                  full(COL_MK), full(COL_MV)],
        out_specs=pl.BlockSpec((tq, LANES), lambda b, h, i: (b * nb + i, h)),
        out_shape=jax.ShapeDtypeStruct((batch * seq, MOBA_HEADS * LANES), BF16),
        scratch_shapes=[pltpu.VMEM((nb, LANES, tq), BF16), pltpu.VMEM((nbp, LANES), F32),
                        pltpu.VMEM((nbp, tq), F32), pltpu.VMEM((LANES, tq), F32)],
        compiler_params=_cparams(("parallel", "parallel", "arbitrary")),
        name="moba_attn",
    )(p, p, p)


def _gla_constants():
    c = GLA_CHUNK
    idx = np.arange(c)
    mats = [np.tril(np.ones((c, c), np.float32))]
    mq, mk, bm = [], [], []
    for lvl in range(6):
        m = 32 >> lvl
        parent, half = idx // (2 * m), (idx // m) % 2
        ref = parent * 2 * m + m - 1
        t = idx[None, :]
        mq.append(((half == 1)[:, None] & (t > ref[:, None]) & (t <= idx[:, None])).astype(np.float32))
        mk.append(((half == 0)[:, None] & (t > idx[:, None]) & (t <= ref[:, None])).astype(np.float32))
        bm.append(((parent[:, None] == parent[None, :]) & (half == 1)[:, None] & (half == 0)[None, :]).astype(np.float32))
    bm.append(np.eye(c, dtype=np.float32))
    cmat = np.concatenate(mats + mq + mk, axis=0)
    return jnp.asarray(cmat, BF16), jnp.asarray(np.concatenate(bm, axis=0), F32)


def _gla_kernel(cmat_ref, bm_ref, up_ref, bias_ref, gn_ref, gg_ref, q_ref, k_ref, v_ref, r_ref, o_ref,
                la_ref, st_ref, *, seq):
    c = GLA_CHUNK
    up_hi, up_mid, _ = _split3(up_ref[...])
    rows = 512 if seq % 512 == 0 else seq
    for r0 in range(0, seq, rows):
        g_hi, g_mid, _ = _split3(gg_ref[r0:r0 + rows, :])
        z = _dot(g_hi, up_hi) + _dot(g_hi, up_mid) + _dot(g_mid, up_hi) + bias_ref[...]
        la_ref[r0:r0 + rows, :] = (jnp.minimum(z, 0.0) - jnp.log(1.0 + jnp.exp(-jnp.abs(z)))) * (1.0 / GLA_TAU)

    st_ref[...] = jnp.zeros_like(st_ref)
    cmat = cmat_ref[...]

    def chunk(ci, carry):
        sl = pl.ds(pl.multiple_of(ci * c, c), c)
        g_hi, g_mid, g_lo = _split3(la_ref[sl, :])
        e = _dot(cmat, g_hi) + _dot(cmat, g_mid) + _dot(cmat, g_lo)
        cum = e[0:c]
        q = q_ref[sl, :].astype(F32)
        k = k_ref[sl, :].astype(F32)
        v = v_ref[sl, :]
        att = bm_ref[6 * c:7 * c, :] * _dot_nt(q.astype(BF16), k.astype(BF16))
        for lvl in range(6):
            eq = jnp.minimum(e[(1 + lvl) * c:(2 + lvl) * c], 0.0)
            ek = jnp.minimum(e[(7 + lvl) * c:(8 + lvl) * c], 0.0)
            qs = (q * jnp.exp(eq)).astype(BF16)
            ks = (k * jnp.exp(ek)).astype(BF16)
            att = att + bm_ref[lvl * c:(lvl + 1) * c, :] * _dot_nt(qs, ks)
        st = st_ref[...]
        qe = (q * jnp.exp(cum)).astype(BF16)
        o = _dot_nt(qe, st.astype(BF16)) + _dot(att.astype(BF16), v)
        last = cum[c - 1:c, :]
        kd = (k * jnp.exp(last - cum)).astype(BF16)
        st_ref[...] = st * jnp.exp(last) + _dot(v.astype(F32).T.astype(BF16), kd)
        o = o * lax.rsqrt(jnp.mean(o * o, axis=-1, keepdims=True) + LN_EPS) * gn_ref[...]
        r = r_ref[sl, :].astype(F32)
        o_ref[sl, :] = (o * (r * jax.nn.sigmoid(r))).astype(BF16)
        return carry

    lax.fori_loop(0, seq // c, chunk, 0)


def _gla(p, gg, up, bias, gn, batch, seq):
    cmat, bm = _gla_constants()
    up_p = jnp.pad(up.reshape(GLA_GATE_RANK, GLA_HEADS, GLA_DK),
                   ((0, GG_WIDTH - GLA_GATE_RANK), (0, 0), (0, LANES - GLA_DK))).reshape(GG_WIDTH, GLA_HEADS * LANES)
    bias_p = jnp.pad(bias.reshape(GLA_HEADS, GLA_DK), ((0, 0), (0, LANES - GLA_DK))).reshape(1, GLA_HEADS * LANES)
    const = lambda shape: pl.BlockSpec(shape, lambda b, h: (0, 0))
    col = lambda c0: pl.BlockSpec((seq, LANES), lambda b, h: (b, c0 + h))
    return pl.pallas_call(
        functools.partial(_gla_kernel, seq=seq),
        grid=(batch, GLA_HEADS),
        in_specs=[const(cmat.shape), const(bm.shape),
                  pl.BlockSpec((GG_WIDTH, LANES), lambda b, h: (0, h)),
                  pl.BlockSpec((1, LANES), lambda b, h: (0, h)),
                  const((1, LANES)),
                  pl.BlockSpec((seq, GG_WIDTH), lambda b, h: (b, 0)),
                  col(COL_GQ), col(COL_GK), col(COL_GV), col(COL_GR)],
        out_specs=pl.BlockSpec((seq, LANES), lambda b, h: (b, h)),
        out_shape=jax.ShapeDtypeStruct((batch * seq, GLA_HEADS * LANES), BF16),
        scratch_shapes=[pltpu.VMEM((seq, LANES), F32), pltpu.VMEM((GLA_DV, LANES), F32)],
        compiler_params=_cparams(("parallel", "parallel")),
        name="gla",
    )(cmat, bm, up_p, bias_p, gn, gg, p, p, p, p)


def _out_kernel(a_ref, w_ref, xf_ref, g_ref, b_ref, of_ref, ob_ref, acc_ref, *, nk):
    k = pl.program_id(1)

    @pl.when(k == 0)
    def _():
        acc_ref[...] = jnp.zeros_like(acc_ref)

    acc_ref[...] += _dot(a_ref[...], w_ref[...])

    @pl.when(k == nk - 1)
    def _():
        out = _layer_norm(DEEPNORM_ALPHA * xf_ref[...] + acc_ref[...], g_ref[...], b_ref[...])
        of_ref[...] = out
        ob_ref[...] = out.astype(BF16)


def _out_proj(a, w, xf, g, b):
    t, d = xf.shape
    kdim = a.shape[1]
    tm = 512 if t % 512 == 0 else t
    tk = 512 if kdim % 512 == 0 else kdim
    nk = kdim // tk
    row = pl.BlockSpec((tm, d), lambda i, k: (i, 0))
    vec = pl.BlockSpec((1, d), lambda i, k: (0, 0))
    return pl.pallas_call(
        functools.partial(_out_kernel, nk=nk),
        grid=(t // tm, nk),
        in_specs=[pl.BlockSpec((tm, tk), lambda i, k: (i, k)),
                  pl.BlockSpec((tk, d), lambda i, k: (k, 0)),
                  row, vec, vec],
        out_specs=[row, row],
        out_shape=[jax.ShapeDtypeStruct((t, d), F32), jax.ShapeDtypeStruct((t, d), BF16)],
        scratch_shapes=[pltpu.VMEM((tm, d), F32)],
        compiler_params=_cparams(("parallel", "arbitrary")),
        name="out_proj",
    )(a, w, xf, g, b)


def _ple_kernel(xb_ref, wg_ref, pb_ref, we_ref, xf_ref, g_ref, b_ref, of_ref, ob_ref, acc_ref, *, nk):
    k = pl.program_id(1)

    @pl.when(k == 0)
    def _():
        acc_ref[...] = jnp.zeros_like(acc_ref)

    acc_ref[...] += _dot(xb_ref[...], wg_ref[...])

    @pl.when(k == nk - 1)
    def _():
        e = _dot(pb_ref[...], we_ref[...]) * jax.nn.sigmoid(acc_ref[...])
        out = _layer_norm(DEEPNORM_ALPHA * xf_ref[...] + e, g_ref[...], b_ref[...])
        of_ref[...] = out
        ob_ref[...] = out.astype(BF16)


def _ple(xf, xb, pb, w_pe, w_pg, g, b):
    t, d = xf.shape
    pdim = pb.shape[1]
    tm = 512 if t % 512 == 0 else t
    tk = 512 if d % 512 == 0 else d
    nk = d // tk
    row = pl.BlockSpec((tm, d), lambda i, k: (i, 0))
    vec = pl.BlockSpec((1, d), lambda i, k: (0, 0))
    return pl.pallas_call(
        functools.partial(_ple_kernel, nk=nk),
        grid=(t // tm, nk),
        in_specs=[pl.BlockSpec((tm, tk), lambda i, k: (i, k)),
                  pl.BlockSpec((tk, d), lambda i, k: (k, 0)),
                  pl.BlockSpec((tm, pdim), lambda i, k: (i, 0)),
                  pl.BlockSpec((pdim, d), lambda i, k: (0, 0)),
                  row, vec, vec],
        out_specs=[row, row],
        out_shape=[jax.ShapeDtypeStruct((t, d), F32), jax.ShapeDtypeStruct((t, d), BF16)],
        scratch_shapes=[pltpu.VMEM((tm, d), F32)],
        compiler_params=_cparams(("parallel", "arbitrary")),
        name="ple",
    )(xb, w_pg, pb, w_pe, xf, g, b)


def kernel(x, p, positions, w_in, w_out, diff_lambda, diff_norm_g, gla_gate_up, gla_gate_b, gla_norm_g,
           ffn1_gate, ffn1_up, ffn1_down, ffn2_gate, ffn2_up, ffn2_down, w_pe, w_pg, ln_g, ln_b):
    batch, seq, d = x.shape
    t = batch * seq
    assert d == (DIFF_HEADS + GLA_HEADS + MOBA_HEADS) * HEAD_DIM and seq % MOBA_BLOCK == 0
    xf = x.reshape(t, d)
    xb = xf.astype(BF16)
    tables = _rope_tables(positions.reshape(t, 1))
    bf = lambda w: w.astype(BF16)
    vec = lambda v: v.reshape(1, -1)
    for i in range(DEPTH):
        xf, xb = _ffn(xf, xb, bf(ffn1_gate[i]), bf(ffn1_up[i]), bf(ffn1_down[i]), vec(ln_g[i, 0]), vec(ln_b[i, 0]))
        proj, gg = _proj(xb, _reorder_w_in(w_in[i]), tables)
        o_diff = _diff_attention(proj, diff_lambda[i], vec(diff_norm_g[i]), batch, seq, i)
        o_gla = _gla(proj, gg, gla_gate_up[i], gla_gate_b[i], vec(gla_norm_g[i]), batch, seq)
        o_moba = _moba_attention(proj, batch, seq)
        mix = jnp.concatenate([o_diff, o_gla, o_moba], axis=1)
        xf, xb = _out_proj(mix, bf(w_out[i]), xf, vec(ln_g[i, 1]), vec(ln_b[i, 1]))
        xf, xb = _ffn(xf, xb, bf(ffn2_gate[i]), bf(ffn2_up[i]), bf(ffn2_down[i]), vec(ln_g[i, 2]), vec(ln_b[i, 2]))
        xf, xb = _ple(xf, xb, bf(p[i].reshape(t, -1)), bf(w_pe[i]), bf(w_pg[i]), vec(ln_g[i, 3]), vec(ln_b[i, 3]))
    return xf.reshape(batch, seq, d)
```

```python
import functools
import math

import numpy as np
import jax
import jax.numpy as jnp
from jax import lax
from jax.experimental import pallas as pl
from jax.experimental.pallas import tpu as pltpu

F32 = jnp.float32
BF16 = jnp.bfloat16

DEPTH = 2
HEAD_DIM = 128
DIFF_HEADS = 6
GLA_HEADS = 4
MOBA_HEADS = 6
DIFF_QK_DIM = 64
GLA_DK = 64
GLA_DV = 128
GLA_GATE_RANK = 16
GLA_TAU = 16.0
GLA_CHUNK = 64
MOBA_BLOCK = 256
MOBA_TOPK = 3
ROPE_THETA = 10000.0
LN_EPS = 1e-5
DEEPNORM_ALPHA = (2 * DEPTH) ** 0.25

LANES = 128
SUBLANES = 8
VMEM_LIMIT_BYTES = 56 * 1024 * 1024

PROJ_TN = 256
_CB = LANES
COL_DQ = 0
COL_DK = COL_DQ + DIFF_HEADS
COL_MQ = COL_DK + DIFF_HEADS
COL_MK = COL_MQ + MOBA_HEADS
COL_DV = COL_MK + MOBA_HEADS
COL_MV = COL_DV + DIFF_HEADS
COL_GQ = COL_MV + MOBA_HEADS
COL_GK = COL_GQ + GLA_HEADS
COL_GV = COL_GK + GLA_HEADS
COL_GR = COL_GV + GLA_HEADS
COL_GG = COL_GR + GLA_HEADS
GG_WIDTH = 2 * _CB
N_COL_BLOCKS = COL_GG + 2
PROJ_WIDTH = N_COL_BLOCKS * _CB
assert PROJ_WIDTH % PROJ_TN == 0

NEG_INF = float("-inf")
LOG2_E = math.log2(math.e)
V_ROWS = HEAD_DIM + 16


def _cparams(sem):
    return pltpu.CompilerParams(dimension_semantics=sem, vmem_limit_bytes=VMEM_LIMIT_BYTES)


def _layer_norm(y, g, b):
    mu = jnp.mean(y, axis=-1, keepdims=True)
    d = y - mu
    var = jnp.mean(d * d, axis=-1, keepdims=True)
    return d * lax.rsqrt(var + LN_EPS) * g + b


def _dot_nt(a, b):
    return lax.dot_general(a, b, (((1,), (1,)), ((), ())), preferred_element_type=F32)


def _dot(a, b):
    return jnp.dot(a, b, preferred_element_type=F32)


def _head_cols(g):
    return slice(g * LANES, (g + 1) * LANES)


def _rope_table_kernel(pos_ref, inv_a_ref, inv_b_ref, sgn_a_ref, sgn_b_ref,
                       cos_a_ref, sin_a_ref, cos_b_ref, sin_b_ref):
    pos = pos_ref[...].astype(F32)
    ang_a = pos * inv_a_ref[...]
    ang_b = pos * inv_b_ref[...]
    cos_a_ref[...] = jnp.cos(ang_a)
    sin_a_ref[...] = jnp.sin(ang_a) * sgn_a_ref[...]
    cos_b_ref[...] = jnp.cos(ang_b)
    sin_b_ref[...] = jnp.sin(ang_b) * sgn_b_ref[...]


def _rope_tables(pos):
    t = pos.shape[0]
    tm = next(c for c in (1024, 512, MOBA_BLOCK) if t % c == 0)

    def pattern(d):
        inv = ROPE_THETA ** (-jnp.arange(0, d, 2, dtype=F32) / d)
        inv = jnp.tile(inv, 2 * LANES // d)[None, :]
        sgn = jnp.tile(jnp.concatenate([-jnp.ones(d // 2, F32), jnp.ones(d // 2, F32)]), LANES // d)[None, :]
        return inv, sgn

    inv_a, sgn_a = pattern(DIFF_QK_DIM)
    inv_b, sgn_b = pattern(HEAD_DIM)
    row = pl.BlockSpec((1, LANES), lambda i: (0, 0))
    out = pl.BlockSpec((tm, LANES), lambda i: (i, 0))
    return pl.pallas_call(
        _rope_table_kernel,
        grid=(t // tm,),
        in_specs=[pl.BlockSpec((tm, 1), lambda i: (i, 0)), row, row, row, row],
        out_specs=[out, out, out, out],
        out_shape=[jax.ShapeDtypeStruct((t, LANES), F32)] * 4,
        compiler_params=_cparams(("parallel",)),
        name="rope_tables",
    )(pos, inv_a, inv_b, sgn_a, sgn_b)


def _ffn_kernel(xf_ref, xb_ref, wg_ref, wu_ref, wd_ref, g_ref, b_ref, of_ref, ob_ref, acc_ref, *, nf):
    j = pl.program_id(1)

    @pl.when(j == 0)
    def _():
        acc_ref[...] = jnp.zeros_like(acc_ref)

    x = xb_ref[...]
    gt = _dot(x, wg_ref[...])
    ut = _dot(x, wu_ref[...])
    h = (gt * jax.nn.sigmoid(gt) * ut).astype(BF16)
    acc_ref[...] += _dot(h, wd_ref[...])

    @pl.when(j == nf - 1)
    def _():
        y = DEEPNORM_ALPHA * xf_ref[...] + 0.5 * acc_ref[...]
        out = _layer_norm(y, g_ref[...], b_ref[...])
        of_ref[...] = out
        ob_ref[...] = out.astype(BF16)


def _ffn_tiles(t, f):
    tm = 512 if t % 512 == 0 else t
    tf = 512 if f % 512 == 0 else f
    return tm, tf


def _ffn(xf, xb, wg, wu, wd, g, b):
    t, d = xf.shape
    f = wg.shape[1]
    tm, tf = _ffn_tiles(t, f)
    nf = f // tf
    row = pl.BlockSpec((tm, d), lambda i, j: (i, 0))
    vec = pl.BlockSpec((1, d), lambda i, j: (0, 0))
    return pl.pallas_call(
        functools.partial(_ffn_kernel, nf=nf),
        grid=(t // tm, nf),
        in_specs=[row, row,
                  pl.BlockSpec((d, tf), lambda i, j: (0, j)),
                  pl.BlockSpec((d, tf), lambda i, j: (0, j)),
                  pl.BlockSpec((tf, d), lambda i, j: (j, 0)),
                  vec, vec],
        out_specs=[row, row],
        out_shape=[jax.ShapeDtypeStruct((t, d), F32), jax.ShapeDtypeStruct((t, d), BF16)],
        scratch_shapes=[pltpu.VMEM((tm, d), F32)],
        compiler_params=_cparams(("parallel", "arbitrary")),
        name="ffn",
    )(xf, xb, wg, wu, wd, g, b)


def _proj_kernel(xb_ref, w_ref, ca_ref, sa_ref, cb_ref, sb_ref, p_ref, gg_ref, *, nj):
    j = pl.program_id(1)
    acc = _dot(xb_ref[...], w_ref[...])
    halves = PROJ_TN // LANES
    blk = lambda c: c * LANES // PROJ_TN
    is_a = j < blk(COL_MQ)
    is_b = (j >= blk(COL_MQ)) & (j < blk(COL_DV))
    scale = jnp.where(j < blk(COL_DK), DIFF_QK_DIM ** -0.5 * LOG2_E,
                      jnp.where((j >= blk(COL_MQ)) & (j < blk(COL_MK)), HEAD_DIM ** -0.5 * LOG2_E,
                                jnp.where((j >= blk(COL_GQ)) & (j < blk(COL_GK)), GLA_DK ** -0.5, 1.0))).astype(F32)

    @pl.when(is_a)
    def _():
        for hh in range(halves):
            t = acc[:, hh * LANES:(hh + 1) * LANES]
            lane = lax.broadcasted_iota(jnp.int32, t.shape, 1)
            q = DIFF_QK_DIM // 2
            rot = jnp.where((lane // q) % 2 == 0, pltpu.roll(t, LANES - q, 1), pltpu.roll(t, q, 1))
            o = (t * ca_ref[...] + rot * sa_ref[...]) * scale
            p_ref[:, hh * LANES:(hh + 1) * LANES] = o.astype(BF16)

    @pl.when(is_b)
    def _():
        for hh in range(halves):
            t = acc[:, hh * LANES:(hh + 1) * LANES]
            rot = pltpu.roll(t, HEAD_DIM // 2, 1)
            o = (t * cb_ref[...] + rot * sb_ref[...]) * scale
            p_ref[:, hh * LANES:(hh + 1) * LANES] = o.astype(BF16)

    @pl.when(jnp.logical_not(is_a | is_b))
    def _():
        p_ref[...] = (acc * scale).astype(BF16)

    @pl.when(j == nj - 1)
    def _():
        gg_ref[...] = acc


def _proj(xb, w, tables):
    t, d = xb.shape
    tm = 2048 if t % 2048 == 0 else t
    nj = PROJ_WIDTH // PROJ_TN
    tab = pl.BlockSpec((tm, LANES), lambda i, j: (i, 0))
    return pl.pallas_call(
        functools.partial(_proj_kernel, nj=nj),
        grid=(t // tm, nj),
        in_specs=[pl.BlockSpec((tm, d), lambda i, j: (i, 0)),
                  pl.BlockSpec((d, PROJ_TN), lambda i, j: (0, j)),
                  tab, tab, tab, tab],
        out_specs=[pl.BlockSpec((tm, PROJ_TN), lambda i, j: (i, j)),
                   pl.BlockSpec((tm, GG_WIDTH), lambda i, j: (i, 0))],
        out_shape=[jax.ShapeDtypeStruct((t, PROJ_WIDTH), BF16), jax.ShapeDtypeStruct((t, GG_WIDTH), F32)],
        compiler_params=_cparams(("parallel", "arbitrary")),
        name="in_proj",
    )(xb, w, *tables)


def _reorder_w_in(w_in):
    d = w_in.shape[0]
    widths = (DIFF_HEADS * 128, DIFF_HEADS * 128, DIFF_HEADS * 128, GLA_HEADS * GLA_DK, GLA_HEADS * GLA_DK,
              GLA_HEADS * GLA_DV, GLA_HEADS * GLA_DV, GLA_GATE_RANK, MOBA_HEADS * 128, MOBA_HEADS * 128,
              MOBA_HEADS * 128)
    splits = tuple(int(v) for v in np.cumsum(widths)[:-1])
    dq, dk, dv, gq, gk, gv, gr, gg, mq, mk, mv = jnp.split(w_in, splits, axis=1)

    def pad_heads(w):
        w = w.reshape(d, GLA_HEADS, GLA_DK)
        return jnp.pad(w, ((0, 0), (0, 0), (0, LANES - GLA_DK))).reshape(d, GLA_HEADS * LANES)

    gg = jnp.pad(gg, ((0, 0), (0, GG_WIDTH - GLA_GATE_RANK)))
    out = jnp.concatenate([dq, dk, mq, mk, dv, mv, pad_heads(gq), pad_heads(gk), gv, gr, gg], axis=1)
    assert out.shape[1] == PROJ_WIDTH
    return out.astype(BF16)


def _store_v_transposed(v_ref, vt_ref, tk):
    seq = v_ref.shape[0]
    for blk in range(seq // tk):
        vt_ref[0:HEAD_DIM, blk * tk:(blk + 1) * tk] = v_ref[blk * tk:(blk + 1) * tk, :].astype(F32).T.astype(BF16)
    vt_ref[HEAD_DIM:V_ROWS, :] = jnp.ones((V_ROWS - HEAD_DIM, seq), BF16)


def _col_max(parts):
    m = jnp.max(parts[0], axis=0, keepdims=True)
    for s in parts[1:]:
        m = jnp.maximum(m, jnp.max(s, axis=0, keepdims=True))
    return m


def _diff_kernel(lam_ref, g_ref, q_ref, k_ref, v_ref, o_ref, vt_ref, *, tq, lam_init):
    seq = k_ref.shape[0]
    _store_v_transposed(v_ref, vt_ref, tq)
    lf = lam_ref[...]
    lam = (jnp.exp(jnp.sum(lf[0:1] * lf[1:2], axis=1, keepdims=True))
           - jnp.exp(jnp.sum(lf[2:3] * lf[3:4], axis=1, keepdims=True)) + lam_init)
    lane = lax.broadcasted_iota(jnp.int32, (tq, LANES), 1)
    causal = (lax.broadcasted_iota(jnp.int32, (tq, 2 * tq), 0)
              <= lax.broadcasted_iota(jnp.int32, (tq, 2 * tq), 1) % tq)
    for qi in range(seq // tq):
        lo, hi = qi * tq, (qi + 1) * tq
        q = q_ref[lo:hi, :].astype(F32)
        qq = jnp.concatenate([jnp.where(lane < DIFF_QK_DIM, q, 0.0),
                              jnp.where(lane >= DIFF_QK_DIM, q, 0.0)], axis=0).astype(BF16)
        parts = [jnp.where(causal, _dot_nt(k_ref[lo:hi, :], qq), NEG_INF)]
        if qi > 0:
            parts.append(_dot_nt(k_ref[0:lo, :], qq))
        m = _col_max(parts)
        acc = _dot(vt_ref[:, lo:hi], jnp.exp2(parts[0] - m).astype(BF16))
        if qi > 0:
            acc = acc + _dot(vt_ref[:, 0:lo], jnp.exp2(parts[1] - m).astype(BF16))
        o_t = acc[0:HEAD_DIM] / acc[HEAD_DIM:HEAD_DIM + 1]
        o = (o_t[:, :tq] - lam * o_t[:, tq:]).T
        o = o * lax.rsqrt(jnp.mean(o * o, axis=-1, keepdims=True) + LN_EPS) * g_ref[...] * (1.0 - lam_init)
        o_ref[lo:hi, :] = o.astype(BF16)


def _diff_attention(p, lam, g, batch, seq, layer):
    tq = 256
    lam_init = 0.8 - 0.6 * math.exp(-0.3 * layer)
    col = lambda c0: pl.BlockSpec((seq, LANES), lambda b, h: (b, c0 + h))
    return pl.pallas_call(
        functools.partial(_diff_kernel, tq=tq, lam_init=lam_init),
        grid=(batch, DIFF_HEADS),
        in_specs=[pl.BlockSpec((4, DIFF_QK_DIM), lambda b, h: (0, 0)),
                  pl.BlockSpec((1, LANES), lambda b, h: (0, 0)),
                  col(COL_DQ), col(COL_DK), col(COL_DV)],
        out_specs=pl.BlockSpec((seq, LANES), lambda b, h: (b, h)),
        out_shape=jax.ShapeDtypeStruct((batch * seq, DIFF_HEADS * LANES), BF16),
        scratch_shapes=[pltpu.VMEM((V_ROWS, seq), BF16)],
        compiler_params=_cparams(("parallel", "parallel")),
        name="diff_attn",
    )(lam, g, p, p, p)


def _moba_kernel(q_ref, k_ref, v_ref, o_ref, vt_ref, *, nb):
    tq = MOBA_BLOCK
    _store_v_transposed(v_ref, vt_ref, tq)
    nbp = -(-nb // SUBLANES) * SUBLANES
    rows = [jnp.mean(k_ref[n * tq:(n + 1) * tq, :].astype(F32), axis=0, keepdims=True) for n in range(nb)]
    if nbp > nb:
        rows.append(jnp.zeros((nbp - nb, LANES), F32))
    km = jnp.concatenate(rows, axis=0)
    km_hi = km.astype(BF16)
    km_lo = (km - km_hi.astype(F32)).astype(BF16)
    causal = (lax.broadcasted_iota(jnp.int32, (tq, tq), 0) <= lax.broadcasted_iota(jnp.int32, (tq, tq), 1))
    for qi in range(nb):
        lo, hi = qi * tq, (qi + 1) * tq
        q = q_ref[lo:hi, :]
        parts = [jnp.where(causal, _dot_nt(k_ref[lo:hi, :], q), NEG_INF)]
        if qi > 0:
            s_past = _dot_nt(k_ref[0:lo, :], q)
            if qi > MOBA_TOPK:
                gate = _dot_nt(km_hi, q) + _dot_nt(km_lo, q)
                n_idx = lax.broadcasted_iota(jnp.int32, gate.shape, 0)
                rank = jnp.zeros(gate.shape, jnp.int32)
                for mb in range(qi):
                    gm = gate[mb:mb + 1, :]
                    rank = rank + jnp.where((gm > gate) | ((gm == gate) & (mb < n_idx)), 1, 0)
                bias = jnp.where(rank < MOBA_TOPK, 0.0, NEG_INF)
                s_past = jnp.concatenate([s_past[n * tq:(n + 1) * tq] + bias[n:n + 1, :] for n in range(qi)], axis=0)
            parts.append(s_past)
        m = _col_max(parts)
        acc = _dot(vt_ref[:, lo:hi], jnp.exp2(parts[0] - m).astype(BF16))
        if qi > 0:
            acc = acc + _dot(vt_ref[:, 0:lo], jnp.exp2(parts[1] - m).astype(BF16))
        o_ref[lo:hi, :] = (acc[0:HEAD_DIM] / acc[HEAD_DIM:HEAD_DIM + 1]).T.astype(BF16)


def _moba_attention(p, batch, seq):
    assert seq % MOBA_BLOCK == 0
    col = lambda c0: pl.BlockSpec((seq, LANES), lambda b, h: (b, c0 + h))
    return pl.pallas_call(
        functools.partial(_moba_kernel, nb=seq // MOBA_BLOCK),
        grid=(batch, MOBA_HEADS),
        in_specs=[col(COL_MQ), col(COL_MK), col(COL_MV)],
        out_specs=pl.BlockSpec((seq, LANES), lambda b, h: (b, h)),
        out_shape=jax.ShapeDtypeStruct((batch * seq, MOBA_HEADS * LANES), BF16),
        scratch_shapes=[pltpu.VMEM((V_ROWS, seq), BF16)],
        compiler_params=_cparams(("parallel", "parallel")),
        name="moba_attn",
    )(p, p, p)


def _gla_constants():
    c = GLA_CHUNK
    idx = np.arange(c)
    mats = [np.tril(np.ones((c, c), np.float32))]
    bm = []
    for lvl in range(6):
        m = 32 >> lvl
        parent, half = idx // (2 * m), (idx // m) % 2
        ref = parent * 2 * m + m - 1
        t = idx[None, :]
        second = (half == 1)[:, None] & (t > ref[:, None]) & (t <= idx[:, None])
        first = (half == 0)[:, None] & (t > idx[:, None]) & (t <= ref[:, None])
        mats.append((second | first).astype(np.float32))
        bm.append(((parent[:, None] == parent[None, :]) & (half == 1)[:, None] & (half == 0)[None, :]).astype(np.float32))
    bm.append(np.eye(c, dtype=np.float32))
    cmat = np.concatenate(mats, axis=0)
    return jnp.asarray(cmat, BF16), jnp.asarray(np.concatenate(bm, axis=0), F32)


def _split2(x):
    hi = x.astype(BF16)
    return hi, (x - hi.astype(F32)).astype(BF16)


def _gla_kernel(cmat_ref, bm_ref, up_ref, bias_ref, gn_ref, gg_ref, q_ref, k_ref, v_ref, r_ref, o_ref,
                la_ref, st_ref, *, seq, heads):
    c = GLA_CHUNK
    up_hi, up_lo = _split2(up_ref[...])
    rows = 512 if seq % 512 == 0 else seq
    for r0 in range(0, seq, rows):
        g_hi, g_lo = _split2(gg_ref[r0:r0 + rows, :])
        z = _dot(g_hi, up_hi) + _dot(g_hi, up_lo) + _dot(g_lo, up_hi) + bias_ref[...]
        la_ref[r0:r0 + rows, :] = (jnp.minimum(z, 0.0) - jnp.log(1.0 + jnp.exp(-jnp.abs(z)))) * (1.0 / GLA_TAU)

    st_ref[...] = jnp.zeros_like(st_ref)
    cmat = cmat_ref[...]

    def chunk(ci, carry):
        sl = pl.ds(pl.multiple_of(ci * c, c), c)
        g_hi, g_lo = _split2(la_ref[sl, :])
        e_all = _dot(cmat, g_hi) + _dot(cmat, g_lo)
        for h in range(heads):
            hc = _head_cols(h)
            e = e_all[:, hc]
            cum = e[0:c]
            q = q_ref[sl, hc].astype(F32)
            k = k_ref[sl, hc].astype(F32)
            v = v_ref[sl, hc]
            att = bm_ref[6 * c:7 * c, :] * _dot_nt(q.astype(BF16), k.astype(BF16))
            for lvl in range(6):
                w = jnp.exp(e[(1 + lvl) * c:(2 + lvl) * c])
                att = att + bm_ref[lvl * c:(lvl + 1) * c, :] * _dot_nt((q * w).astype(BF16), (k * w).astype(BF16))
            st = st_ref[h]
            qe = (q * jnp.exp(cum)).astype(BF16)
            o = _dot_nt(qe, st.astype(BF16)) + _dot(att.astype(BF16), v)
            last = cum[c - 1:c, :]
            kd = (k * jnp.exp(last - cum)).astype(BF16)
            st_ref[h] = st * jnp.exp(last) + _dot(v.astype(F32).T.astype(BF16), kd)
            o = o * lax.rsqrt(jnp.mean(o * o, axis=-1, keepdims=True) + LN_EPS) * gn_ref[...]
            r = r_ref[sl, hc].astype(F32)
            o_ref[sl, hc] = (o * (r * jax.nn.sigmoid(r))).astype(BF16)
        return carry

    lax.fori_loop(0, seq // c, chunk, 0)


def _gla(p, gg, up, bias, gn, batch, seq):
    cmat, bm = _gla_constants()
    heads = GLA_HEADS
    w = heads * LANES
    up_p = jnp.pad(up.reshape(GLA_GATE_RANK, heads, GLA_DK),
                   ((0, GG_WIDTH - GLA_GATE_RANK), (0, 0), (0, LANES - GLA_DK))).reshape(GG_WIDTH, w)
    bias_p = jnp.pad(bias.reshape(heads, GLA_DK), ((0, 0), (0, LANES - GLA_DK))).reshape(1, w)
    const = lambda shape: pl.BlockSpec(shape, lambda b: (0, 0))
    col = lambda c0: pl.BlockSpec((seq, w), lambda b: (b, c0 // heads))
    return pl.pallas_call(
        functools.partial(_gla_kernel, seq=seq, heads=heads),
        grid=(batch,),
        in_specs=[const(cmat.shape), const(bm.shape), const((GG_WIDTH, w)), const((1, w)), const((1, LANES)),
                  pl.BlockSpec((seq, GG_WIDTH), lambda b: (b, 0)),
                  col(COL_GQ), col(COL_GK), col(COL_GV), col(COL_GR)],
        out_specs=pl.BlockSpec((seq, w), lambda b: (b, 0)),
        out_shape=jax.ShapeDtypeStruct((batch * seq, w), BF16),
        scratch_shapes=[pltpu.VMEM((seq, w), F32), pltpu.VMEM((heads, GLA_DV, LANES), F32)],
        compiler_params=_cparams(("parallel",)),
        name="gla",
    )(cmat, bm, up_p, bias_p, gn, gg, p, p, p, p)


def _out_kernel(a_ref, w_ref, xf_ref, g_ref, b_ref, of_ref, ob_ref, acc_ref, *, nk):
    k = pl.program_id(1)

    @pl.when(k == 0)
    def _():
        acc_ref[...] = jnp.zeros_like(acc_ref)

    acc_ref[...] += _dot(a_ref[...], w_ref[...])

    @pl.when(k == nk - 1)
    def _():
        out = _layer_norm(DEEPNORM_ALPHA * xf_ref[...] + acc_ref[...], g_ref[...], b_ref[...])
        of_ref[...] = out
        ob_ref[...] = out.astype(BF16)


def _out_proj(a, w, xf, g, b):
    t, d = xf.shape
    kdim = a.shape[1]
    tm = 512 if t % 512 == 0 else t
    tk = 512 if kdim % 512 == 0 else kdim
    nk = kdim // tk
    row = pl.BlockSpec((tm, d), lambda i, k: (i, 0))
    vec = pl.BlockSpec((1, d), lambda i, k: (0, 0))
    return pl.pallas_call(
        functools.partial(_out_kernel, nk=nk),
        grid=(t // tm, nk),
        in_specs=[pl.BlockSpec((tm, tk), lambda i, k: (i, k)),
                  pl.BlockSpec((tk, d), lambda i, k: (k, 0)),
                  row, vec, vec],
        out_specs=[row, row],
        out_shape=[jax.ShapeDtypeStruct((t, d), F32), jax.ShapeDtypeStruct((t, d), BF16)],
        scratch_shapes=[pltpu.VMEM((tm, d), F32)],
        compiler_params=_cparams(("parallel", "arbitrary")),
        name="out_proj",
    )(a, w, xf, g, b)


def _ple_kernel(xb_ref, wg_ref, pb_ref, we_ref, xf_ref, g_ref, b_ref, of_ref, ob_ref, acc_ref, *, nk):
    k = pl.program_id(1)

    @pl.when(k == 0)
    def _():
        acc_ref[...] = jnp.zeros_like(acc_ref)

    acc_ref[...] += _dot(xb_ref[...], wg_ref[...])

    @pl.when(k == nk - 1)
    def _():
        e = _dot(pb_ref[...], we_ref[...]) * jax.nn.sigmoid(acc_ref[...])
        out = _layer_norm(DEEPNORM_ALPHA * xf_ref[...] + e, g_ref[...], b_ref[...])
        of_ref[...] = out
        ob_ref[...] = out.astype(BF16)


def _ple(xf, xb, pb, w_pe, w_pg, g, b):
    t, d = xf.shape
    pdim = pb.shape[1]
    tm = 512 if t % 512 == 0 else t
    tk = 512 if d % 512 == 0 else d
    nk = d // tk
    row = pl.BlockSpec((tm, d), lambda i, k: (i, 0))
    vec = pl.BlockSpec((1, d), lambda i, k: (0, 0))
    return pl.pallas_call(
        functools.partial(_ple_kernel, nk=nk),
        grid=(t // tm, nk),
        in_specs=[pl.BlockSpec((tm, tk), lambda i, k: (i, k)),
                  pl.BlockSpec((tk, d), lambda i, k: (k, 0)),
                  pl.BlockSpec((tm, pdim), lambda i, k: (i, 0)),
                  pl.BlockSpec((pdim, d), lambda i, k: (0, 0)),
                  row, vec, vec],
        out_specs=[row, row],
        out_shape=[jax.ShapeDtypeStruct((t, d), F32), jax.ShapeDtypeStruct((t, d), BF16)],
        scratch_shapes=[pltpu.VMEM((tm, d), F32)],
        compiler_params=_cparams(("parallel", "arbitrary")),
        name="ple",
    )(xb, w_pg, pb, w_pe, xf, g, b)


def kernel(x, p, positions, w_in, w_out, diff_lambda, diff_norm_g, gla_gate_up, gla_gate_b, gla_norm_g,
           ffn1_gate, ffn1_up, ffn1_down, ffn2_gate, ffn2_up, ffn2_down, w_pe, w_pg, ln_g, ln_b):
    batch, seq, d = x.shape
    t = batch * seq
    assert d == (DIFF_HEADS + GLA_HEADS + MOBA_HEADS) * HEAD_DIM and seq % MOBA_BLOCK == 0
    xf = x.reshape(t, d)
    xb = xf.astype(BF16)
    tables = _rope_tables(positions.reshape(t, 1))
    bf = lambda w: w.astype(BF16)
    vec = lambda v: v.reshape(1, -1)
    for i in range(DEPTH):
        xf, xb = _ffn(xf, xb, bf(ffn1_gate[i]), bf(ffn1_up[i]), bf(ffn1_down[i]), vec(ln_g[i, 0]), vec(ln_b[i, 0]))
        proj, gg = _proj(xb, _reorder_w_in(w_in[i]), tables)
        o_diff = _diff_attention(proj, diff_lambda[i], vec(diff_norm_g[i]), batch, seq, i)
        o_gla = _gla(proj, gg, gla_gate_up[i], gla_gate_b[i], vec(gla_norm_g[i]), batch, seq)
        o_moba = _moba_attention(proj, batch, seq)
        mix = jnp.concatenate([o_diff, o_gla, o_moba], axis=1)
        xf, xb = _out_proj(mix, bf(w_out[i]), xf, vec(ln_g[i, 1]), vec(ln_b[i, 1]))
        xf, xb = _ffn(xf, xb, bf(ffn2_gate[i]), bf(ffn2_up[i]), bf(ffn2_down[i]), vec(ln_g[i, 2]), vec(ln_b[i, 2]))
        xf, xb = _ple(xf, xb, bf(p[i].reshape(t, -1)), bf(w_pe[i]), bf(w_pg[i]), vec(ln_g[i, 3]), vec(ln_b[i, 3]))
    return xf.reshape(batch, seq, d)
```

```python
import functools
import math

import numpy as np
import jax
import jax.numpy as jnp
from jax import lax
from jax.experimental import pallas as pl
from jax.experimental.pallas import tpu as pltpu

F32 = jnp.float32
BF16 = jnp.bfloat16

DEPTH = 2
HEAD_DIM = 128
DIFF_HEADS = 6
GLA_HEADS = 4
MOBA_HEADS = 6
DIFF_QK_DIM = 64
GLA_DK = 64
GLA_DV = 128
GLA_GATE_RANK = 16
GLA_TAU = 16.0
GLA_CHUNK = 64
GLA_UNROLL = 2
MOBA_BLOCK = 256
MOBA_TOPK = 3
ROPE_THETA = 10000.0
LN_EPS = 1e-5
DEEPNORM_ALPHA = (2 * DEPTH) ** 0.25

LANES = 128
SUBLANES = 8
VMEM_LIMIT_BYTES = 56 * 1024 * 1024

PROJ_TN = 256
_CB = LANES
COL_DQ = 0
COL_DK = COL_DQ + DIFF_HEADS
COL_MQ = COL_DK + DIFF_HEADS
COL_MK = COL_MQ + MOBA_HEADS
COL_DV = COL_MK + MOBA_HEADS
COL_MV = COL_DV + DIFF_HEADS
COL_GQ = COL_MV + MOBA_HEADS
COL_GK = COL_GQ + GLA_HEADS
COL_GV = COL_GK + GLA_HEADS
COL_GR = COL_GV + GLA_HEADS
COL_GG = COL_GR + GLA_HEADS
GG_WIDTH = 2 * _CB
N_COL_BLOCKS = COL_GG + 2
PROJ_WIDTH = N_COL_BLOCKS * _CB
assert PROJ_WIDTH % PROJ_TN == 0

NEG_INF = float("-inf")
LOG2_E = math.log2(math.e)
V_ROWS = HEAD_DIM + 16


def _cparams(sem):
    return pltpu.CompilerParams(dimension_semantics=sem, vmem_limit_bytes=VMEM_LIMIT_BYTES)


def _layer_norm(y, g, b):
    mu = jnp.mean(y, axis=-1, keepdims=True)
    d = y - mu
    var = jnp.mean(d * d, axis=-1, keepdims=True)
    return d * lax.rsqrt(var + LN_EPS) * g + b


def _dot_nt(a, b):
    return lax.dot_general(a, b, (((1,), (1,)), ((), ())), preferred_element_type=F32)


def _dot(a, b):
    return jnp.dot(a, b, preferred_element_type=F32)


def _head_cols(g):
    return slice(g * LANES, (g + 1) * LANES)


def _rope_table_kernel(pos_ref, inv_a_ref, inv_b_ref, sgn_a_ref, sgn_b_ref,
                       cos_a_ref, sin_a_ref, cos_b_ref, sin_b_ref):
    pos = pos_ref[...].astype(F32)
    ang_a = pos * inv_a_ref[...]
    ang_b = pos * inv_b_ref[...]
    cos_a_ref[...] = jnp.cos(ang_a)
    sin_a_ref[...] = jnp.sin(ang_a) * sgn_a_ref[...]
    cos_b_ref[...] = jnp.cos(ang_b)
    sin_b_ref[...] = jnp.sin(ang_b) * sgn_b_ref[...]


def _rope_tables(pos):
    t = pos.shape[0]
    tm = next(c for c in (1024, 512, MOBA_BLOCK) if t % c == 0)

    def pattern(d):
        inv = ROPE_THETA ** (-jnp.arange(0, d, 2, dtype=F32) / d)
        inv = jnp.tile(inv, 2 * LANES // d)[None, :]
        sgn = jnp.tile(jnp.concatenate([-jnp.ones(d // 2, F32), jnp.ones(d // 2, F32)]), LANES // d)[None, :]
        return inv, sgn

    inv_a, sgn_a = pattern(DIFF_QK_DIM)
    inv_b, sgn_b = pattern(HEAD_DIM)
    row = pl.BlockSpec((1, LANES), lambda i: (0, 0))
    out = pl.BlockSpec((tm, LANES), lambda i: (i, 0))
    return pl.pallas_call(
        _rope_table_kernel,
        grid=(t // tm,),
        in_specs=[pl.BlockSpec((tm, 1), lambda i: (i, 0)), row, row, row, row],
        out_specs=[out, out, out, out],
        out_shape=[jax.ShapeDtypeStruct((t, LANES), F32)] * 4,
        compiler_params=_cparams(("parallel",)),
        name="rope_tables",
    )(pos, inv_a, inv_b, sgn_a, sgn_b)


def _ffn_kernel(xf_ref, xb_ref, wg_ref, wu_ref, wd_ref, g_ref, b_ref, of_ref, ob_ref, acc_ref, *, nf):
    j = pl.program_id(1)

    @pl.when(j == 0)
    def _():
        acc_ref[...] = jnp.zeros_like(acc_ref)

    x = xb_ref[...]
    gt = _dot(x, wg_ref[0])
    ut = _dot(x, wu_ref[0])
    h = (gt * jax.nn.sigmoid(gt) * ut).astype(BF16)
    acc_ref[...] += _dot(h, wd_ref[0])

    @pl.when(j == nf - 1)
    def _():
        y = DEEPNORM_ALPHA * xf_ref[...] + 0.5 * acc_ref[...]
        out = _layer_norm(y, g_ref[...], b_ref[...])
        of_ref[...] = out
        ob_ref[...] = out.astype(BF16)


def _ffn_tiles(t, f):
    tm = 512 if t % 512 == 0 else t
    tf = 512 if f % 512 == 0 else f
    return tm, tf


def _ffn(xf, xb, wg, wu, wd, layer, g, b):
    t, d = xf.shape
    f = wg.shape[2]
    tm, tf = _ffn_tiles(t, f)
    nf = f // tf
    row = pl.BlockSpec((tm, d), lambda i, j: (i, 0))
    vec = pl.BlockSpec((1, d), lambda i, j: (0, 0))
    return pl.pallas_call(
        functools.partial(_ffn_kernel, nf=nf),
        grid=(t // tm, nf),
        in_specs=[row, row,
                  pl.BlockSpec((1, d, tf), lambda i, j: (layer, 0, j)),
                  pl.BlockSpec((1, d, tf), lambda i, j: (layer, 0, j)),
                  pl.BlockSpec((1, tf, d), lambda i, j: (layer, j, 0)),
                  vec, vec],
        out_specs=[row, row],
        out_shape=[jax.ShapeDtypeStruct((t, d), F32), jax.ShapeDtypeStruct((t, d), BF16)],
        scratch_shapes=[pltpu.VMEM((tm, d), F32)],
        compiler_params=_cparams(("parallel", "arbitrary")),
        name="ffn",
    )(xf, xb, wg, wu, wd, g, b)


def _proj_rope_kernel(xb_ref, w_ref, cos_ref, sin_ref, p_ref, *, rot_dim, q_scale, q_tiles):
    j = pl.program_id(1)
    acc = _dot(xb_ref[...], w_ref[0])
    scale = jnp.where(j < q_tiles, q_scale, 1.0).astype(F32)
    half = rot_dim // 2
    for hh in range(PROJ_TN // LANES):
        t = acc[:, _head_cols(hh)]
        if rot_dim == LANES:
            rot = pltpu.roll(t, half, 1)
        else:
            lane = lax.broadcasted_iota(jnp.int32, t.shape, 1)
            rot = jnp.where((lane // half) % 2 == 0, pltpu.roll(t, LANES - half, 1), pltpu.roll(t, half, 1))
        p_ref[:, _head_cols(hh)] = ((t * cos_ref[...] + rot * sin_ref[...]) * scale).astype(BF16)


def _proj_plain_kernel(xb_ref, w_ref, p_ref, gg_ref, *, nj, gq_tiles):
    j = pl.program_id(1)
    acc = _dot(xb_ref[...], w_ref[0])
    scale = jnp.where((j >= gq_tiles[0]) & (j < gq_tiles[1]), GLA_DK ** -0.5, 1.0).astype(F32)
    p_ref[...] = (acc * scale).astype(BF16)

    @pl.when(j == nj - 1)
    def _():
        gg_ref[...] = acc


def _proj(xb, w, layer, tables):
    t, d = xb.shape
    tm = 2048 if t % 2048 == 0 else t
    tiles = lambda c: c * LANES // PROJ_TN
    cos_a, sin_a, cos_b, sin_b = tables
    x_spec = pl.BlockSpec((tm, d), lambda i, j: (i, 0))
    tab = pl.BlockSpec((tm, LANES), lambda i, j: (i, 0))
    out = pl.BlockSpec((tm, PROJ_TN), lambda i, j: (i, j))
    w_spec = lambda first: pl.BlockSpec((1, d, PROJ_TN), lambda i, j: (layer, 0, first + j))

    def rope_group(first_col, q_heads, n_heads, rot_dim, q_scale, cos, sin, name):
        nj = tiles(n_heads)
        return pl.pallas_call(
            functools.partial(_proj_rope_kernel, rot_dim=rot_dim, q_scale=q_scale, q_tiles=tiles(q_heads)),
            grid=(t // tm, nj),
            in_specs=[x_spec, w_spec(tiles(first_col)), tab, tab],
            out_specs=out,
            out_shape=jax.ShapeDtypeStruct((t, nj * PROJ_TN), BF16),
            compiler_params=_cparams(("parallel", "arbitrary")),
            name=name,
        )(xb, w, cos, sin)

    pa = rope_group(COL_DQ, DIFF_HEADS, 2 * DIFF_HEADS, DIFF_QK_DIM, DIFF_QK_DIM ** -0.5 * LOG2_E, cos_a, sin_a,
                    "in_proj_diff")
    pb = rope_group(COL_MQ, MOBA_HEADS, 2 * MOBA_HEADS, HEAD_DIM, HEAD_DIM ** -0.5 * LOG2_E, cos_b, sin_b,
                    "in_proj_moba")
    nj = tiles(N_COL_BLOCKS - COL_DV)
    pc, gg = pl.pallas_call(
        functools.partial(_proj_plain_kernel, nj=nj, gq_tiles=(tiles(COL_GQ - COL_DV), tiles(COL_GK - COL_DV))),
        grid=(t // tm, nj),
        in_specs=[x_spec, w_spec(tiles(COL_DV))],
        out_specs=[out, pl.BlockSpec((tm, GG_WIDTH), lambda i, j: (i, 0))],
        out_shape=[jax.ShapeDtypeStruct((t, nj * PROJ_TN), BF16), jax.ShapeDtypeStruct((t, GG_WIDTH), F32)],
        compiler_params=_cparams(("parallel", "arbitrary")),
        name="in_proj_plain",
    )(xb, w)
    return pa, pb, pc, gg


_IN_WIDTHS = (DIFF_HEADS * 128, DIFF_HEADS * 128, DIFF_HEADS * 128, GLA_HEADS * GLA_DK, GLA_HEADS * GLA_DK,
              GLA_HEADS * GLA_DV, GLA_HEADS * GLA_DV, GLA_GATE_RANK, MOBA_HEADS * 128, MOBA_HEADS * 128,
              MOBA_HEADS * 128)
_IN_OFFS = tuple(int(v) for v in np.concatenate([[0], np.cumsum(_IN_WIDTHS)]))
D_IN = _IN_OFFS[-1]


def _reorder_w_in_kernel(w_ref, o_ref):
    src = dict(zip(("dq", "dk", "dv", "gq", "gk", "gv", "gr", "gg", "mq", "mk", "mv"), _IN_OFFS[:-1]))
    o_ref[...] = jnp.zeros_like(o_ref)

    def put(dst_block, name, width, src_off=0):
        s0 = src[name] + src_off
        o_ref[0, :, dst_block * LANES:dst_block * LANES + width] = w_ref[0, :, s0:s0 + width].astype(BF16)

    for name, col, heads in (("dq", COL_DQ, DIFF_HEADS), ("dk", COL_DK, DIFF_HEADS), ("mq", COL_MQ, MOBA_HEADS),
                             ("mk", COL_MK, MOBA_HEADS), ("dv", COL_DV, DIFF_HEADS), ("mv", COL_MV, MOBA_HEADS),
                             ("gv", COL_GV, GLA_HEADS), ("gr", COL_GR, GLA_HEADS)):
        put(col, name, heads * LANES)
    for h in range(GLA_HEADS):
        put(COL_GQ + h, "gq", GLA_DK, h * GLA_DK)
        put(COL_GK + h, "gk", GLA_DK, h * GLA_DK)
    put(COL_GG, "gg", GLA_GATE_RANK)


def _reorder_w_in(w_in):
    depth, d, d_in = w_in.shape
    assert d_in == D_IN
    tr = 128 if d % 128 == 0 else d
    return pl.pallas_call(
        _reorder_w_in_kernel,
        grid=(depth, d // tr),
        in_specs=[pl.BlockSpec((1, tr, d_in), lambda l, r: (l, r, 0))],
        out_specs=pl.BlockSpec((1, tr, PROJ_WIDTH), lambda l, r: (l, r, 0)),
        out_shape=jax.ShapeDtypeStruct((depth, d, PROJ_WIDTH), BF16),
        compiler_params=_cparams(("parallel", "parallel")),
        name="reorder_w_in",
    )(w_in)


def _cast_kernel(x_ref, o_ref):
    o_ref[...] = x_ref[...].astype(o_ref.dtype)


def _cast_bf16(x):
    l, r, c = x.shape
    tr = next((t for t in (1024, 512, 256, 128) if r % t == 0 and t * c * 4 <= (4 << 20)), r)
    spec = pl.BlockSpec((1, tr, c), lambda i, j: (i, j, 0))
    return pl.pallas_call(
        _cast_kernel,
        grid=(l, r // tr),
        in_specs=[spec],
        out_specs=spec,
        out_shape=jax.ShapeDtypeStruct(x.shape, BF16),
        compiler_params=_cparams(("parallel", "parallel")),
        name="cast_bf16",
    )(x)


def _store_v_transposed(v_ref, vt_ref, tk):
    seq = v_ref.shape[0]
    for blk in range(seq // tk):
        vt_ref[0:HEAD_DIM, blk * tk:(blk + 1) * tk] = v_ref[blk * tk:(blk + 1) * tk, :].astype(F32).T.astype(BF16)
    vt_ref[HEAD_DIM:V_ROWS, :] = jnp.ones((V_ROWS - HEAD_DIM, seq), BF16)


def _col_max(parts):
    m = jnp.max(parts[0], axis=0, keepdims=True)
    for s in parts[1:]:
        m = jnp.maximum(m, jnp.max(s, axis=0, keepdims=True))
    return m


def _diff_kernel(lam_ref, g_ref, q_ref, k_ref, v_ref, o_ref, vt_ref, *, tq, lam_init):
    seq = k_ref.shape[0]
    _store_v_transposed(v_ref, vt_ref, tq)
    lf = lam_ref[...]
    lam = (jnp.exp(jnp.sum(lf[0:1] * lf[1:2], axis=1, keepdims=True))
           - jnp.exp(jnp.sum(lf[2:3] * lf[3:4], axis=1, keepdims=True)) + lam_init)
    lane = lax.broadcasted_iota(jnp.int32, (tq, LANES), 1)
    causal = (lax.broadcasted_iota(jnp.int32, (tq, 2 * tq), 0)
              <= lax.broadcasted_iota(jnp.int32, (tq, 2 * tq), 1) % tq)
    for qi in range(seq // tq):
        lo, hi = qi * tq, (qi + 1) * tq
        q = q_ref[lo:hi, :].astype(F32)
        qq = jnp.concatenate([jnp.where(lane < DIFF_QK_DIM, q, 0.0),
                              jnp.where(lane >= DIFF_QK_DIM, q, 0.0)], axis=0).astype(BF16)
        parts = [jnp.where(causal, _dot_nt(k_ref[lo:hi, :], qq), NEG_INF)]
        if qi > 0:
            parts.append(_dot_nt(k_ref[0:lo, :], qq))
        m = _col_max(parts)
        acc = _dot(vt_ref[:, lo:hi], jnp.exp2(parts[0] - m).astype(BF16))
        if qi > 0:
            acc = acc + _dot(vt_ref[:, 0:lo], jnp.exp2(parts[1] - m).astype(BF16))
        o_t = acc[0:HEAD_DIM] / acc[HEAD_DIM:HEAD_DIM + 1]
        o = (o_t[:, :tq] - lam * o_t[:, tq:]).T
        o = o * lax.rsqrt(jnp.mean(o * o, axis=-1, keepdims=True) + LN_EPS) * g_ref[...] * (1.0 - lam_init)
        o_ref[lo:hi, :] = o.astype(BF16)


def _diff_attention(pa, pc, lam, g, batch, seq, layer):
    tq = 256
    lam_init = 0.8 - 0.6 * math.exp(-0.3 * layer)
    col = lambda c0: pl.BlockSpec((seq, LANES), lambda b, h: (b, c0 + h))
    return pl.pallas_call(
        functools.partial(_diff_kernel, tq=tq, lam_init=lam_init),
        grid=(batch, DIFF_HEADS),
        in_specs=[pl.BlockSpec((4, DIFF_QK_DIM), lambda b, h: (0, 0)),
                  pl.BlockSpec((1, LANES), lambda b, h: (0, 0)),
                  col(0), col(DIFF_HEADS), col(0)],
        out_specs=pl.BlockSpec((seq, LANES), lambda b, h: (b, h)),
        out_shape=jax.ShapeDtypeStruct((batch * seq, DIFF_HEADS * LANES), BF16),
        scratch_shapes=[pltpu.VMEM((V_ROWS, seq), BF16)],
        compiler_params=_cparams(("parallel", "parallel")),
        name="diff_attn",
    )(lam, g, pa, pa, pc)


def _moba_kernel(q_ref, k_ref, v_ref, o_ref, vt_ref, *, nb):
    tq = MOBA_BLOCK
    _store_v_transposed(v_ref, vt_ref, tq)
    nbp = -(-nb // SUBLANES) * SUBLANES
    rows = [jnp.mean(k_ref[n * tq:(n + 1) * tq, :].astype(F32), axis=0, keepdims=True) for n in range(nb)]
    if nbp > nb:
        rows.append(jnp.zeros((nbp - nb, LANES), F32))
    km = jnp.concatenate(rows, axis=0)
    km_hi = km.astype(BF16)
    km_lo = (km - km_hi.astype(F32)).astype(BF16)
    causal = (lax.broadcasted_iota(jnp.int32, (tq, tq), 0) <= lax.broadcasted_iota(jnp.int32, (tq, tq), 1))
    for qi in range(nb):
        lo, hi = qi * tq, (qi + 1) * tq
        q = q_ref[lo:hi, :]
        parts = [jnp.where(causal, _dot_nt(k_ref[lo:hi, :], q), NEG_INF)]
        if qi > 0:
            s_past = _dot_nt(k_ref[0:lo, :], q)
            if qi > MOBA_TOPK:
                gate = _dot_nt(km_hi, q) + _dot_nt(km_lo, q)
                n_idx = lax.broadcasted_iota(jnp.int32, gate.shape, 0)
                rank = jnp.zeros(gate.shape, jnp.int32)
                for mb in range(qi):
                    gm = gate[mb:mb + 1, :]
                    rank = rank + jnp.where((gm > gate) | ((gm == gate) & (mb < n_idx)), 1, 0)
                bias = jnp.where(rank < MOBA_TOPK, 0.0, NEG_INF)
                s_past = jnp.concatenate([s_past[n * tq:(n + 1) * tq] + bias[n:n + 1, :] for n in range(qi)], axis=0)
            parts.append(s_past)
        m = _col_max(parts)
        acc = _dot(vt_ref[:, lo:hi], jnp.exp2(parts[0] - m).astype(BF16))
        if qi > 0:
            acc = acc + _dot(vt_ref[:, 0:lo], jnp.exp2(parts[1] - m).astype(BF16))
        o_ref[lo:hi, :] = (acc[0:HEAD_DIM] / acc[HEAD_DIM:HEAD_DIM + 1]).T.astype(BF16)


def _moba_attention(pb, pc, batch, seq):
    assert seq % MOBA_BLOCK == 0
    col = lambda c0: pl.BlockSpec((seq, LANES), lambda b, h: (b, c0 + h))
    return pl.pallas_call(
        functools.partial(_moba_kernel, nb=seq // MOBA_BLOCK),
        grid=(batch, MOBA_HEADS),
        in_specs=[col(0), col(MOBA_HEADS), col(COL_MV - COL_DV)],
        out_specs=pl.BlockSpec((seq, LANES), lambda b, h: (b, h)),
        out_shape=jax.ShapeDtypeStruct((batch * seq, MOBA_HEADS * LANES), BF16),
        scratch_shapes=[pltpu.VMEM((V_ROWS, seq), BF16)],
        compiler_params=_cparams(("parallel", "parallel")),
        name="moba_attn",
    )(pb, pb, pc)


def _gla_constants():
    c = GLA_CHUNK
    idx = np.arange(c)
    mats = [np.tril(np.ones((c, c), np.float32))]
    bm = []
    for lvl in range(6):
        m = 32 >> lvl
        parent, half = idx // (2 * m), (idx // m) % 2
        ref = parent * 2 * m + m - 1
        t = idx[None, :]
        second = (half == 1)[:, None] & (t > ref[:, None]) & (t <= idx[:, None])
        first = (half == 0)[:, None] & (t > idx[:, None]) & (t <= ref[:, None])
        mats.append((second | first).astype(np.float32))
        bm.append(((parent[:, None] == parent[None, :]) & (half == 1)[:, None] & (half == 0)[None, :]).astype(np.float32))
    bm.append(np.eye(c, dtype=np.float32))
    cmat = np.concatenate(mats, axis=0)
    return jnp.asarray(cmat, BF16), jnp.asarray(np.concatenate(bm, axis=0), F32)


def _split2(x):
    hi = x.astype(BF16)
    return hi, (x - hi.astype(F32)).astype(BF16)


def _gla_kernel(cmat_ref, bm_ref, up_ref, bias_ref, gn_ref, gg_ref, q_ref, k_ref, v_ref, r_ref, o_ref,
                la_ref, st_ref, *, seq, heads):
    c = GLA_CHUNK
    up_hi, up_lo = _split2(up_ref[...])
    rows = 512 if seq % 512 == 0 else seq
    for r0 in range(0, seq, rows):
        g_hi, g_lo = _split2(gg_ref[r0:r0 + rows, :])
        z = _dot(g_hi, up_hi) + _dot(g_hi, up_lo) + _dot(g_lo, up_hi) + bias_ref[...]
        la_ref[r0:r0 + rows, :] = (jnp.minimum(z, 0.0) - jnp.log(1.0 + jnp.exp(-jnp.abs(z)))) * (1.0 / GLA_TAU)

    st_ref[...] = jnp.zeros_like(st_ref)
    cmat = cmat_ref[...]

    def chunk_pair(cp, carry):
        for u in range(GLA_UNROLL):
            one_chunk(cp * GLA_UNROLL + u)
        return carry

    def one_chunk(ci):
        sl = pl.ds(pl.multiple_of(ci * c, c), c)
        g_hi, g_lo = _split2(la_ref[sl, :])
        e_all = _dot(cmat, g_hi) + _dot(cmat, g_lo)
        for h in range(heads):
            hc = _head_cols(h)
            e = e_all[:, hc]
            cum = e[0:c]
            q = q_ref[sl, hc].astype(F32)
            k = k_ref[sl, hc].astype(F32)
            v = v_ref[sl, hc]
            att = bm_ref[6 * c:7 * c, :] * _dot_nt(q.astype(BF16), k.astype(BF16))
            for lvl in range(6):
                w = jnp.exp(e[(1 + lvl) * c:(2 + lvl) * c])
                att = att + bm_ref[lvl * c:(lvl + 1) * c, :] * _dot_nt((q * w).astype(BF16), (k * w).astype(BF16))
            st = st_ref[h]
            qe = (q * jnp.exp(cum)).astype(BF16)
            o = _dot_nt(qe, st.astype(BF16)) + _dot(att.astype(BF16), v)
            last = cum[c - 1:c, :]
            kd = (k * jnp.exp(last - cum)).astype(BF16)
            st_ref[h] = st * jnp.exp(last) + _dot(v.astype(F32).T.astype(BF16), kd)
            o = o * lax.rsqrt(jnp.mean(o * o, axis=-1, keepdims=True) + LN_EPS) * gn_ref[...]
            r = r_ref[sl, hc].astype(F32)
            o_ref[sl, hc] = (o * (r * jax.nn.sigmoid(r))).astype(BF16)

    assert (seq // c) % GLA_UNROLL == 0
    lax.fori_loop(0, seq // c // GLA_UNROLL, chunk_pair, 0)


def _gla(p, gg, up, bias, gn, batch, seq):
    cmat, bm = _gla_constants()
    heads = GLA_HEADS
    w = heads * LANES
    up_p = jnp.pad(up.reshape(GLA_GATE_RANK, heads, GLA_DK),
                   ((0, GG_WIDTH - GLA_GATE_RANK), (0, 0), (0, LANES - GLA_DK))).reshape(GG_WIDTH, w)
    bias_p = jnp.pad(bias.reshape(heads, GLA_DK), ((0, 0), (0, LANES - GLA_DK))).reshape(1, w)
    const = lambda shape: pl.BlockSpec(shape, lambda b: (0, 0))
    col = lambda c0: pl.BlockSpec((seq, w), lambda b: (b, (c0 - COL_DV) // heads))
    return pl.pallas_call(
        functools.partial(_gla_kernel, seq=seq, heads=heads),
        grid=(batch,),
        in_specs=[const(cmat.shape), const(bm.shape), const((GG_WIDTH, w)), const((1, w)), const((1, LANES)),
                  pl.BlockSpec((seq, GG_WIDTH), lambda b: (b, 0)),
                  col(COL_GQ), col(COL_GK), col(COL_GV), col(COL_GR)],
        out_specs=pl.BlockSpec((seq, w), lambda b: (b, 0)),
        out_shape=jax.ShapeDtypeStruct((batch * seq, w), BF16),
        scratch_shapes=[pltpu.VMEM((seq, w), F32), pltpu.VMEM((heads, GLA_DV, LANES), F32)],
        compiler_params=_cparams(("parallel",)),
        name="gla",
    )(cmat, bm, up_p, bias_p, gn, gg, p, p, p, p)


def _emit_stream(of_ref, ob_ref, y, g_ref, b_ref):
    out = _layer_norm(y, g_ref[...], b_ref[...])
    of_ref[...] = out
    ob_ref[...] = out.astype(BF16)


def _out_kernel(od_ref, og_ref, om_ref, w_ref, xf_ref, g_ref, b_ref, of_ref, ob_ref):
    k0 = od_ref.shape[1]
    k1 = k0 + og_ref.shape[1]
    y = (_dot(od_ref[...], w_ref[0, 0:k0, :]) + _dot(og_ref[...], w_ref[0, k0:k1, :])
         + _dot(om_ref[...], w_ref[0, k1:, :]))
    _emit_stream(of_ref, ob_ref, DEEPNORM_ALPHA * xf_ref[...] + y, g_ref, b_ref)


def _row_tile(t):
    return 512 if t % 512 == 0 else t


def _out_proj(o_diff, o_gla, o_moba, w, layer, xf, g, b):
    t, d = xf.shape
    tm = _row_tile(t)
    row = pl.BlockSpec((tm, d), lambda i: (i, 0))
    vec = pl.BlockSpec((1, d), lambda i: (0, 0))
    part = lambda a: pl.BlockSpec((tm, a.shape[1]), lambda i: (i, 0))
    return pl.pallas_call(
        _out_kernel,
        grid=(t // tm,),
        in_specs=[part(o_diff), part(o_gla), part(o_moba),
                  pl.BlockSpec((1,) + w.shape[1:], lambda i: (layer, 0, 0)),
                  row, vec, vec],
        out_specs=[row, row],
        out_shape=[jax.ShapeDtypeStruct((t, d), F32), jax.ShapeDtypeStruct((t, d), BF16)],
        compiler_params=_cparams(("parallel",)),
        name="out_proj",
    )(o_diff, o_gla, o_moba, w, xf, g, b)


def _ple_kernel(xb_ref, wg_ref, pb_ref, we_ref, xf_ref, g_ref, b_ref, of_ref, ob_ref):
    e = _dot(pb_ref[0], we_ref[0]) * jax.nn.sigmoid(_dot(xb_ref[...], wg_ref[0]))
    _emit_stream(of_ref, ob_ref, DEEPNORM_ALPHA * xf_ref[...] + e, g_ref, b_ref)


def _ple(xf, xb, pb, w_pe, w_pg, layer, g, b):
    t, d = xf.shape
    tm = _row_tile(t)
    row = pl.BlockSpec((tm, d), lambda i: (i, 0))
    vec = pl.BlockSpec((1, d), lambda i: (0, 0))
    whole = lambda a: pl.BlockSpec((1,) + a.shape[1:], lambda i: (layer, 0, 0))
    return pl.pallas_call(
        _ple_kernel,
        grid=(t // tm,),
        in_specs=[row, whole(w_pg),
                  pl.BlockSpec((1, tm, pb.shape[2]), lambda i: (layer, i, 0)),
                  whole(w_pe), row, vec, vec],
        out_specs=[row, row],
        out_shape=[jax.ShapeDtypeStruct((t, d), F32), jax.ShapeDtypeStruct((t, d), BF16)],
        compiler_params=_cparams(("parallel",)),
        name="ple",
    )(xb, w_pg, pb, w_pe, xf, g, b)


def kernel(x, p, positions, w_in, w_out, diff_lambda, diff_norm_g, gla_gate_up, gla_gate_b, gla_norm_g,
           ffn1_gate, ffn1_up, ffn1_down, ffn2_gate, ffn2_up, ffn2_down, w_pe, w_pg, ln_g, ln_b):
    batch, seq, d = x.shape
    t = batch * seq
    assert d == (DIFF_HEADS + GLA_HEADS + MOBA_HEADS) * HEAD_DIM and seq % MOBA_BLOCK == 0
    xf = x.reshape(t, d)
    xb = _cast_bf16(xf[None])[0]
    tables = _rope_tables(positions.reshape(t, 1))
    vec = lambda v: v.reshape(1, -1)
    f1g, f1u, f1d, f2g, f2u, f2d = (_cast_bf16(w) for w in (ffn1_gate, ffn1_up, ffn1_down, ffn2_gate, ffn2_up, ffn2_down))
    w_in_b, w_out_b, w_pe_b, w_pg_b = _reorder_w_in(w_in), _cast_bf16(w_out), _cast_bf16(w_pe), _cast_bf16(w_pg)
    p_b = _cast_bf16(p.reshape(DEPTH, t, -1))
    for i in range(DEPTH):
        xf, xb = _ffn(xf, xb, f1g, f1u, f1d, i, vec(ln_g[i, 0]), vec(ln_b[i, 0]))
        pa, pb, pc, gg = _proj(xb, w_in_b, i, tables)
        o_diff = _diff_attention(pa, pc, diff_lambda[i], vec(diff_norm_g[i]), batch, seq, i)
        o_gla = _gla(pc, gg, gla_gate_up[i], gla_gate_b[i], vec(gla_norm_g[i]), batch, seq)
        o_moba = _moba_attention(pb, pc, batch, seq)
        xf, xb = _out_proj(o_diff, o_gla, o_moba, w_out_b, i, xf, vec(ln_g[i, 1]), vec(ln_b[i, 1]))
        xf, xb = _ffn(xf, xb, f2g, f2u, f2d, i, vec(ln_g[i, 2]), vec(ln_b[i, 2]))
        xf, xb = _ple(xf, xb, p_b, w_pe_b, w_pg_b, i, vec(ln_g[i, 3]), vec(ln_b[i, 3]))
    return xf.reshape(batch, seq, d)
```

```python
import functools
import math

import numpy as np
import jax
import jax.numpy as jnp
from jax import lax
from jax.experimental import pallas as pl
from jax.experimental.pallas import tpu as pltpu

F32 = jnp.float32
BF16 = jnp.bfloat16

DEPTH = 2
HEAD_DIM = 128
DIFF_HEADS = 6
GLA_HEADS = 4
MOBA_HEADS = 6
DIFF_QK_DIM = 64
GLA_DK = 64
GLA_DV = 128
GLA_GATE_RANK = 16
GLA_TAU = 16.0
GLA_CHUNK = 64
GLA_UNROLL = 2
MOBA_BLOCK = 256
MOBA_TOPK = 3
ROPE_THETA = 10000.0
LN_EPS = 1e-5
DEEPNORM_ALPHA = (2 * DEPTH) ** 0.25

LANES = 128
SUBLANES = 8
VMEM_LIMIT_BYTES = 56 * 1024 * 1024

PROJ_TN = 768
_CB = LANES
COL_DQ = 0
COL_DK = COL_DQ + DIFF_HEADS
COL_MQ = COL_DK + DIFF_HEADS
COL_MK = COL_MQ + MOBA_HEADS
COL_DV = COL_MK + MOBA_HEADS
COL_MV = COL_DV + DIFF_HEADS
COL_GQ = COL_MV + MOBA_HEADS
COL_GK = COL_GQ + GLA_HEADS
COL_GV = COL_GK + GLA_HEADS
COL_GR = COL_GV + GLA_HEADS
COL_GG = COL_GR + GLA_HEADS
GG_WIDTH = 2 * _CB
N_COL_BLOCKS = COL_GG + 2
PROJ_WIDTH = N_COL_BLOCKS * _CB
assert PROJ_WIDTH % PROJ_TN == 0

NEG_INF = float("-inf")
LOG2_E = math.log2(math.e)
V_ROWS = HEAD_DIM + 16


def _cparams(sem):
    return pltpu.CompilerParams(dimension_semantics=sem, vmem_limit_bytes=VMEM_LIMIT_BYTES)


def _layer_norm(y, g, b):
    mu = jnp.mean(y, axis=-1, keepdims=True)
    d = y - mu
    var = jnp.mean(d * d, axis=-1, keepdims=True)
    return d * lax.rsqrt(var + LN_EPS) * g + b


def _dot_nt(a, b):
    return lax.dot_general(a, b, (((1,), (1,)), ((), ())), preferred_element_type=F32)


def _dot(a, b):
    return jnp.dot(a, b, preferred_element_type=F32)


def _head_cols(g):
    return slice(g * LANES, (g + 1) * LANES)


def _rope_table_kernel(pos_ref, inv_a_ref, inv_b_ref, sgn_a_ref, sgn_b_ref,
                       cos_a_ref, sin_a_ref, cos_b_ref, sin_b_ref):
    pos = pos_ref[...].astype(F32)
    ang_a = pos * inv_a_ref[...]
    ang_b = pos * inv_b_ref[...]
    cos_a_ref[...] = jnp.cos(ang_a)
    sin_a_ref[...] = jnp.sin(ang_a) * sgn_a_ref[...]
    cos_b_ref[...] = jnp.cos(ang_b)
    sin_b_ref[...] = jnp.sin(ang_b) * sgn_b_ref[...]


def _rope_tables(pos):
    t = pos.shape[0]
    tm = next(c for c in (1024, 512, MOBA_BLOCK) if t % c == 0)

    def pattern(d):
        inv = ROPE_THETA ** (-jnp.arange(0, d, 2, dtype=F32) / d)
        inv = jnp.tile(inv, 2 * LANES // d)[None, :]
        sgn = jnp.tile(jnp.concatenate([-jnp.ones(d // 2, F32), jnp.ones(d // 2, F32)]), LANES // d)[None, :]
        return inv, sgn

    inv_a, sgn_a = pattern(DIFF_QK_DIM)
    inv_b, sgn_b = pattern(HEAD_DIM)
    row = pl.BlockSpec((1, LANES), lambda i: (0, 0))
    out = pl.BlockSpec((tm, LANES), lambda i: (i, 0))
    return pl.pallas_call(
        _rope_table_kernel,
        grid=(t // tm,),
        in_specs=[pl.BlockSpec((tm, 1), lambda i: (i, 0)), row, row, row, row],
        out_specs=[out, out, out, out],
        out_shape=[jax.ShapeDtypeStruct((t, LANES), F32)] * 4,
        compiler_params=_cparams(("parallel",)),
        name="rope_tables",
    )(pos, inv_a, inv_b, sgn_a, sgn_b)


def _ffn_kernel(xf_ref, xb_ref, wg_ref, wu_ref, wd_ref, g_ref, b_ref, of_ref, ob_ref, acc_ref, *, nf):
    j = pl.program_id(1)

    def partial():
        x = xb_ref[...]
        gt = _dot(x, wg_ref[0])
        ut = _dot(x, wu_ref[0])
        h = (gt * jax.nn.sigmoid(gt) * ut).astype(BF16)
        return _dot(h, wd_ref[0])

    def finish(total):
        _emit_stream(of_ref, ob_ref, DEEPNORM_ALPHA * xf_ref[...] + 0.5 * total, g_ref, b_ref)

    if nf == 1:
        finish(partial())
        return

    @pl.when(j == 0)
    def _():
        acc_ref[...] = partial()

    @pl.when((j > 0) & (j < nf - 1))
    def _():
        acc_ref[...] += partial()

    @pl.when(j == nf - 1)
    def _():
        finish(acc_ref[...] + partial())


def _ffn_tiles(t, f):
    tm = 512 if t % 512 == 0 else t
    tf = 512 if f % 512 == 0 else f
    return tm, tf


def _ffn(xf, xb, wg, wu, wd, layer, g, b):
    t, d = xf.shape
    f = wg.shape[2]
    tm, tf = _ffn_tiles(t, f)
    nf = f // tf
    row = pl.BlockSpec((tm, d), lambda i, j: (i, 0))
    vec = pl.BlockSpec((1, d), lambda i, j: (0, 0))
    return pl.pallas_call(
        functools.partial(_ffn_kernel, nf=nf),
        grid=(t // tm, nf),
        in_specs=[row, row,
                  pl.BlockSpec((1, d, tf), lambda i, j: (layer, 0, j)),
                  pl.BlockSpec((1, d, tf), lambda i, j: (layer, 0, j)),
                  pl.BlockSpec((1, tf, d), lambda i, j: (layer, j, 0)),
                  vec, vec],
        out_specs=[row, row],
        out_shape=[jax.ShapeDtypeStruct((t, d), F32), jax.ShapeDtypeStruct((t, d), BF16)],
        scratch_shapes=[pltpu.VMEM((tm, d), F32)],
        compiler_params=_cparams(("parallel", "arbitrary")),
        name="ffn",
    )(xf, xb, wg, wu, wd, g, b)


def _proj_rope_kernel(xb_ref, w_ref, cos_ref, sin_ref, p_ref, *, rot_dim, q_scale, q_tiles):
    j = pl.program_id(1)
    acc = _dot_nt(xb_ref[...], w_ref[0])
    scale = jnp.where(j < q_tiles, q_scale, 1.0).astype(F32)
    half = rot_dim // 2
    for hh in range(PROJ_TN // LANES):
        t = acc[:, _head_cols(hh)]
        if rot_dim == LANES:
            rot = pltpu.roll(t, half, 1)
        else:
            lane = lax.broadcasted_iota(jnp.int32, t.shape, 1)
            rot = jnp.where((lane // half) % 2 == 0, pltpu.roll(t, LANES - half, 1), pltpu.roll(t, half, 1))
        p_ref[:, _head_cols(hh)] = ((t * cos_ref[...] + rot * sin_ref[...]) * scale).astype(BF16)


def _proj_plain_kernel(xb_ref, w_ref, p_ref, gg_ref, *, nj, gq_cols):
    j = pl.program_id(1)
    acc = _dot_nt(xb_ref[...], w_ref[0])
    col = j * PROJ_TN + lax.broadcasted_iota(jnp.int32, (1, PROJ_TN), 1)
    scale = jnp.where((col >= gq_cols[0]) & (col < gq_cols[1]), GLA_DK ** -0.5, 1.0).astype(F32)
    p_ref[...] = (acc * scale).astype(BF16)

    @pl.when(j == nj - 1)
    def _():
        gg_ref[...] = acc[:, PROJ_TN - GG_WIDTH:]


def _proj(xb, w, layer, tables):
    t, d = xb.shape
    tm = 2048 if t % 2048 == 0 else t

    def tiles(c):
        assert (c * LANES) % PROJ_TN == 0
        return c * LANES // PROJ_TN

    cos_a, sin_a, cos_b, sin_b = tables
    x_spec = pl.BlockSpec((tm, d), lambda i, j: (i, 0))
    tab = pl.BlockSpec((tm, LANES), lambda i, j: (i, 0))
    out = pl.BlockSpec((tm, PROJ_TN), lambda i, j: (i, j))
    w_spec = lambda first: pl.BlockSpec((1, PROJ_TN, d), lambda i, j: (layer, first + j, 0))

    def rope_group(first_col, q_heads, n_heads, rot_dim, q_scale, cos, sin, name):
        nj = tiles(n_heads)
        return pl.pallas_call(
            functools.partial(_proj_rope_kernel, rot_dim=rot_dim, q_scale=q_scale, q_tiles=tiles(q_heads)),
            grid=(t // tm, nj),
            in_specs=[x_spec, w_spec(tiles(first_col)), tab, tab],
            out_specs=out,
            out_shape=jax.ShapeDtypeStruct((t, nj * PROJ_TN), BF16),
            compiler_params=_cparams(("parallel", "arbitrary")),
            name=name,
        )(xb, w, cos, sin)

    pa = rope_group(COL_DQ, DIFF_HEADS, 2 * DIFF_HEADS, DIFF_QK_DIM, DIFF_QK_DIM ** -0.5 * LOG2_E, cos_a, sin_a,
                    "in_proj_diff")
    pb = rope_group(COL_MQ, MOBA_HEADS, 2 * MOBA_HEADS, HEAD_DIM, HEAD_DIM ** -0.5 * LOG2_E, cos_b, sin_b,
                    "in_proj_moba")
    nj = tiles(N_COL_BLOCKS - COL_DV)
    pc, gg = pl.pallas_call(
        functools.partial(_proj_plain_kernel, nj=nj, gq_cols=((COL_GQ - COL_DV) * LANES, (COL_GK - COL_DV) * LANES)),
        grid=(t // tm, nj),
        in_specs=[x_spec, w_spec(tiles(COL_DV))],
        out_specs=[out, pl.BlockSpec((tm, GG_WIDTH), lambda i, j: (i, 0))],
        out_shape=[jax.ShapeDtypeStruct((t, nj * PROJ_TN), BF16), jax.ShapeDtypeStruct((t, GG_WIDTH), F32)],
        compiler_params=_cparams(("parallel", "arbitrary")),
        name="in_proj_plain",
    )(xb, w)
    return pa, pb, pc, gg


_IN_WIDTHS = (DIFF_HEADS * 128, DIFF_HEADS * 128, DIFF_HEADS * 128, GLA_HEADS * GLA_DK, GLA_HEADS * GLA_DK,
              GLA_HEADS * GLA_DV, GLA_HEADS * GLA_DV, GLA_GATE_RANK, MOBA_HEADS * 128, MOBA_HEADS * 128,
              MOBA_HEADS * 128)
_IN_OFFS = tuple(int(v) for v in np.concatenate([[0], np.cumsum(_IN_WIDTHS)]))
D_IN = _IN_OFFS[-1]


def _reorder_w_in_kernel(w_ref, o_ref):
    src = dict(zip(("dq", "dk", "dv", "gq", "gk", "gv", "gr", "gg", "mq", "mk", "mv"), _IN_OFFS[:-1]))
    o_ref[...] = jnp.zeros_like(o_ref)

    def put(dst_block, name, width, src_off=0):
        s0 = src[name] + src_off
        o_ref[0, dst_block * LANES:dst_block * LANES + width, :] = w_ref[0, s0:s0 + width, :].astype(BF16)

    for name, col, heads in (("dq", COL_DQ, DIFF_HEADS), ("dk", COL_DK, DIFF_HEADS), ("mq", COL_MQ, MOBA_HEADS),
                             ("mk", COL_MK, MOBA_HEADS), ("dv", COL_DV, DIFF_HEADS), ("mv", COL_MV, MOBA_HEADS),
                             ("gv", COL_GV, GLA_HEADS), ("gr", COL_GR, GLA_HEADS)):
        put(col, name, heads * LANES)
    for h in range(GLA_HEADS):
        put(COL_GQ + h, "gq", GLA_DK, h * GLA_DK)
        put(COL_GK + h, "gk", GLA_DK, h * GLA_DK)
    put(COL_GG, "gg", GLA_GATE_RANK)


def _reorder_w_in(w_in):
    depth, d, d_in = w_in.shape
    assert d_in == D_IN and all(o % 16 == 0 for o in _IN_OFFS)
    wt = jnp.swapaxes(w_in, 1, 2)
    tc = 128 if d % 128 == 0 else d
    return pl.pallas_call(
        _reorder_w_in_kernel,
        grid=(depth, d // tc),
        in_specs=[pl.BlockSpec((1, d_in, tc), lambda l, c: (l, 0, c))],
        out_specs=pl.BlockSpec((1, PROJ_WIDTH, tc), lambda l, c: (l, 0, c)),
        out_shape=jax.ShapeDtypeStruct((depth, PROJ_WIDTH, d), BF16),
        compiler_params=_cparams(("parallel", "parallel")),
        name="reorder_w_in",
    )(wt)


def _cast_kernel(x_ref, o_ref):
    o_ref[...] = x_ref[...].astype(o_ref.dtype)


def _cast_bf16(x):
    l, r, c = x.shape
    tr = next((t for t in (1024, 512, 256, 128) if r % t == 0 and t * c * 4 <= (4 << 20)), r)
    spec = pl.BlockSpec((1, tr, c), lambda i, j: (i, j, 0))
    return pl.pallas_call(
        _cast_kernel,
        grid=(l, r // tr),
        in_specs=[spec],
        out_specs=spec,
        out_shape=jax.ShapeDtypeStruct(x.shape, BF16),
        compiler_params=_cparams(("parallel", "parallel")),
        name="cast_bf16",
    )(x)


def _store_v_transposed(v_ref, vt_ref, tk):
    seq = v_ref.shape[0]
    for blk in range(seq // tk):
        vt_ref[0:HEAD_DIM, blk * tk:(blk + 1) * tk] = v_ref[blk * tk:(blk + 1) * tk, :].astype(F32).T.astype(BF16)
    vt_ref[HEAD_DIM:V_ROWS, :] = jnp.ones((V_ROWS - HEAD_DIM, seq), BF16)


def _col_max(parts):
    m = jnp.max(parts[0], axis=0, keepdims=True)
    for s in parts[1:]:
        m = jnp.maximum(m, jnp.max(s, axis=0, keepdims=True))
    return m


def _diff_kernel(lam_ref, g_ref, q_ref, k_ref, v_ref, o_ref, vt_ref, *, tq, lam_init):
    seq = k_ref.shape[0]
    _store_v_transposed(v_ref, vt_ref, tq)
    lf = lam_ref[...]
    lam = (jnp.exp(jnp.sum(lf[0:1] * lf[1:2], axis=1, keepdims=True))
           - jnp.exp(jnp.sum(lf[2:3] * lf[3:4], axis=1, keepdims=True)) + lam_init)
    lane = lax.broadcasted_iota(jnp.int32, (tq, LANES), 1)
    causal = (lax.broadcasted_iota(jnp.int32, (tq, 2 * tq), 0)
              <= lax.broadcasted_iota(jnp.int32, (tq, 2 * tq), 1) % tq)
    for qi in range(seq // tq):
        lo, hi = qi * tq, (qi + 1) * tq
        q = q_ref[lo:hi, :].astype(F32)
        qq = jnp.concatenate([jnp.where(lane < DIFF_QK_DIM, q, 0.0),
                              jnp.where(lane >= DIFF_QK_DIM, q, 0.0)], axis=0).astype(BF16)
        parts = [jnp.where(causal, _dot_nt(k_ref[lo:hi, :], qq), NEG_INF)]
        if qi > 0:
            parts.append(_dot_nt(k_ref[0:lo, :], qq))
        m = _col_max(parts)
        acc = _dot(vt_ref[:, lo:hi], jnp.exp2(parts[0] - m).astype(BF16))
        if qi > 0:
            acc = acc + _dot(vt_ref[:, 0:lo], jnp.exp2(parts[1] - m).astype(BF16))
        o_t = acc[0:HEAD_DIM] / acc[HEAD_DIM:HEAD_DIM + 1]
        o = (o_t[:, :tq] - lam * o_t[:, tq:]).T
        o = o * lax.rsqrt(jnp.mean(o * o, axis=-1, keepdims=True) + LN_EPS) * g_ref[...] * (1.0 - lam_init)
        o_ref[lo:hi, :] = o.astype(BF16)


def _diff_attention(pa, pc, lam, g, batch, seq, layer):
    tq = 256
    lam_init = 0.8 - 0.6 * math.exp(-0.3 * layer)
    col = lambda c0: pl.BlockSpec((seq, LANES), lambda b, h: (b, c0 + h))
    return pl.pallas_call(
        functools.partial(_diff_kernel, tq=tq, lam_init=lam_init),
        grid=(batch, DIFF_HEADS),
        in_specs=[pl.BlockSpec((4, DIFF_QK_DIM), lambda b, h: (0, 0)),
                  pl.BlockSpec((1, LANES), lambda b, h: (0, 0)),
                  col(0), col(DIFF_HEADS), col(0)],
        out_specs=pl.BlockSpec((seq, LANES), lambda b, h: (b, h)),
        out_shape=jax.ShapeDtypeStruct((batch * seq, DIFF_HEADS * LANES), BF16),
        scratch_shapes=[pltpu.VMEM((V_ROWS, seq), BF16)],
        compiler_params=_cparams(("parallel", "parallel")),
        name="diff_attn",
    )(lam, g, pa, pa, pc)


def _moba_kernel(q_ref, k_ref, v_ref, o_ref, vt_ref, *, nb):
    tq = MOBA_BLOCK
    _store_v_transposed(v_ref, vt_ref, tq)
    nbp = -(-nb // SUBLANES) * SUBLANES
    rows = [jnp.mean(k_ref[n * tq:(n + 1) * tq, :].astype(F32), axis=0, keepdims=True) for n in range(nb)]
    if nbp > nb:
        rows.append(jnp.zeros((nbp - nb, LANES), F32))
    km = jnp.concatenate(rows, axis=0)
    km_hi = km.astype(BF16)
    km_lo = (km - km_hi.astype(F32)).astype(BF16)
    causal = (lax.broadcasted_iota(jnp.int32, (tq, tq), 0) <= lax.broadcasted_iota(jnp.int32, (tq, tq), 1))
    for qi in range(nb):
        lo, hi = qi * tq, (qi + 1) * tq
        q = q_ref[lo:hi, :]
        parts = [jnp.where(causal, _dot_nt(k_ref[lo:hi, :], q), NEG_INF)]
        if qi > 0:
            s_past = _dot_nt(k_ref[0:lo, :], q)
            if qi > MOBA_TOPK:
                gate = _dot_nt(km_hi, q) + _dot_nt(km_lo, q)
                n_idx = lax.broadcasted_iota(jnp.int32, gate.shape, 0)
                rank = jnp.zeros(gate.shape, jnp.int32)
                for mb in range(qi):
                    gm = gate[mb:mb + 1, :]
                    rank = rank + jnp.where((gm > gate) | ((gm == gate) & (mb < n_idx)), 1, 0)
                bias = jnp.where(rank < MOBA_TOPK, 0.0, NEG_INF)
                s_past = jnp.concatenate([s_past[n * tq:(n + 1) * tq] + bias[n:n + 1, :] for n in range(qi)], axis=0)
            parts.append(s_past)
        m = _col_max(parts)
        acc = _dot(vt_ref[:, lo:hi], jnp.exp2(parts[0] - m).astype(BF16))
        if qi > 0:
            acc = acc + _dot(vt_ref[:, 0:lo], jnp.exp2(parts[1] - m).astype(BF16))
        o_ref[lo:hi, :] = (acc[0:HEAD_DIM] / acc[HEAD_DIM:HEAD_DIM + 1]).T.astype(BF16)


def _moba_attention(pb, pc, batch, seq):
    assert seq % MOBA_BLOCK == 0
    col = lambda c0: pl.BlockSpec((seq, LANES), lambda b, h: (b, c0 + h))
    return pl.pallas_call(
        functools.partial(_moba_kernel, nb=seq // MOBA_BLOCK),
        grid=(batch, MOBA_HEADS),
        in_specs=[col(0), col(MOBA_HEADS), col(COL_MV - COL_DV)],
        out_specs=pl.BlockSpec((seq, LANES), lambda b, h: (b, h)),
        out_shape=jax.ShapeDtypeStruct((batch * seq, MOBA_HEADS * LANES), BF16),
        scratch_shapes=[pltpu.VMEM((V_ROWS, seq), BF16)],
        compiler_params=_cparams(("parallel", "parallel")),
        name="moba_attn",
    )(pb, pb, pc)


def _gla_constants():
    c = GLA_CHUNK
    idx = np.arange(c)
    mats = [np.tril(np.ones((c, c), np.float32))]
    bm = []
    for lvl in range(6):
        m = 32 >> lvl
        parent, half = idx // (2 * m), (idx // m) % 2
        ref = parent * 2 * m + m - 1
        t = idx[None, :]
        second = (half == 1)[:, None] & (t > ref[:, None]) & (t <= idx[:, None])
        first = (half == 0)[:, None] & (t > idx[:, None]) & (t <= ref[:, None])
        mats.append((second | first).astype(np.float32))
        bm.append(((parent[:, None] == parent[None, :]) & (half == 1)[:, None] & (half == 0)[None, :]).astype(np.float32))
    bm.append(np.eye(c, dtype=np.float32))
    cmat = np.concatenate(mats, axis=0)
    return jnp.asarray(cmat, BF16), jnp.asarray(np.concatenate(bm, axis=0), F32)


def _split2(x):
    hi = x.astype(BF16)
    return hi, (x - hi.astype(F32)).astype(BF16)


def _gla_kernel(cmat_ref, bm_ref, up_ref, bias_ref, gn_ref, gg_ref, q_ref, k_ref, v_ref, r_ref, o_ref,
                la_ref, st_ref, *, seq, heads):
    c = GLA_CHUNK
    up_hi, up_lo = _split2(up_ref[...])
    rows = 512 if seq % 512 == 0 else seq
    for r0 in range(0, seq, rows):
        g_hi, g_lo = _split2(gg_ref[r0:r0 + rows, :])
        z = _dot(g_hi, up_hi) + _dot(g_hi, up_lo) + _dot(g_lo, up_hi) + bias_ref[...]
        la_ref[r0:r0 + rows, :] = (jnp.minimum(z, 0.0) - jnp.log(1.0 + jnp.exp(-jnp.abs(z)))) * (1.0 / GLA_TAU)

    st_ref[...] = jnp.zeros_like(st_ref)
    cmat = cmat_ref[...]

    def chunk_pair(cp, carry):
        for u in range(GLA_UNROLL):
            one_chunk(cp * GLA_UNROLL + u)
        return carry

    def one_chunk(ci):
        sl = pl.ds(pl.multiple_of(ci * c, c), c)
        g_hi, g_lo = _split2(la_ref[sl, :])
        e_all = _dot(cmat, g_hi) + _dot(cmat, g_lo)
        for h in range(heads):
            hc = _head_cols(h)
            e = e_all[:, hc]
            cum = e[0:c]
            q = q_ref[sl, hc].astype(F32)
            k = k_ref[sl, hc].astype(F32)
            v = v_ref[sl, hc]
            att = bm_ref[6 * c:7 * c, :] * _dot_nt(q.astype(BF16), k.astype(BF16))
            for lvl in range(6):
                w = jnp.exp(e[(1 + lvl) * c:(2 + lvl) * c])
                att = att + bm_ref[lvl * c:(lvl + 1) * c, :] * _dot_nt((q * w).astype(BF16), (k * w).astype(BF16))
            st = st_ref[h]
            qe = (q * jnp.exp(cum)).astype(BF16)
            o = _dot_nt(qe, st.astype(BF16)) + _dot(att.astype(BF16), v)
            last = cum[c - 1:c, :]
            kd = (k * jnp.exp(last - cum)).astype(BF16)
            st_ref[h] = st * jnp.exp(last) + _dot(v.astype(F32).T.astype(BF16), kd)
            o = o * lax.rsqrt(jnp.mean(o * o, axis=-1, keepdims=True) + LN_EPS) * gn_ref[...]
            r = r_ref[sl, hc].astype(F32)
            o_ref[sl, hc] = (o * (r * jax.nn.sigmoid(r))).astype(BF16)

    assert (seq // c) % GLA_UNROLL == 0
    lax.fori_loop(0, seq // c // GLA_UNROLL, chunk_pair, 0)


def _gla(p, gg, up, bias, gn, batch, seq):
    cmat, bm = _gla_constants()
    heads = GLA_HEADS
    w = heads * LANES
    up_p = jnp.pad(up.reshape(GLA_GATE_RANK, heads, GLA_DK),
                   ((0, GG_WIDTH - GLA_GATE_RANK), (0, 0), (0, LANES - GLA_DK))).reshape(GG_WIDTH, w)
    bias_p = jnp.pad(bias.reshape(heads, GLA_DK), ((0, 0), (0, LANES - GLA_DK))).reshape(1, w)
    const = lambda shape: pl.BlockSpec(shape, lambda b: (0, 0))
    col = lambda c0: pl.BlockSpec((seq, w), lambda b: (b, (c0 - COL_DV) // heads))
    return pl.pallas_call(
        functools.partial(_gla_kernel, seq=seq, heads=heads),
        grid=(batch,),
        in_specs=[const(cmat.shape), const(bm.shape), const((GG_WIDTH, w)), const((1, w)), const((1, LANES)),
                  pl.BlockSpec((seq, GG_WIDTH), lambda b: (b, 0)),
                  col(COL_GQ), col(COL_GK), col(COL_GV), col(COL_GR)],
        out_specs=pl.BlockSpec((seq, w), lambda b: (b, 0)),
        out_shape=jax.ShapeDtypeStruct((batch * seq, w), BF16),
        scratch_shapes=[pltpu.VMEM((seq, w), F32), pltpu.VMEM((heads, GLA_DV, LANES), F32)],
        compiler_params=_cparams(("parallel",)),
        name="gla",
    )(cmat, bm, up_p, bias_p, gn, gg, p, p, p, p)


def _emit_stream(of_ref, ob_ref, y, g_ref, b_ref):
    out = _layer_norm(y, g_ref[...], b_ref[...])
    of_ref[...] = out
    ob_ref[...] = out.astype(BF16)


def _out_kernel(od_ref, og_ref, om_ref, w_ref, xf_ref, g_ref, b_ref, of_ref, ob_ref):
    k0 = od_ref.shape[1]
    k1 = k0 + og_ref.shape[1]
    y = (_dot(od_ref[...], w_ref[0, 0:k0, :]) + _dot(og_ref[...], w_ref[0, k0:k1, :])
         + _dot(om_ref[...], w_ref[0, k1:, :]))
    _emit_stream(of_ref, ob_ref, DEEPNORM_ALPHA * xf_ref[...] + y, g_ref, b_ref)


def _row_tile(t):
    return 512 if t % 512 == 0 else t


def _out_proj(o_diff, o_gla, o_moba, w, layer, xf, g, b):
    t, d = xf.shape
    tm = _row_tile(t)
    row = pl.BlockSpec((tm, d), lambda i: (i, 0))
    vec = pl.BlockSpec((1, d), lambda i: (0, 0))
    part = lambda a: pl.BlockSpec((tm, a.shape[1]), lambda i: (i, 0))
    return pl.pallas_call(
        _out_kernel,
        grid=(t // tm,),
        in_specs=[part(o_diff), part(o_gla), part(o_moba),
                  pl.BlockSpec((1,) + w.shape[1:], lambda i: (layer, 0, 0)),
                  row, vec, vec],
        out_specs=[row, row],
        out_shape=[jax.ShapeDtypeStruct((t, d), F32), jax.ShapeDtypeStruct((t, d), BF16)],
        compiler_params=_cparams(("parallel",)),
        name="out_proj",
    )(o_diff, o_gla, o_moba, w, xf, g, b)


def _ple_kernel(xb_ref, wg_ref, pb_ref, we_ref, xf_ref, g_ref, b_ref, of_ref, ob_ref):
    e = _dot(pb_ref[0], we_ref[0]) * jax.nn.sigmoid(_dot(xb_ref[...], wg_ref[0]))
    _emit_stream(of_ref, ob_ref, DEEPNORM_ALPHA * xf_ref[...] + e, g_ref, b_ref)


def _ple(xf, xb, pb, w_pe, w_pg, layer, g, b):
    t, d = xf.shape
    tm = _row_tile(t)
    row = pl.BlockSpec((tm, d), lambda i: (i, 0))
    vec = pl.BlockSpec((1, d), lambda i: (0, 0))
    whole = lambda a: pl.BlockSpec((1,) + a.shape[1:], lambda i: (layer, 0, 0))
    return pl.pallas_call(
        _ple_kernel,
        grid=(t // tm,),
        in_specs=[row, whole(w_pg),
                  pl.BlockSpec((1, tm, pb.shape[2]), lambda i: (layer, i, 0)),
                  whole(w_pe), row, vec, vec],
        out_specs=[row, row],
        out_shape=[jax.ShapeDtypeStruct((t, d), F32), jax.ShapeDtypeStruct((t, d), BF16)],
        compiler_params=_cparams(("parallel",)),
        name="ple",
    )(xb, w_pg, pb, w_pe, xf, g, b)


def kernel(x, p, positions, w_in, w_out, diff_lambda, diff_norm_g, gla_gate_up, gla_gate_b, gla_norm_g,
           ffn1_gate, ffn1_up, ffn1_down, ffn2_gate, ffn2_up, ffn2_down, w_pe, w_pg, ln_g, ln_b):
    batch, seq, d = x.shape
    t = batch * seq
    assert d == (DIFF_HEADS + GLA_HEADS + MOBA_HEADS) * HEAD_DIM and seq % MOBA_BLOCK == 0
    xf = x.reshape(t, d)
    xb = _cast_bf16(xf[None])[0]
    tables = _rope_tables(positions.reshape(t, 1))
    vec = lambda v: v.reshape(1, -1)
    f1g, f1u, f1d, f2g, f2u, f2d = (_cast_bf16(w) for w in (ffn1_gate, ffn1_up, ffn1_down, ffn2_gate, ffn2_up, ffn2_down))
    w_in_b, w_out_b, w_pe_b, w_pg_b = _reorder_w_in(w_in), _cast_bf16(w_out), _cast_bf16(w_pe), _cast_bf16(w_pg)
    p_b = _cast_bf16(p.reshape(DEPTH, t, -1))
    for i in range(DEPTH):
        xf, xb = _ffn(xf, xb, f1g, f1u, f1d, i, vec(ln_g[i, 0]), vec(ln_b[i, 0]))
        pa, pb, pc, gg = _proj(xb, w_in_b, i, tables)
        o_diff = _diff_attention(pa, pc, diff_lambda[i], vec(diff_norm_g[i]), batch, seq, i)
        o_gla = _gla(pc, gg, gla_gate_up[i], gla_gate_b[i], vec(gla_norm_g[i]), batch, seq)
        o_moba = _moba_attention(pb, pc, batch, seq)
        xf, xb = _out_proj(o_diff, o_gla, o_moba, w_out_b, i, xf, vec(ln_g[i, 1]), vec(ln_b[i, 1]))
        xf, xb = _ffn(xf, xb, f2g, f2u, f2d, i, vec(ln_g[i, 2]), vec(ln_b[i, 2]))
        xf, xb = _ple(xf, xb, p_b, w_pe_b, w_pg_b, i, vec(ln_g[i, 3]), vec(ln_b[i, 3]))
    return xf.reshape(batch, seq, d)
```

```python
import functools
import math

import numpy as np
import jax
import jax.numpy as jnp
from jax import lax
from jax.experimental import pallas as pl
from jax.experimental.pallas import tpu as pltpu

F32 = jnp.float32
BF16 = jnp.bfloat16

DEPTH = 2
HEAD_DIM = 128
DIFF_HEADS = 6
GLA_HEADS = 4
MOBA_HEADS = 6
DIFF_QK_DIM = 64
GLA_DK = 64
GLA_DV = 128
GLA_GATE_RANK = 16
GLA_TAU = 16.0
GLA_CHUNK = 64
GLA_UNROLL = 4
MOBA_BLOCK = 256
MOBA_TOPK = 3
ROPE_THETA = 10000.0
LN_EPS = 1e-5
DEEPNORM_ALPHA = (2 * DEPTH) ** 0.25

LANES = 128
SUBLANES = 8
VMEM_LIMIT_BYTES = 56 * 1024 * 1024

PROJ_TN = 768
_CB = LANES
COL_DQ = 0
COL_DK = COL_DQ + DIFF_HEADS
COL_MQ = COL_DK + DIFF_HEADS
COL_MK = COL_MQ + MOBA_HEADS
COL_DV = COL_MK + MOBA_HEADS
COL_MV = COL_DV + DIFF_HEADS
COL_GQ = COL_MV + MOBA_HEADS
COL_GK = COL_GQ + GLA_HEADS
COL_GV = COL_GK + GLA_HEADS
COL_GR = COL_GV + GLA_HEADS
COL_GG = COL_GR + GLA_HEADS
GG_WIDTH = 2 * _CB
N_COL_BLOCKS = COL_GG + 2
PROJ_WIDTH = N_COL_BLOCKS * _CB
assert PROJ_WIDTH % PROJ_TN == 0

NEG_INF = float("-inf")
LOG2_E = math.log2(math.e)
ATTN_LOOKAHEAD = 2
V_ROWS = HEAD_DIM + 16


def _cparams(sem):
    return pltpu.CompilerParams(dimension_semantics=sem, vmem_limit_bytes=VMEM_LIMIT_BYTES)


def _layer_norm(y, g, b):
    mu = jnp.mean(y, axis=-1, keepdims=True)
    d = y - mu
    var = jnp.mean(d * d, axis=-1, keepdims=True)
    return d * lax.rsqrt(var + LN_EPS) * g + b


def _dot_nt(a, b):
    return lax.dot_general(a, b, (((1,), (1,)), ((), ())), preferred_element_type=F32)


def _dot(a, b):
    return jnp.dot(a, b, preferred_element_type=F32)


def _head_cols(g):
    return slice(g * LANES, (g + 1) * LANES)


def _rope_table_kernel(pos_ref, inv_a_ref, inv_b_ref, sgn_a_ref, sgn_b_ref,
                       cos_a_ref, sin_a_ref, cos_b_ref, sin_b_ref):
    pos = pos_ref[...].astype(F32)
    ang_a = pos * inv_a_ref[...]
    ang_b = pos * inv_b_ref[...]
    cos_a_ref[...] = jnp.cos(ang_a)
    sin_a_ref[...] = jnp.sin(ang_a) * sgn_a_ref[...]
    cos_b_ref[...] = jnp.cos(ang_b)
    sin_b_ref[...] = jnp.sin(ang_b) * sgn_b_ref[...]


def _rope_tables(pos):
    t = pos.shape[0]
    tm = next(c for c in (1024, 512, MOBA_BLOCK) if t % c == 0)

    def pattern(d):
        inv = ROPE_THETA ** (-jnp.arange(0, d, 2, dtype=F32) / d)
        inv = jnp.tile(inv, 2 * LANES // d)[None, :]
        sgn = jnp.tile(jnp.concatenate([-jnp.ones(d // 2, F32), jnp.ones(d // 2, F32)]), LANES // d)[None, :]
        return inv, sgn

    inv_a, sgn_a = pattern(DIFF_QK_DIM)
    inv_b, sgn_b = pattern(HEAD_DIM)
    row = pl.BlockSpec((1, LANES), lambda i: (0, 0))
    out = pl.BlockSpec((tm, LANES), lambda i: (i, 0))
    return pl.pallas_call(
        _rope_table_kernel,
        grid=(t // tm,),
        in_specs=[pl.BlockSpec((tm, 1), lambda i: (i, 0)), row, row, row, row],
        out_specs=[out, out, out, out],
        out_shape=[jax.ShapeDtypeStruct((t, LANES), F32)] * 4,
        compiler_params=_cparams(("parallel",)),
        name="rope_tables",
    )(pos, inv_a, inv_b, sgn_a, sgn_b)


def _ffn_kernel(xf_ref, xb_ref, wg_ref, wu_ref, wd_ref, g_ref, b_ref, of_ref, ob_ref, acc_ref, *, nf):
    j = pl.program_id(1)

    @pl.when(j == 0)
    def _():
        acc_ref[...] = jnp.zeros_like(acc_ref)

    x = xb_ref[...]
    gt = _dot(x, wg_ref[0])
    ut = _dot(x, wu_ref[0])
    h = (gt * jax.nn.sigmoid(gt) * ut).astype(BF16)
    acc_ref[...] += _dot(h, wd_ref[0])

    @pl.when(j == nf - 1)
    def _():
        _emit_stream(of_ref, ob_ref, DEEPNORM_ALPHA * xf_ref[...] + 0.5 * acc_ref[...], g_ref, b_ref)


def _ffn_tiles(t, f):
    tm = 512 if t % 512 == 0 else t
    tf = 512 if f % 512 == 0 else f
    return tm, tf


def _ffn(xf, xb, wg, wu, wd, layer, g, b):
    t, d = xf.shape
    f = wg.shape[2]
    tm, tf = _ffn_tiles(t, f)
    nf = f // tf
    row = pl.BlockSpec((tm, d), lambda i, j: (i, 0))
    vec = pl.BlockSpec((1, d), lambda i, j: (0, 0))
    return pl.pallas_call(
        functools.partial(_ffn_kernel, nf=nf),
        grid=(t // tm, nf),
        in_specs=[row, row,
                  pl.BlockSpec((1, d, tf), lambda i, j: (layer, 0, j)),
                  pl.BlockSpec((1, d, tf), lambda i, j: (layer, 0, j)),
                  pl.BlockSpec((1, tf, d), lambda i, j: (layer, j, 0)),
                  vec, vec],
        out_specs=[row, row],
        out_shape=[jax.ShapeDtypeStruct((t, d), F32), jax.ShapeDtypeStruct((t, d), BF16)],
        scratch_shapes=[pltpu.VMEM((tm, d), F32)],
        compiler_params=_cparams(("parallel", "arbitrary")),
        name="ffn",
    )(xf, xb, wg, wu, wd, g, b)


def _proj_rope_kernel(xb_ref, w_ref, cos_ref, sin_ref, p_ref, *, rot_dim, q_scale, q_tiles):
    j = pl.program_id(1)
    scale = jnp.where(j < q_tiles, q_scale, 1.0).astype(F32)
    half = rot_dim // 2
    acc = _dot_nt(xb_ref[...], w_ref[0])
    for hh in range(PROJ_TN // LANES):
        t = acc[:, _head_cols(hh)]
        if rot_dim == LANES:
            rot = pltpu.roll(t, half, 1)
        else:
            lane = lax.broadcasted_iota(jnp.int32, t.shape, 1)
            rot = jnp.where((lane // half) % 2 == 0, pltpu.roll(t, LANES - half, 1), pltpu.roll(t, half, 1))
        p_ref[:, _head_cols(hh)] = ((t * cos_ref[...] + rot * sin_ref[...]) * scale).astype(BF16)


def _proj_plain_kernel(xb_ref, w_ref, p_ref, gg_ref, *, nj, gq_cols):
    j = pl.program_id(1)
    acc = _dot_nt(xb_ref[...], w_ref[0])
    col = j * PROJ_TN + lax.broadcasted_iota(jnp.int32, (1, PROJ_TN), 1)
    scale = jnp.where((col >= gq_cols[0]) & (col < gq_cols[1]), GLA_DK ** -0.5, 1.0).astype(F32)
    p_ref[...] = (acc * scale).astype(BF16)

    @pl.when(j == nj - 1)
    def _():
        gg_ref[...] = acc[:, PROJ_TN - GG_WIDTH:]


def _proj(xb, w, layer, tables):
    t, d = xb.shape
    tm = 2048 if t % 2048 == 0 else t

    def tiles(c):
        assert (c * LANES) % PROJ_TN == 0
        return c * LANES // PROJ_TN

    cos_a, sin_a, cos_b, sin_b = tables
    x_spec = pl.BlockSpec((tm, d), lambda i, j: (i, 0))
    tab = pl.BlockSpec((tm, LANES), lambda i, j: (i, 0))
    out = pl.BlockSpec((tm, PROJ_TN), lambda i, j: (i, j))
    w_spec = lambda first: pl.BlockSpec((1, PROJ_TN, d), lambda i, j: (layer, first + j, 0))

    def rope_group(first_col, q_heads, n_heads, rot_dim, q_scale, cos, sin, name):
        nj = tiles(n_heads)
        return pl.pallas_call(
            functools.partial(_proj_rope_kernel, rot_dim=rot_dim, q_scale=q_scale, q_tiles=tiles(q_heads)),
            grid=(t // tm, nj),
            in_specs=[x_spec, w_spec(tiles(first_col)), tab, tab],
            out_specs=out,
            out_shape=jax.ShapeDtypeStruct((t, nj * PROJ_TN), BF16),
            compiler_params=_cparams(("parallel", "arbitrary")),
            name=name,
        )(xb, w, cos, sin)

    pa = rope_group(COL_DQ, DIFF_HEADS, 2 * DIFF_HEADS, DIFF_QK_DIM, DIFF_QK_DIM ** -0.5 * LOG2_E, cos_a, sin_a,
                    "in_proj_diff")
    pb = rope_group(COL_MQ, MOBA_HEADS, 2 * MOBA_HEADS, HEAD_DIM, HEAD_DIM ** -0.5 * LOG2_E, cos_b, sin_b,
                    "in_proj_moba")
    nj = tiles(N_COL_BLOCKS - COL_DV)
    pc, gg = pl.pallas_call(
        functools.partial(_proj_plain_kernel, nj=nj, gq_cols=((COL_GQ - COL_DV) * LANES, (COL_GK - COL_DV) * LANES)),
        grid=(t // tm, nj),
        in_specs=[x_spec, w_spec(tiles(COL_DV))],
        out_specs=[out, pl.BlockSpec((tm, GG_WIDTH), lambda i, j: (i, 0))],
        out_shape=[jax.ShapeDtypeStruct((t, nj * PROJ_TN), BF16), jax.ShapeDtypeStruct((t, GG_WIDTH), F32)],
        compiler_params=_cparams(("parallel", "arbitrary")),
        name="in_proj_plain",
    )(xb, w)
    return pa, pb, pc, gg


_IN_WIDTHS = (DIFF_HEADS * 128, DIFF_HEADS * 128, DIFF_HEADS * 128, GLA_HEADS * GLA_DK, GLA_HEADS * GLA_DK,
              GLA_HEADS * GLA_DV, GLA_HEADS * GLA_DV, GLA_GATE_RANK, MOBA_HEADS * 128, MOBA_HEADS * 128,
              MOBA_HEADS * 128)
_IN_OFFS = tuple(int(v) for v in np.concatenate([[0], np.cumsum(_IN_WIDTHS)]))
D_IN = _IN_OFFS[-1]


def _reorder_w_in_kernel(w_ref, o_ref):
    src = dict(zip(("dq", "dk", "dv", "gq", "gk", "gv", "gr", "gg", "mq", "mk", "mv"), _IN_OFFS[:-1]))
    o_ref[...] = jnp.zeros_like(o_ref)

    def put(dst_block, name, width, src_off=0):
        s0 = src[name] + src_off
        o_ref[0, dst_block * LANES:dst_block * LANES + width, :] = w_ref[0, s0:s0 + width, :].astype(BF16)

    for name, col, heads in (("dq", COL_DQ, DIFF_HEADS), ("dk", COL_DK, DIFF_HEADS), ("mq", COL_MQ, MOBA_HEADS),
                             ("mk", COL_MK, MOBA_HEADS), ("dv", COL_DV, DIFF_HEADS), ("mv", COL_MV, MOBA_HEADS),
                             ("gv", COL_GV, GLA_HEADS), ("gr", COL_GR, GLA_HEADS)):
        put(col, name, heads * LANES)
    for h in range(GLA_HEADS):
        put(COL_GQ + h, "gq", GLA_DK, h * GLA_DK)
        put(COL_GK + h, "gk", GLA_DK, h * GLA_DK)
    put(COL_GG, "gg", GLA_GATE_RANK)


def _reorder_w_in(w_in):
    depth, d, d_in = w_in.shape
    assert d_in == D_IN and all(o % 16 == 0 for o in _IN_OFFS)
    wt = jnp.swapaxes(w_in, 1, 2)
    tc = 128 if d % 128 == 0 else d
    return pl.pallas_call(
        _reorder_w_in_kernel,
        grid=(depth, d // tc),
        in_specs=[pl.BlockSpec((1, d_in, tc), lambda l, c: (l, 0, c))],
        out_specs=pl.BlockSpec((1, PROJ_WIDTH, tc), lambda l, c: (l, 0, c)),
        out_shape=jax.ShapeDtypeStruct((depth, PROJ_WIDTH, d), BF16),
        compiler_params=_cparams(("parallel", "parallel")),
        name="reorder_w_in",
    )(wt)


def _cast_kernel(x_ref, o_ref):
    o_ref[...] = x_ref[...].astype(o_ref.dtype)


def _cast_bf16(x):
    l, r, c = x.shape
    tr = next((t for t in (1024, 512, 256, 128) if r % t == 0 and t * c * 4 <= (4 << 20)), r)
    spec = pl.BlockSpec((1, tr, c), lambda i, j: (i, j, 0))
    return pl.pallas_call(
        _cast_kernel,
        grid=(l, r // tr),
        in_specs=[spec],
        out_specs=spec,
        out_shape=jax.ShapeDtypeStruct(x.shape, BF16),
        compiler_params=_cparams(("parallel", "parallel")),
        name="cast_bf16",
    )(x)


def _store_v_transposed(v_ref, vt_ref, tk):
    seq = v_ref.shape[0]
    for blk in range(seq // tk):
        vt_ref[0:HEAD_DIM, blk * tk:(blk + 1) * tk] = v_ref[blk * tk:(blk + 1) * tk, :].astype(F32).T.astype(BF16)
    vt_ref[HEAD_DIM:V_ROWS, :] = jnp.ones((V_ROWS - HEAD_DIM, seq), BF16)


def _col_max(parts):
    m = jnp.max(parts[0], axis=0, keepdims=True)
    for s in parts[1:]:
        m = jnp.maximum(m, jnp.max(s, axis=0, keepdims=True))
    return m


def _diff_kernel(lam_ref, g_ref, q_ref, k_ref, v_ref, o_ref, vt_ref, *, tq, lam_init):
    seq = k_ref.shape[0]
    _store_v_transposed(v_ref, vt_ref, tq)
    lf = lam_ref[...]
    lam = (jnp.exp(jnp.sum(lf[0:1] * lf[1:2], axis=1, keepdims=True))
           - jnp.exp(jnp.sum(lf[2:3] * lf[3:4], axis=1, keepdims=True)) + lam_init)
    lane = lax.broadcasted_iota(jnp.int32, (tq, LANES), 1)
    causal = (lax.broadcasted_iota(jnp.int32, (tq, 2 * tq), 0)
              <= lax.broadcasted_iota(jnp.int32, (tq, 2 * tq), 1) % tq)
    def scores(qi):
        lo, hi = qi * tq, (qi + 1) * tq
        q = q_ref[lo:hi, :].astype(F32)
        qq = jnp.concatenate([jnp.where(lane < DIFF_QK_DIM, q, 0.0),
                              jnp.where(lane >= DIFF_QK_DIM, q, 0.0)], axis=0).astype(BF16)
        parts = [jnp.where(causal, _dot_nt(k_ref[lo:hi, :], qq), NEG_INF)]
        if qi > 0:
            parts.append(_dot_nt(k_ref[0:lo, :], qq))
        return parts

    def finish(qi, parts):
        lo, hi = qi * tq, (qi + 1) * tq
        m = _col_max(parts)
        acc = _dot(vt_ref[:, lo:hi], jnp.exp2(parts[0] - m).astype(BF16))
        if qi > 0:
            acc = acc + _dot(vt_ref[:, 0:lo], jnp.exp2(parts[1] - m).astype(BF16))
        o_t = acc[0:HEAD_DIM] / acc[HEAD_DIM:HEAD_DIM + 1]
        o = (o_t[:, :tq] - lam * o_t[:, tq:]).T
        o = o * lax.rsqrt(jnp.mean(o * o, axis=-1, keepdims=True) + LN_EPS) * g_ref[...] * (1.0 - lam_init)
        o_ref[lo:hi, :] = o.astype(BF16)

    n_tiles = seq // tq
    ahead = ATTN_LOOKAHEAD
    pending = [scores(i) for i in range(min(ahead, n_tiles))]
    for qi in range(n_tiles):
        if qi + ahead < n_tiles:
            pending.append(scores(qi + ahead))
        finish(qi, pending.pop(0))


def _diff_attention(pa, pc, lam, g, batch, seq, layer):
    tq = 256
    lam_init = 0.8 - 0.6 * math.exp(-0.3 * layer)
    col = lambda c0: pl.BlockSpec((seq, LANES), lambda b, h: (b, c0 + h))
    return pl.pallas_call(
        functools.partial(_diff_kernel, tq=tq, lam_init=lam_init),
        grid=(batch, DIFF_HEADS),
        in_specs=[pl.BlockSpec((4, DIFF_QK_DIM), lambda b, h: (0, 0)),
                  pl.BlockSpec((1, LANES), lambda b, h: (0, 0)),
                  col(0), col(DIFF_HEADS), col(0)],
        out_specs=pl.BlockSpec((seq, LANES), lambda b, h: (b, h)),
        out_shape=jax.ShapeDtypeStruct((batch * seq, DIFF_HEADS * LANES), BF16),
        scratch_shapes=[pltpu.VMEM((V_ROWS, seq), BF16)],
        compiler_params=_cparams(("parallel", "parallel")),
        name="diff_attn",
    )(lam, g, pa, pa, pc)


def _moba_kernel(q_ref, k_ref, v_ref, o_ref, vt_ref, *, nb):
    tq = MOBA_BLOCK
    _store_v_transposed(v_ref, vt_ref, tq)
    nbp = -(-nb // SUBLANES) * SUBLANES
    rows = [jnp.mean(k_ref[n * tq:(n + 1) * tq, :].astype(F32), axis=0, keepdims=True) for n in range(nb)]
    if nbp > nb:
        rows.append(jnp.zeros((nbp - nb, LANES), F32))
    km = jnp.concatenate(rows, axis=0)
    km_hi = km.astype(BF16)
    km_lo = (km - km_hi.astype(F32)).astype(BF16)
    causal = (lax.broadcasted_iota(jnp.int32, (tq, tq), 0) <= lax.broadcasted_iota(jnp.int32, (tq, tq), 1))
    def scores(qi):
        lo, hi = qi * tq, (qi + 1) * tq
        q = q_ref[lo:hi, :]
        parts = [jnp.where(causal, _dot_nt(k_ref[lo:hi, :], q), NEG_INF)]
        if qi > 0:
            s_past = _dot_nt(k_ref[0:lo, :], q)
            if qi > MOBA_TOPK:
                gate = _dot_nt(km_hi, q) + _dot_nt(km_lo, q)
                n_idx = lax.broadcasted_iota(jnp.int32, gate.shape, 0)
                rank = jnp.zeros(gate.shape, jnp.int32)
                for mb in range(qi):
                    gm = gate[mb:mb + 1, :]
                    rank = rank + jnp.where((gm > gate) | ((gm == gate) & (mb < n_idx)), 1, 0)
                bias = jnp.where(rank < MOBA_TOPK, 0.0, NEG_INF)
                s_past = jnp.concatenate([s_past[n * tq:(n + 1) * tq] + bias[n:n + 1, :] for n in range(qi)], axis=0)
            parts.append(s_past)
        return parts

    def finish(qi, parts):
        lo, hi = qi * tq, (qi + 1) * tq
        m = _col_max(parts)
        acc = _dot(vt_ref[:, lo:hi], jnp.exp2(parts[0] - m).astype(BF16))
        if qi > 0:
            acc = acc + _dot(vt_ref[:, 0:lo], jnp.exp2(parts[1] - m).astype(BF16))
        o_ref[lo:hi, :] = (acc[0:HEAD_DIM] / acc[HEAD_DIM:HEAD_DIM + 1]).T.astype(BF16)

    pending = [scores(i) for i in range(min(ATTN_LOOKAHEAD, nb))]
    for qi in range(nb):
        if qi + ATTN_LOOKAHEAD < nb:
            pending.append(scores(qi + ATTN_LOOKAHEAD))
        finish(qi, pending.pop(0))


def _moba_attention(pb, pc, batch, seq):
    assert seq % MOBA_BLOCK == 0
    col = lambda c0: pl.BlockSpec((seq, LANES), lambda b, h: (b, c0 + h))
    return pl.pallas_call(
        functools.partial(_moba_kernel, nb=seq // MOBA_BLOCK),
        grid=(batch, MOBA_HEADS),
        in_specs=[col(0), col(MOBA_HEADS), col(COL_MV - COL_DV)],
        out_specs=pl.BlockSpec((seq, LANES), lambda b, h: (b, h)),
        out_shape=jax.ShapeDtypeStruct((batch * seq, MOBA_HEADS * LANES), BF16),
        scratch_shapes=[pltpu.VMEM((V_ROWS, seq), BF16)],
        compiler_params=_cparams(("parallel", "parallel")),
        name="moba_attn",
    )(pb, pb, pc)


def _gla_constants():
    c = GLA_CHUNK
    idx = np.arange(c)
    mats = [np.tril(np.ones((c, c), np.float32))]
    bm = []
    for lvl in range(6):
        m = 32 >> lvl
        parent, half = idx // (2 * m), (idx // m) % 2
        ref = parent * 2 * m + m - 1
        t = idx[None, :]
        second = (half == 1)[:, None] & (t > ref[:, None]) & (t <= idx[:, None])
        first = (half == 0)[:, None] & (t > idx[:, None]) & (t <= ref[:, None])
        mats.append((second | first).astype(np.float32))
        bm.append(((parent[:, None] == parent[None, :]) & (half == 1)[:, None] & (half == 0)[None, :]).astype(np.float32))
    bm.append(np.eye(c, dtype=np.float32))
    cmat = np.concatenate(mats, axis=0)
    return jnp.asarray(cmat, BF16), jnp.asarray(np.concatenate(bm, axis=0), F32)


def _split2(x):
    hi = x.astype(BF16)
    return hi, (x - hi.astype(F32)).astype(BF16)


def _gla_kernel(cmat_ref, bm_ref, up_ref, bias_ref, gn_ref, gg_ref, q_ref, k_ref, v_ref, r_ref, o_ref,
                la_ref, st_ref, *, seq, heads):
    c = GLA_CHUNK
    up_hi, up_lo = _split2(up_ref[...])
    rows = 512 if seq % 512 == 0 else seq
    for r0 in range(0, seq, rows):
        g_hi, g_lo = _split2(gg_ref[r0:r0 + rows, :])
        z = _dot(g_hi, up_hi) + _dot(g_hi, up_lo) + _dot(g_lo, up_hi) + bias_ref[...]
        la_ref[r0:r0 + rows, :] = (jnp.minimum(z, 0.0) - jnp.log(1.0 + jnp.exp(-jnp.abs(z)))) * (1.0 / GLA_TAU)

    st_ref[...] = jnp.zeros_like(st_ref)
    cmat = cmat_ref[...]

    def chunk_pair(cp, carry):
        sls = [pl.ds(pl.multiple_of((cp * GLA_UNROLL + u) * c, c), c) for u in range(GLA_UNROLL)]
        e_alls = []
        for sl in sls:
            g_hi, g_lo = _split2(la_ref[sl, :])
            e_alls.append(_dot(cmat, g_hi) + _dot(cmat, g_lo))
        ids = [(sl, h, e_all[:, _head_cols(h)]) for sl, e_all in zip(sls, e_alls) for h in range(heads)]
        qs = [q_ref[sl, _head_cols(h)].astype(F32) for sl, h, _ in ids]
        ks = [k_ref[sl, _head_cols(h)].astype(F32) for sl, h, _ in ids]
        atts = [bm_ref[6 * c:7 * c, :] * _dot_nt(q.astype(BF16), k.astype(BF16)) for q, k in zip(qs, ks)]
        for lvl in range(6):
            for i, (_, _, e) in enumerate(ids):
                w = jnp.exp(e[(1 + lvl) * c:(2 + lvl) * c])
                atts[i] = atts[i] + bm_ref[lvl * c:(lvl + 1) * c, :] * _dot_nt((qs[i] * w).astype(BF16),
                                                                              (ks[i] * w).astype(BF16))
        streams = [finish_intra(sl, h, e, q, k, att) for (sl, h, e), q, k, att in zip(ids, qs, ks, atts)]
        for args in streams:
            recur(*args)
        return carry

    def finish_intra(sl, h, e, q, k, att):
        hc = _head_cols(h)
        cum = e[0:c]
        v = v_ref[sl, hc]
        last = cum[c - 1:c, :]
        qe = (q * jnp.exp(cum)).astype(BF16)
        kd = (k * jnp.exp(last - cum)).astype(BF16)
        o_intra = _dot(att.astype(BF16), v)
        upd = _dot(v.astype(F32).T.astype(BF16), kd)
        return sl, h, qe, o_intra, upd, jnp.exp(last)

    def recur(sl, h, qe, o_intra, upd, decay):
        hc = _head_cols(h)
        st = st_ref[h]
        o = _dot_nt(qe, st.astype(BF16)) + o_intra
        st_ref[h] = st * decay + upd
        o = o * lax.rsqrt(jnp.mean(o * o, axis=-1, keepdims=True) + LN_EPS) * gn_ref[...]
        r = r_ref[sl, hc].astype(F32)
        o_ref[sl, hc] = (o * (r * jax.nn.sigmoid(r))).astype(BF16)

    assert (seq // c) % GLA_UNROLL == 0
    lax.fori_loop(0, seq // c // GLA_UNROLL, chunk_pair, 0)


def _gla(p, gg, up, bias, gn, batch, seq):
    cmat, bm = _gla_constants()
    heads = GLA_HEADS
    w = heads * LANES
    up_p = jnp.pad(up.reshape(GLA_GATE_RANK, heads, GLA_DK),
                   ((0, GG_WIDTH - GLA_GATE_RANK), (0, 0), (0, LANES - GLA_DK))).reshape(GG_WIDTH, w)
    bias_p = jnp.pad(bias.reshape(heads, GLA_DK), ((0, 0), (0, LANES - GLA_DK))).reshape(1, w)
    const = lambda shape: pl.BlockSpec(shape, lambda b: (0, 0))
    col = lambda c0: pl.BlockSpec((seq, w), lambda b: (b, (c0 - COL_DV) // heads))
    return pl.pallas_call(
        functools.partial(_gla_kernel, seq=seq, heads=heads),
        grid=(batch,),
        in_specs=[const(cmat.shape), const(bm.shape), const((GG_WIDTH, w)), const((1, w)), const((1, LANES)),
                  pl.BlockSpec((seq, GG_WIDTH), lambda b: (b, 0)),
                  col(COL_GQ), col(COL_GK), col(COL_GV), col(COL_GR)],
        out_specs=pl.BlockSpec((seq, w), lambda b: (b, 0)),
        out_shape=jax.ShapeDtypeStruct((batch * seq, w), BF16),
        scratch_shapes=[pltpu.VMEM((seq, w), F32), pltpu.VMEM((heads, GLA_DV, LANES), F32)],
        compiler_params=_cparams(("parallel",)),
        name="gla",
    )(cmat, bm, up_p, bias_p, gn, gg, p, p, p, p)


def _emit_stream(of_ref, ob_ref, y, g_ref, b_ref):
    out = _layer_norm(y, g_ref[...], b_ref[...])
    of_ref[...] = out
    ob_ref[...] = out.astype(BF16)


def _out_kernel(od_ref, og_ref, om_ref, w_ref, xf_ref, g_ref, b_ref, of_ref, ob_ref):
    k0 = od_ref.shape[1]
    k1 = k0 + og_ref.shape[1]
    y = (_dot(od_ref[...], w_ref[0, 0:k0, :]) + _dot(og_ref[...], w_ref[0, k0:k1, :])
         + _dot(om_ref[...], w_ref[0, k1:, :]))
    _emit_stream(of_ref, ob_ref, DEEPNORM_ALPHA * xf_ref[...] + y, g_ref, b_ref)


def _row_tile(t):
    return 512 if t % 512 == 0 else t


def _out_proj(o_diff, o_gla, o_moba, w, layer, xf, g, b):
    t, d = xf.shape
    tm = _row_tile(t)
    row = pl.BlockSpec((tm, d), lambda i: (i, 0))
    vec = pl.BlockSpec((1, d), lambda i: (0, 0))
    part = lambda a: pl.BlockSpec((tm, a.shape[1]), lambda i: (i, 0))
    return pl.pallas_call(
        _out_kernel,
        grid=(t // tm,),
        in_specs=[part(o_diff), part(o_gla), part(o_moba),
                  pl.BlockSpec((1,) + w.shape[1:], lambda i: (layer, 0, 0)),
                  row, vec, vec],
        out_specs=[row, row],
        out_shape=[jax.ShapeDtypeStruct((t, d), F32), jax.ShapeDtypeStruct((t, d), BF16)],
        compiler_params=_cparams(("parallel",)),
        name="out_proj",
    )(o_diff, o_gla, o_moba, w, xf, g, b)


def _ple_kernel(xb_ref, wg_ref, pb_ref, we_ref, xf_ref, g_ref, b_ref, of_ref, ob_ref):
    e = _dot(pb_ref[0], we_ref[0]) * jax.nn.sigmoid(_dot(xb_ref[...], wg_ref[0]))
    _emit_stream(of_ref, ob_ref, DEEPNORM_ALPHA * xf_ref[...] + e, g_ref, b_ref)


def _ple(xf, xb, pb, w_pe, w_pg, layer, g, b):
    t, d = xf.shape
    tm = _row_tile(t)
    row = pl.BlockSpec((tm, d), lambda i: (i, 0))
    vec = pl.BlockSpec((1, d), lambda i: (0, 0))
    whole = lambda a: pl.BlockSpec((1,) + a.shape[1:], lambda i: (layer, 0, 0))
    return pl.pallas_call(
        _ple_kernel,
        grid=(t // tm,),
        in_specs=[row, whole(w_pg),
                  pl.BlockSpec((1, tm, pb.shape[2]), lambda i: (layer, i, 0)),
                  whole(w_pe), row, vec, vec],
        out_specs=[row, row],
        out_shape=[jax.ShapeDtypeStruct((t, d), F32), jax.ShapeDtypeStruct((t, d), BF16)],
        compiler_params=_cparams(("parallel",)),
        name="ple",
    )(xb, w_pg, pb, w_pe, xf, g, b)


def kernel(x, p, positions, w_in, w_out, diff_lambda, diff_norm_g, gla_gate_up, gla_gate_b, gla_norm_g,
           ffn1_gate, ffn1_up, ffn1_down, ffn2_gate, ffn2_up, ffn2_down, w_pe, w_pg, ln_g, ln_b):
    batch, seq, d = x.shape
    t = batch * seq
    assert d == (DIFF_HEADS + GLA_HEADS + MOBA_HEADS) * HEAD_DIM and seq % MOBA_BLOCK == 0
    xf = x.reshape(t, d)
    xb = _cast_bf16(xf[None])[0]
    tables = _rope_tables(positions.reshape(t, 1))
    vec = lambda v: v.reshape(1, -1)
    f1g, f1u, f1d, f2g, f2u, f2d = (_cast_bf16(w) for w in (ffn1_gate, ffn1_up, ffn1_down, ffn2_gate, ffn2_up, ffn2_down))
    w_in_b, w_out_b, w_pe_b, w_pg_b = _reorder_w_in(w_in), _cast_bf16(w_out), _cast_bf16(w_pe), _cast_bf16(w_pg)
    p_b = _cast_bf16(p.reshape(DEPTH, t, -1))
    for i in range(DEPTH):
        xf, xb = _ffn(xf, xb, f1g, f1u, f1d, i, vec(ln_g[i, 0]), vec(ln_b[i, 0]))
        pa, pb, pc, gg = _proj(xb, w_in_b, i, tables)
        o_diff = _diff_attention(pa, pc, diff_lambda[i], vec(diff_norm_g[i]), batch, seq, i)
        o_gla = _gla(pc, gg, gla_gate_up[i], gla_gate_b[i], vec(gla_norm_g[i]), batch, seq)
        o_moba = _moba_attention(pb, pc, batch, seq)
        xf, xb = _out_proj(o_diff, o_gla, o_moba, w_out_b, i, xf, vec(ln_g[i, 1]), vec(ln_b[i, 1]))
        xf, xb = _ffn(xf, xb, f2g, f2u, f2d, i, vec(ln_g[i, 2]), vec(ln_b[i, 2]))
        xf, xb = _ple(xf, xb, p_b, w_pe_b, w_pg_b, i, vec(ln_g[i, 3]), vec(ln_b[i, 3]))
    return xf.reshape(batch, seq, d)
```

```python
import functools
import math

import numpy as np
import jax
import jax.numpy as jnp
from jax import lax
from jax.experimental import pallas as pl
from jax.experimental.pallas import tpu as pltpu

F32 = jnp.float32
BF16 = jnp.bfloat16

DEPTH = 2
HEAD_DIM = 128
DIFF_HEADS = 6
GLA_HEADS = 4
MOBA_HEADS = 6
DIFF_QK_DIM = 64
GLA_DK = 64
GLA_DV = 128
GLA_GATE_RANK = 16
GLA_TAU = 16.0
GLA_CHUNK = 64
GLA_UNROLL = 4
MOBA_BLOCK = 256
MOBA_TOPK = 3
ROPE_THETA = 10000.0
LN_EPS = 1e-5
DEEPNORM_ALPHA = (2 * DEPTH) ** 0.25

LANES = 128
SUBLANES = 8
VMEM_LIMIT_BYTES = 56 * 1024 * 1024

PROJ_TN = 768
PROJ_ROW_CHUNK = 256
STREAM_ROW_CHUNK = 128
_CB = LANES
COL_DQ = 0
COL_DK = COL_DQ + DIFF_HEADS
COL_MQ = COL_DK + DIFF_HEADS
COL_MK = COL_MQ + MOBA_HEADS
COL_DV = COL_MK + MOBA_HEADS
COL_MV = COL_DV + DIFF_HEADS
COL_GQ = COL_MV + MOBA_HEADS
COL_GK = COL_GQ + GLA_HEADS
COL_GV = COL_GK + GLA_HEADS
COL_GR = COL_GV + GLA_HEADS
COL_GG = COL_GR + GLA_HEADS
GG_WIDTH = 2 * _CB
N_COL_BLOCKS = COL_GG + 2
PROJ_WIDTH = N_COL_BLOCKS * _CB
assert PROJ_WIDTH % PROJ_TN == 0

NEG_INF = float("-inf")
LOG2_E = math.log2(math.e)
ATTN_LOOKAHEAD = 2
V_ROWS = HEAD_DIM + 16


def _cparams(sem):
    return pltpu.CompilerParams(dimension_semantics=sem, vmem_limit_bytes=VMEM_LIMIT_BYTES)


def _layer_norm(y, g, b):
    mu = jnp.mean(y, axis=-1, keepdims=True)
    d = y - mu
    var = jnp.mean(d * d, axis=-1, keepdims=True)
    return d * lax.rsqrt(var + LN_EPS) * g + b


def _dot_nt(a, b):
    return lax.dot_general(a, b, (((1,), (1,)), ((), ())), preferred_element_type=F32)


def _dot(a, b):
    return jnp.dot(a, b, preferred_element_type=F32)


def _head_cols(g):
    return slice(g * LANES, (g + 1) * LANES)


def _rope_table_kernel(pos_ref, inv_a_ref, inv_b_ref, sgn_a_ref, sgn_b_ref,
                       cos_a_ref, sin_a_ref, cos_b_ref, sin_b_ref):
    pos = pos_ref[...].astype(F32)
    ang_a = pos * inv_a_ref[...]
    ang_b = pos * inv_b_ref[...]
    cos_a_ref[...] = jnp.cos(ang_a)
    sin_a_ref[...] = jnp.sin(ang_a) * sgn_a_ref[...]
    cos_b_ref[...] = jnp.cos(ang_b)
    sin_b_ref[...] = jnp.sin(ang_b) * sgn_b_ref[...]


def _rope_tables(pos):
    t = pos.shape[0]
    tm = next(c for c in (1024, 512, MOBA_BLOCK) if t % c == 0)

    def pattern(d):
        inv = ROPE_THETA ** (-jnp.arange(0, d, 2, dtype=F32) / d)
        inv = jnp.tile(inv, 2 * LANES // d)[None, :]
        sgn = jnp.tile(jnp.concatenate([-jnp.ones(d // 2, F32), jnp.ones(d // 2, F32)]), LANES // d)[None, :]
        return inv, sgn

    inv_a, sgn_a = pattern(DIFF_QK_DIM)
    inv_b, sgn_b = pattern(HEAD_DIM)
    row = pl.BlockSpec((1, LANES), lambda i: (0, 0))
    out = pl.BlockSpec((tm, LANES), lambda i: (i, 0))
    return pl.pallas_call(
        _rope_table_kernel,
        grid=(t // tm,),
        in_specs=[pl.BlockSpec((tm, 1), lambda i: (i, 0)), row, row, row, row],
        out_specs=[out, out, out, out],
        out_shape=[jax.ShapeDtypeStruct((t, LANES), F32)] * 4,
        compiler_params=_cparams(("parallel",)),
        name="rope_tables",
    )(pos, inv_a, inv_b, sgn_a, sgn_b)


def _ffn_kernel(*refs, nf, has_xb, n_hosted):
    refs = list(refs)
    xf_ref = refs.pop(0)
    xb_ref = refs.pop(0) if has_xb else None
    wg_ref, wu_ref, wd_ref, g_ref, b_ref = refs[:5]
    src_refs = refs[5:5 + n_hosted]
    of_ref, ob_ref = refs[5 + n_hosted:7 + n_hosted]
    dst_refs = refs[7 + n_hosted:7 + 2 * n_hosted]
    acc_ref = refs[-1]
    j = pl.program_id(1)

    @pl.when(j == 0)
    def _():
        acc_ref[...] = jnp.zeros_like(acc_ref)

    x = xb_ref[...] if has_xb else xf_ref[...].astype(BF16)
    gt = _dot(x, wg_ref[0])
    ut = _dot(x, wu_ref[0])
    h = (gt * jax.nn.sigmoid(gt) * ut).astype(BF16)
    acc_ref[...] += _dot(h, wd_ref[0])
    for src, dst in zip(src_refs, dst_refs):
        dst[...] = src[...].astype(BF16)

    @pl.when(j == nf - 1)
    def _():
        _emit_stream(of_ref, ob_ref, DEEPNORM_ALPHA * xf_ref[...] + 0.5 * acc_ref[...], g_ref, b_ref)


def _ffn_tiles(t, f):
    tm = 512 if t % 512 == 0 else t
    tf = 512 if f % 512 == 0 else f
    return tm, tf


def _ffn(xf, xb, w, layer, g, b, hosted=None):
    wg, wu, wd = w
    t, d = xf.shape
    f = wg.shape[2]
    tm, tf = _ffn_tiles(t, f)
    nf, ni = f // tf, t // tm
    row = pl.BlockSpec((tm, d), lambda i, j: (i, 0))
    vec = pl.BlockSpec((1, d), lambda i, j: (0, 0))
    in_specs = [row] + ([row] if xb is not None else []) + [
        pl.BlockSpec((1, d, tf), lambda i, j: (layer, 0, j)),
        pl.BlockSpec((1, d, tf), lambda i, j: (layer, 0, j)),
        pl.BlockSpec((1, tf, d), lambda i, j: (layer, j, 0)),
        vec, vec]
    args = [xf] + ([xb] if xb is not None else []) + [wg, wu, wd, g, b]
    out_specs = [row, row]
    out_shape = [jax.ShapeDtypeStruct((t, d), F32), jax.ShapeDtypeStruct((t, d), BF16)]
    n_hosted = 0
    if hosted is not None:
        hg, hu, hd, src = hosted
        dr = d // ni
        assert d % ni == 0 and dr % LANES == 0
        in_specs += [pl.BlockSpec((1, dr, tf), lambda i, j: (src, i, j)),
                     pl.BlockSpec((1, dr, tf), lambda i, j: (src, i, j)),
                     pl.BlockSpec((1, tf, dr), lambda i, j: (src, j, i))]
        args += [hg, hu, hd]
        out_specs += [pl.BlockSpec((1, dr, tf), lambda i, j: (0, i, j)),
                      pl.BlockSpec((1, dr, tf), lambda i, j: (0, i, j)),
                      pl.BlockSpec((1, tf, dr), lambda i, j: (0, j, i))]
        out_shape += [jax.ShapeDtypeStruct((1,) + a.shape[1:], BF16) for a in (hg, hu, hd)]
        n_hosted = 3
    outs = pl.pallas_call(
        functools.partial(_ffn_kernel, nf=nf, has_xb=xb is not None, n_hosted=n_hosted),
        grid=(ni, nf),
        in_specs=in_specs,
        out_specs=out_specs,
        out_shape=out_shape,
        scratch_shapes=[pltpu.VMEM((tm, d), F32)],
        compiler_params=_cparams(("parallel", "arbitrary")),
        name="ffn",
    )(*args)
    return outs[0], outs[1], (tuple(outs[2:]) if hosted is not None else None)


def _proj_rope_kernel(xb_ref, w_ref, cos_ref, sin_ref, p_ref, *, rot_dim, q_scale, q_tiles):
    j = pl.program_id(1)
    scale = jnp.where(j < q_tiles, q_scale, 1.0).astype(F32)
    half = rot_dim // 2
    tm = xb_ref.shape[0]
    rc = PROJ_ROW_CHUNK if tm % PROJ_ROW_CHUNK == 0 else tm
    for r0 in range(0, tm, rc):
        rows = slice(r0, r0 + rc)
        acc = _dot_nt(xb_ref[rows, :], w_ref[0])
        cos, sin = cos_ref[rows, :], sin_ref[rows, :]
        for hh in range(PROJ_TN // LANES):
            t = acc[:, _head_cols(hh)]
            if rot_dim == LANES:
                rot = pltpu.roll(t, half, 1)
            else:
                lane = lax.broadcasted_iota(jnp.int32, t.shape, 1)
                rot = jnp.where((lane // half) % 2 == 0, pltpu.roll(t, LANES - half, 1), pltpu.roll(t, half, 1))
            p_ref[rows, _head_cols(hh)] = ((t * cos + rot * sin) * scale).astype(BF16)


def _proj_plain_kernel(xb_ref, w_ref, p_ref, gg_ref, *, nj, gq_cols):
    j = pl.program_id(1)
    acc = _dot_nt(xb_ref[...], w_ref[0])
    col = j * PROJ_TN + lax.broadcasted_iota(jnp.int32, (1, PROJ_TN), 1)
    scale = jnp.where((col >= gq_cols[0]) & (col < gq_cols[1]), GLA_DK ** -0.5, 1.0).astype(F32)
    p_ref[...] = (acc * scale).astype(BF16)

    @pl.when(j == nj - 1)
    def _():
        gg_ref[...] = acc[:, PROJ_TN - GG_WIDTH:]


def _proj(xb, w, layer, tables):
    t, d = xb.shape
    tm = 2048 if t % 2048 == 0 else t

    def tiles(c):
        assert (c * LANES) % PROJ_TN == 0
        return c * LANES // PROJ_TN

    cos_a, sin_a, cos_b, sin_b = tables
    x_spec = pl.BlockSpec((tm, d), lambda i, j: (i, 0))
    tab = pl.BlockSpec((tm, LANES), lambda i, j: (i, 0))
    out = pl.BlockSpec((tm, PROJ_TN), lambda i, j: (i, j))
    w_spec = lambda first: pl.BlockSpec((1, PROJ_TN, d), lambda i, j: (layer, first + j, 0))

    def rope_group(first_col, q_heads, n_heads, rot_dim, q_scale, cos, sin, name):
        nj = tiles(n_heads)
        return pl.pallas_call(
            functools.partial(_proj_rope_kernel, rot_dim=rot_dim, q_scale=q_scale, q_tiles=tiles(q_heads)),
            grid=(t // tm, nj),
            in_specs=[x_spec, w_spec(tiles(first_col)), tab, tab],
            out_specs=out,
            out_shape=jax.ShapeDtypeStruct((t, nj * PROJ_TN), BF16),
            compiler_params=_cparams(("parallel", "arbitrary")),
            name=name,
        )(xb, w, cos, sin)

    pa = rope_group(COL_DQ, DIFF_HEADS, 2 * DIFF_HEADS, DIFF_QK_DIM, DIFF_QK_DIM ** -0.5 * LOG2_E, cos_a, sin_a,
                    "in_proj_diff")
    pb = rope_group(COL_MQ, MOBA_HEADS, 2 * MOBA_HEADS, HEAD_DIM, HEAD_DIM ** -0.5 * LOG2_E, cos_b, sin_b,
                    "in_proj_moba")
    nj = tiles(N_COL_BLOCKS - COL_DV)
    pc, gg = pl.pallas_call(
        functools.partial(_proj_plain_kernel, nj=nj, gq_cols=((COL_GQ - COL_DV) * LANES, (COL_GK - COL_DV) * LANES)),
        grid=(t // tm, nj),
        in_specs=[x_spec, w_spec(tiles(COL_DV))],
        out_specs=[out, pl.BlockSpec((tm, GG_WIDTH), lambda i, j: (i, 0))],
        out_shape=[jax.ShapeDtypeStruct((t, nj * PROJ_TN), BF16), jax.ShapeDtypeStruct((t, GG_WIDTH), F32)],
        compiler_params=_cparams(("parallel", "arbitrary")),
        name="in_proj_plain",
    )(xb, w)
    return pa, pb, pc, gg


_IN_WIDTHS = (DIFF_HEADS * 128, DIFF_HEADS * 128, DIFF_HEADS * 128, GLA_HEADS * GLA_DK, GLA_HEADS * GLA_DK,
              GLA_HEADS * GLA_DV, GLA_HEADS * GLA_DV, GLA_GATE_RANK, MOBA_HEADS * 128, MOBA_HEADS * 128,
              MOBA_HEADS * 128)
_IN_OFFS = tuple(int(v) for v in np.concatenate([[0], np.cumsum(_IN_WIDTHS)]))
D_IN = _IN_OFFS[-1]


def _reorder_w_in_kernel(w_ref, o_ref):
    src = dict(zip(("dq", "dk", "dv", "gq", "gk", "gv", "gr", "gg", "mq", "mk", "mv"), _IN_OFFS[:-1]))
    o_ref[...] = jnp.zeros_like(o_ref)

    def put(dst_block, name, width, src_off=0):
        s0 = src[name] + src_off
        o_ref[0, dst_block * LANES:dst_block * LANES + width, :] = w_ref[0, s0:s0 + width, :].astype(BF16)

    for name, col, heads in (("dq", COL_DQ, DIFF_HEADS), ("dk", COL_DK, DIFF_HEADS), ("mq", COL_MQ, MOBA_HEADS),
                             ("mk", COL_MK, MOBA_HEADS), ("dv", COL_DV, DIFF_HEADS), ("mv", COL_MV, MOBA_HEADS),
                             ("gv", COL_GV, GLA_HEADS), ("gr", COL_GR, GLA_HEADS)):
        put(col, name, heads * LANES)
    for h in range(GLA_HEADS):
        put(COL_GQ + h, "gq", GLA_DK, h * GLA_DK)
        put(COL_GK + h, "gk", GLA_DK, h * GLA_DK)
    put(COL_GG, "gg", GLA_GATE_RANK)


def _reorder_w_in(w_in):
    depth, d, d_in = w_in.shape
    assert d_in == D_IN and all(o % 16 == 0 for o in _IN_OFFS)
    wt = jnp.swapaxes(w_in, 1, 2)
    tc = 128 if d % 128 == 0 else d
    return pl.pallas_call(
        _reorder_w_in_kernel,
        grid=(depth, d // tc),
        in_specs=[pl.BlockSpec((1, d_in, tc), lambda l, c: (l, 0, c))],
        out_specs=pl.BlockSpec((1, PROJ_WIDTH, tc), lambda l, c: (l, 0, c)),
        out_shape=jax.ShapeDtypeStruct((depth, PROJ_WIDTH, d), BF16),
        compiler_params=_cparams(("parallel", "parallel")),
        name="reorder_w_in",
    )(wt)


def _cast_kernel(x_ref, o_ref):
    o_ref[...] = x_ref[...].astype(o_ref.dtype)


def _cast_bf16(x, layer=None):
    l, r, c = x.shape
    tr = next((t for t in (1024, 512, 256, 128) if r % t == 0 and t * c * 4 <= (4 << 20)), r)
    first, n = (0, l) if layer is None else (layer, 1)
    return pl.pallas_call(
        _cast_kernel,
        grid=(n, r // tr),
        in_specs=[pl.BlockSpec((1, tr, c), lambda i, j: (first + i, j, 0))],
        out_specs=pl.BlockSpec((1, tr, c), lambda i, j: (i, j, 0)),
        out_shape=jax.ShapeDtypeStruct((n, r, c), BF16),
        compiler_params=_cparams(("parallel", "parallel")),
        name="cast_bf16",
    )(x)


def _store_v_transposed(v_ref, vt_ref, tk):
    seq = v_ref.shape[0]
    for blk in range(seq // tk):
        vt_ref[0:HEAD_DIM, blk * tk:(blk + 1) * tk] = v_ref[blk * tk:(blk + 1) * tk, :].astype(F32).T.astype(BF16)
    vt_ref[HEAD_DIM:V_ROWS, :] = jnp.ones((V_ROWS - HEAD_DIM, seq), BF16)


def _run_tiles(n, scores, finish):
    pending = [scores(t) for t in range(min(ATTN_LOOKAHEAD, n))]
    for t in range(n):
        if t + ATTN_LOOKAHEAD < n:
            pending.append(scores(t + ATTN_LOOKAHEAD))
        finish(t, pending.pop(0))


def _col_max(parts):
    m = jnp.max(parts[0], axis=0, keepdims=True)
    for s in parts[1:]:
        m = jnp.maximum(m, jnp.max(s, axis=0, keepdims=True))
    return m


def _diff_kernel(lam_ref, g_ref, q_ref, k_ref, v_ref, o_ref, vt_ref, *, tq, lam_init):
    seq = k_ref.shape[0]
    _store_v_transposed(v_ref, vt_ref, tq)
    lf = lam_ref[...]
    lam = (jnp.exp(jnp.sum(lf[0:1] * lf[1:2], axis=1, keepdims=True))
           - jnp.exp(jnp.sum(lf[2:3] * lf[3:4], axis=1, keepdims=True)) + lam_init)
    lane = lax.broadcasted_iota(jnp.int32, (tq, LANES), 1)
    causal = (lax.broadcasted_iota(jnp.int32, (tq, 2 * tq), 0)
              <= lax.broadcasted_iota(jnp.int32, (tq, 2 * tq), 1) % tq)
    def scores(qi):
        lo, hi = qi * tq, (qi + 1) * tq
        q = q_ref[lo:hi, :].astype(F32)
        qq = jnp.concatenate([jnp.where(lane < DIFF_QK_DIM, q, 0.0),
                              jnp.where(lane >= DIFF_QK_DIM, q, 0.0)], axis=0).astype(BF16)
        parts = [jnp.where(causal, _dot_nt(k_ref[lo:hi, :], qq), NEG_INF)]
        if qi > 0:
            parts.append(_dot_nt(k_ref[0:lo, :], qq))
        return parts

    def finish(qi, parts):
        lo, hi = qi * tq, (qi + 1) * tq
        m = _col_max(parts)
        acc = _dot(vt_ref[:, lo:hi], jnp.exp2(parts[0] - m).astype(BF16))
        if qi > 0:
            acc = acc + _dot(vt_ref[:, 0:lo], jnp.exp2(parts[1] - m).astype(BF16))
        o_t = acc[0:HEAD_DIM] / acc[HEAD_DIM:HEAD_DIM + 1]
        o = (o_t[:, :tq] - lam * o_t[:, tq:]).T
        o = o * lax.rsqrt(jnp.mean(o * o, axis=-1, keepdims=True) + LN_EPS) * g_ref[...] * (1.0 - lam_init)
        o_ref[lo:hi, :] = o.astype(BF16)

    _run_tiles(seq // tq, scores, finish)


def _diff_attention(pa, pc, lam, g, batch, seq, layer):
    tq = 256
    lam_init = 0.8 - 0.6 * math.exp(-0.3 * layer)
    col = lambda c0: pl.BlockSpec((seq, LANES), lambda b, h: (b, c0 + h))
    return pl.pallas_call(
        functools.partial(_diff_kernel, tq=tq, lam_init=lam_init),
        grid=(batch, DIFF_HEADS),
        in_specs=[pl.BlockSpec((4, DIFF_QK_DIM), lambda b, h: (0, 0)),
                  pl.BlockSpec((1, LANES), lambda b, h: (0, 0)),
                  col(0), col(DIFF_HEADS), col(0)],
        out_specs=pl.BlockSpec((seq, LANES), lambda b, h: (b, h)),
        out_shape=jax.ShapeDtypeStruct((batch * seq, DIFF_HEADS * LANES), BF16),
        scratch_shapes=[pltpu.VMEM((V_ROWS, seq), BF16)],
        compiler_params=_cparams(("parallel", "parallel")),
        name="diff_attn",
    )(lam, g, pa, pa, pc)


def _moba_kernel(q_ref, k_ref, v_ref, o_ref, vt_ref, *, nb):
    tq = MOBA_BLOCK
    _store_v_transposed(v_ref, vt_ref, tq)
    nbp = -(-nb // SUBLANES) * SUBLANES
    rows = [jnp.mean(k_ref[n * tq:(n + 1) * tq, :].astype(F32), axis=0, keepdims=True) for n in range(nb)]
    if nbp > nb:
        rows.append(jnp.zeros((nbp - nb, LANES), F32))
    km = jnp.concatenate(rows, axis=0)
    km_hi = km.astype(BF16)
    km_lo = (km - km_hi.astype(F32)).astype(BF16)
    causal = (lax.broadcasted_iota(jnp.int32, (tq, tq), 0) <= lax.broadcasted_iota(jnp.int32, (tq, tq), 1))
    def scores(qi):
        lo, hi = qi * tq, (qi + 1) * tq
        q = q_ref[lo:hi, :]
        parts = [jnp.where(causal, _dot_nt(k_ref[lo:hi, :], q), NEG_INF)]
        if qi > 0:
            s_past = _dot_nt(k_ref[0:lo, :], q)
            if qi > MOBA_TOPK:
                gate = _dot_nt(km_hi, q) + _dot_nt(km_lo, q)
                n_idx = lax.broadcasted_iota(jnp.int32, gate.shape, 0)
                rank = jnp.zeros(gate.shape, jnp.int32)
                for mb in range(qi):
                    gm = gate[mb:mb + 1, :]
                    rank = rank + jnp.where((gm > gate) | ((gm == gate) & (mb < n_idx)), 1, 0)
                bias = jnp.where(rank < MOBA_TOPK, 0.0, NEG_INF)
                s_past = jnp.concatenate([s_past[n * tq:(n + 1) * tq] + bias[n:n + 1, :] for n in range(qi)], axis=0)
            parts.append(s_past)
        return parts

    def finish(qi, parts):
        lo, hi = qi * tq, (qi + 1) * tq
        m = _col_max(parts)
        acc = _dot(vt_ref[:, lo:hi], jnp.exp2(parts[0] - m).astype(BF16))
        if qi > 0:
            acc = acc + _dot(vt_ref[:, 0:lo], jnp.exp2(parts[1] - m).astype(BF16))
        o_ref[lo:hi, :] = (acc[0:HEAD_DIM] / acc[HEAD_DIM:HEAD_DIM + 1]).T.astype(BF16)

    _run_tiles(nb, scores, finish)


def _moba_attention(pb, pc, batch, seq):
    assert seq % MOBA_BLOCK == 0
    col = lambda c0: pl.BlockSpec((seq, LANES), lambda b, h: (b, c0 + h))
    return pl.pallas_call(
        functools.partial(_moba_kernel, nb=seq // MOBA_BLOCK),
        grid=(batch, MOBA_HEADS),
        in_specs=[col(0), col(MOBA_HEADS), col(COL_MV - COL_DV)],
        out_specs=pl.BlockSpec((seq, LANES), lambda b, h: (b, h)),
        out_shape=jax.ShapeDtypeStruct((batch * seq, MOBA_HEADS * LANES), BF16),
        scratch_shapes=[pltpu.VMEM((V_ROWS, seq), BF16)],
        compiler_params=_cparams(("parallel", "parallel")),
        name="moba_attn",
    )(pb, pb, pc)


def _gla_constants():
    c = GLA_CHUNK
    idx = np.arange(c)
    mats = [np.tril(np.ones((c, c), np.float32))]
    bm = []
    for lvl in range(6):
        m = 32 >> lvl
        parent, half = idx // (2 * m), (idx // m) % 2
        ref = parent * 2 * m + m - 1
        t = idx[None, :]
        second = (half == 1)[:, None] & (t > ref[:, None]) & (t <= idx[:, None])
        first = (half == 0)[:, None] & (t > idx[:, None]) & (t <= ref[:, None])
        mats.append((second | first).astype(np.float32))
        bm.append(((parent[:, None] == parent[None, :]) & (half == 1)[:, None] & (half == 0)[None, :]).astype(np.float32))
    bm.append(np.eye(c, dtype=np.float32))
    cmat = np.concatenate(mats, axis=0)
    return jnp.asarray(cmat, BF16), jnp.asarray(np.concatenate(bm, axis=0), F32)


def _split2(x):
    hi = x.astype(BF16)
    return hi, (x - hi.astype(F32)).astype(BF16)


def _gla_kernel(cmat_ref, bm_ref, up_ref, bias_ref, gn_ref, gg_ref, q_ref, k_ref, v_ref, r_ref, o_ref,
                la_ref, st_ref, *, seq, heads):
    c = GLA_CHUNK
    up_hi, up_lo = _split2(up_ref[...])
    rows = 512 if seq % 512 == 0 else seq
    for r0 in range(0, seq, rows):
        g_hi, g_lo = _split2(gg_ref[r0:r0 + rows, :])
        z = _dot(g_hi, up_hi) + _dot(g_hi, up_lo) + _dot(g_lo, up_hi) + bias_ref[...]
        la_ref[r0:r0 + rows, :] = (jnp.minimum(z, 0.0) - jnp.log(1.0 + jnp.exp(-jnp.abs(z)))) * (1.0 / GLA_TAU)

    st_ref[...] = jnp.zeros_like(st_ref)
    cmat = cmat_ref[...]

    def chunk_pair(cp, carry):
        sls = [pl.ds(pl.multiple_of((cp * GLA_UNROLL + u) * c, c), c) for u in range(GLA_UNROLL)]
        e_alls = []
        for sl in sls:
            g_hi, g_lo = _split2(la_ref[sl, :])
            e_alls.append(_dot(cmat, g_hi) + _dot(cmat, g_lo))
        ids = [(sl, h, e_all[:, _head_cols(h)]) for sl, e_all in zip(sls, e_alls) for h in range(heads)]
        qs = [q_ref[sl, _head_cols(h)].astype(F32) for sl, h, _ in ids]
        ks = [k_ref[sl, _head_cols(h)].astype(F32) for sl, h, _ in ids]
        atts = [bm_ref[6 * c:7 * c, :] * _dot_nt(q.astype(BF16), k.astype(BF16)) for q, k in zip(qs, ks)]
        for lvl in range(6):
            for i, (_, _, e) in enumerate(ids):
                w = jnp.exp(e[(1 + lvl) * c:(2 + lvl) * c])
                atts[i] = atts[i] + bm_ref[lvl * c:(lvl + 1) * c, :] * _dot_nt((qs[i] * w).astype(BF16),
                                                                              (ks[i] * w).astype(BF16))
        streams = [finish_intra(sl, h, e, q, k, att) for (sl, h, e), q, k, att in zip(ids, qs, ks, atts)]
        for args in streams:
            recur(*args)
        return carry

    def finish_intra(sl, h, e, q, k, att):
        hc = _head_cols(h)
        cum = e[0:c]
        v = v_ref[sl, hc]
        last = cum[c - 1:c, :]
        qe = (q * jnp.exp(cum)).astype(BF16)
        kd = (k * jnp.exp(last - cum)).astype(BF16)
        o_intra = _dot(att.astype(BF16), v)
        upd = _dot(v.astype(F32).T.astype(BF16), kd)
        return sl, h, qe, o_intra, upd, jnp.exp(last)

    def recur(sl, h, qe, o_intra, upd, decay):
        hc = _head_cols(h)
        st = st_ref[h]
        o = _dot_nt(qe, st.astype(BF16)) + o_intra
        st_ref[h] = st * decay + upd
        o = o * lax.rsqrt(jnp.mean(o * o, axis=-1, keepdims=True) + LN_EPS) * gn_ref[...]
        r = r_ref[sl, hc].astype(F32)
        o_ref[sl, hc] = (o * (r * jax.nn.sigmoid(r))).astype(BF16)

    assert (seq // c) % GLA_UNROLL == 0
    lax.fori_loop(0, seq // c // GLA_UNROLL, chunk_pair, 0)


def _gla(p, gg, up, bias, gn, batch, seq):
    cmat, bm = _gla_constants()
    heads = GLA_HEADS
    w = heads * LANES
    up_p = jnp.pad(up.reshape(GLA_GATE_RANK, heads, GLA_DK),
                   ((0, GG_WIDTH - GLA_GATE_RANK), (0, 0), (0, LANES - GLA_DK))).reshape(GG_WIDTH, w)
    bias_p = jnp.pad(bias.reshape(heads, GLA_DK), ((0, 0), (0, LANES - GLA_DK))).reshape(1, w)
    const = lambda shape: pl.BlockSpec(shape, lambda b: (0, 0))
    col = lambda c0: pl.BlockSpec((seq, w), lambda b: (b, (c0 - COL_DV) // heads))
    return pl.pallas_call(
        functools.partial(_gla_kernel, seq=seq, heads=heads),
        grid=(batch,),
        in_specs=[const(cmat.shape), const(bm.shape), const((GG_WIDTH, w)), const((1, w)), const((1, LANES)),
                  pl.BlockSpec((seq, GG_WIDTH), lambda b: (b, 0)),
                  col(COL_GQ), col(COL_GK), col(COL_GV), col(COL_GR)],
        out_specs=pl.BlockSpec((seq, w), lambda b: (b, 0)),
        out_shape=jax.ShapeDtypeStruct((batch * seq, w), BF16),
        scratch_shapes=[pltpu.VMEM((seq, w), F32), pltpu.VMEM((heads, GLA_DV, LANES), F32)],
        compiler_params=_cparams(("parallel",)),
        name="gla",
    )(cmat, bm, up_p, bias_p, gn, gg, p, p, p, p)


def _emit_stream(of_ref, ob_ref, y, g_ref, b_ref, rows=slice(None)):
    out = _layer_norm(y, g_ref[...], b_ref[...])
    of_ref[rows, :] = out
    ob_ref[rows, :] = out.astype(BF16)


def _row_chunks(tm):
    rc = STREAM_ROW_CHUNK if tm % STREAM_ROW_CHUNK == 0 else tm
    return [slice(r0, r0 + rc) for r0 in range(0, tm, rc)]


def _out_kernel(od_ref, og_ref, om_ref, w_ref, xf_ref, g_ref, b_ref, of_ref, ob_ref):
    k0 = od_ref.shape[1]
    k1 = k0 + og_ref.shape[1]
    for rows in _row_chunks(xf_ref.shape[0]):
        y = (_dot(od_ref[rows, :], w_ref[0, 0:k0, :]) + _dot(og_ref[rows, :], w_ref[0, k0:k1, :])
             + _dot(om_ref[rows, :], w_ref[0, k1:, :]))
        _emit_stream(of_ref, ob_ref, DEEPNORM_ALPHA * xf_ref[rows, :] + y, g_ref, b_ref, rows)


def _row_tile(t):
    return 512 if t % 512 == 0 else t


def _out_proj(o_diff, o_gla, o_moba, w, layer, xf, g, b):
    t, d = xf.shape
    tm = _row_tile(t)
    row = pl.BlockSpec((tm, d), lambda i: (i, 0))
    vec = pl.BlockSpec((1, d), lambda i: (0, 0))
    part = lambda a: pl.BlockSpec((tm, a.shape[1]), lambda i: (i, 0))
    return pl.pallas_call(
        _out_kernel,
        grid=(t // tm,),
        in_specs=[part(o_diff), part(o_gla), part(o_moba),
                  pl.BlockSpec((1,) + w.shape[1:], lambda i: (layer, 0, 0)),
                  row, vec, vec],
        out_specs=[row, row],
        out_shape=[jax.ShapeDtypeStruct((t, d), F32), jax.ShapeDtypeStruct((t, d), BF16)],
        compiler_params=_cparams(("parallel",)),
        name="out_proj",
    )(o_diff, o_gla, o_moba, w, xf, g, b)


def _ple_kernel(xb_ref, wg_ref, pb_ref, we_ref, xf_ref, g_ref, b_ref, of_ref, ob_ref):
    for rows in _row_chunks(xf_ref.shape[0]):
        e = _dot(pb_ref[0, rows, :], we_ref[0]) * jax.nn.sigmoid(_dot(xb_ref[rows, :], wg_ref[0]))
        _emit_stream(of_ref, ob_ref, DEEPNORM_ALPHA * xf_ref[rows, :] + e, g_ref, b_ref, rows)


def _ple(xf, xb, pb, w_pe, w_pg, layer, g, b):
    t, d = xf.shape
    tm = _row_tile(t)
    row = pl.BlockSpec((tm, d), lambda i: (i, 0))
    vec = pl.BlockSpec((1, d), lambda i: (0, 0))
    whole = lambda a: pl.BlockSpec((1,) + a.shape[1:], lambda i: (layer, 0, 0))
    return pl.pallas_call(
        _ple_kernel,
        grid=(t // tm,),
        in_specs=[row, whole(w_pg),
                  pl.BlockSpec((1, tm, pb.shape[2]), lambda i: (layer, i, 0)),
                  whole(w_pe), row, vec, vec],
        out_specs=[row, row],
        out_shape=[jax.ShapeDtypeStruct((t, d), F32), jax.ShapeDtypeStruct((t, d), BF16)],
        compiler_params=_cparams(("parallel",)),
        name="ple",
    )(xb, w_pg, pb, w_pe, xf, g, b)


def kernel(x, p, positions, w_in, w_out, diff_lambda, diff_norm_g, gla_gate_up, gla_gate_b, gla_norm_g,
           ffn1_gate, ffn1_up, ffn1_down, ffn2_gate, ffn2_up, ffn2_down, w_pe, w_pg, ln_g, ln_b):
    batch, seq, d = x.shape
    t = batch * seq
    assert d == (DIFF_HEADS + GLA_HEADS + MOBA_HEADS) * HEAD_DIM and seq % MOBA_BLOCK == 0
    xf, xb = x.reshape(t, d), None
    tables = _rope_tables(positions.reshape(t, 1))
    vec = lambda v: v.reshape(1, -1)
    w_in_b, w_out_b, w_pe_b, w_pg_b = _reorder_w_in(w_in), _cast_bf16(w_out), _cast_bf16(w_pe), _cast_bf16(w_pg)
    p_b = _cast_bf16(p.reshape(DEPTH, t, -1))
    ffn_params = [prm for i in range(DEPTH) for prm in ((ffn1_gate, ffn1_up, ffn1_down, i), (ffn2_gate, ffn2_up, ffn2_down, i))]
    w_next = tuple(_cast_bf16(a, layer=ffn_params[0][3]) for a in ffn_params[0][:3])

    def ffn(k, xf, xb, g, b):
        hosted = ffn_params[k + 1] if k + 1 < len(ffn_params) else None
        return _ffn(xf, xb, w_next, 0, g, b, hosted)

    for i in range(DEPTH):
        xf, xb, w_next = ffn(2 * i, xf, xb, vec(ln_g[i, 0]), vec(ln_b[i, 0]))
        pa, pb, pc, gg = _proj(xb, w_in_b, i, tables)
        o_diff = _diff_attention(pa, pc, diff_lambda[i], vec(diff_norm_g[i]), batch, seq, i)
        o_gla = _gla(pc, gg, gla_gate_up[i], gla_gate_b[i], vec(gla_norm_g[i]), batch, seq)
        o_moba = _moba_attention(pb, pc, batch, seq)
        xf, xb = _out_proj(o_diff, o_gla, o_moba, w_out_b, i, xf, vec(ln_g[i, 1]), vec(ln_b[i, 1]))
        xf, xb, w_next = ffn(2 * i + 1, xf, xb, vec(ln_g[i, 2]), vec(ln_b[i, 2]))
        xf, xb = _ple(xf, xb, p_b, w_pe_b, w_pg_b, i, vec(ln_g[i, 3]), vec(ln_b[i, 3]))
    return xf.reshape(batch, seq, d)
```

```python
import functools
import math

import numpy as np
import jax
import jax.numpy as jnp
from jax import lax
from jax.experimental import pallas as pl
from jax.experimental.pallas import tpu as pltpu

F32 = jnp.float32
BF16 = jnp.bfloat16

DEPTH = 2
HEAD_DIM = 128
DIFF_HEADS = 6
GLA_HEADS = 4
MOBA_HEADS = 6
DIFF_QK_DIM = 64
GLA_DK = 64
GLA_DV = 128
GLA_GATE_RANK = 16
GLA_TAU = 16.0
GLA_CHUNK = 64
GLA_UNROLL = 4
MOBA_BLOCK = 256
MOBA_TOPK = 3
ROPE_THETA = 10000.0
LN_EPS = 1e-5
DEEPNORM_ALPHA = (2 * DEPTH) ** 0.25

LANES = 128
SUBLANES = 8
VMEM_LIMIT_BYTES = 56 * 1024 * 1024

PROJ_TN = 768
PROJ_ROW_CHUNK = 256
STREAM_ROW_CHUNK = 128
_CB = LANES
COL_DQ = 0
COL_DK = COL_DQ + DIFF_HEADS
COL_MQ = COL_DK + DIFF_HEADS
COL_MK = COL_MQ + MOBA_HEADS
COL_DV = COL_MK + MOBA_HEADS
COL_MV = COL_DV + DIFF_HEADS
COL_GQ = COL_MV + MOBA_HEADS
COL_GK = COL_GQ + GLA_HEADS * GLA_DK // LANES
COL_GV = COL_GK + GLA_HEADS * GLA_DK // LANES
COL_GR = COL_GV + GLA_HEADS
N_COL_BLOCKS = COL_GR + GLA_HEADS
PROJ_WIDTH = N_COL_BLOCKS * _CB
assert PROJ_WIDTH % PROJ_TN == 0
GG_PAD = LANES

NEG_INF = float("-inf")
LOG2_E = math.log2(math.e)
ATTN_LOOKAHEAD = 2
V_ROWS = HEAD_DIM + 16


def _cparams(sem):
    return pltpu.CompilerParams(dimension_semantics=sem, vmem_limit_bytes=VMEM_LIMIT_BYTES)


def _layer_norm(y, g, b):
    mu = jnp.mean(y, axis=-1, keepdims=True)
    d = y - mu
    var = jnp.mean(d * d, axis=-1, keepdims=True)
    return d * lax.rsqrt(var + LN_EPS) * g + b


def _dot_nt(a, b):
    return lax.dot_general(a, b, (((1,), (1,)), ((), ())), preferred_element_type=F32)


def _dot(a, b):
    return jnp.dot(a, b, preferred_element_type=F32)


def _head_cols(g):
    return slice(g * LANES, (g + 1) * LANES)


def _rope_table_kernel(pos_ref, inv_a_ref, inv_b_ref, sgn_a_ref, sgn_b_ref,
                       cos_a_ref, sin_a_ref, cos_b_ref, sin_b_ref):
    pos = pos_ref[...].astype(F32)
    ang_a = pos * inv_a_ref[...]
    ang_b = pos * inv_b_ref[...]
    cos_a_ref[...] = jnp.cos(ang_a)
    sin_a_ref[...] = jnp.sin(ang_a) * sgn_a_ref[...]
    cos_b_ref[...] = jnp.cos(ang_b)
    sin_b_ref[...] = jnp.sin(ang_b) * sgn_b_ref[...]


def _rope_tables(pos):
    t = pos.shape[0]
    tm = next(c for c in (1024, 512, MOBA_BLOCK) if t % c == 0)

    def pattern(d):
        inv = ROPE_THETA ** (-jnp.arange(0, d, 2, dtype=F32) / d)
        inv = jnp.tile(inv, 2 * LANES // d)[None, :]
        sgn = jnp.tile(jnp.concatenate([-jnp.ones(d // 2, F32), jnp.ones(d // 2, F32)]), LANES // d)[None, :]
        return inv, sgn

    inv_a, sgn_a = pattern(DIFF_QK_DIM)
    inv_b, sgn_b = pattern(HEAD_DIM)
    row = pl.BlockSpec((1, LANES), lambda i: (0, 0))
    out = pl.BlockSpec((tm, LANES), lambda i: (i, 0))
    return pl.pallas_call(
        _rope_table_kernel,
        grid=(t // tm,),
        in_specs=[pl.BlockSpec((tm, 1), lambda i: (i, 0)), row, row, row, row],
        out_specs=[out, out, out, out],
        out_shape=[jax.ShapeDtypeStruct((t, LANES), F32)] * 4,
        compiler_params=_cparams(("parallel",)),
        name="rope_tables",
    )(pos, inv_a, inv_b, sgn_a, sgn_b)


def _ffn_kernel(*refs, nf, has_xb, n_hosted):
    refs = list(refs)
    xf_ref = refs.pop(0)
    xb_ref = refs.pop(0) if has_xb else None
    wg_ref, wu_ref, wd_ref, g_ref, b_ref = refs[:5]
    src_refs = refs[5:5 + n_hosted]
    of_ref, ob_ref = refs[5 + n_hosted:7 + n_hosted]
    dst_refs = refs[7 + n_hosted:7 + 2 * n_hosted]
    acc_ref = refs[-1]
    j = pl.program_id(1)

    @pl.when(j == 0)
    def _():
        acc_ref[...] = jnp.zeros_like(acc_ref)

    x = xb_ref[...] if has_xb else xf_ref[...].astype(BF16)
    gt = _dot(x, wg_ref[0])
    ut = _dot(x, wu_ref[0])
    h = (gt * jax.nn.sigmoid(gt) * ut).astype(BF16)
    acc_ref[...] += _dot(h, wd_ref[0])
    for src, dst in zip(src_refs, dst_refs):
        dst[...] = src[...].astype(BF16)

    @pl.when(j == nf - 1)
    def _():
        _emit_stream(of_ref, ob_ref, DEEPNORM_ALPHA * xf_ref[...] + 0.5 * acc_ref[...], g_ref, b_ref)


def _ffn_tiles(t, f):
    tm = 512 if t % 512 == 0 else t
    tf = 512 if f % 512 == 0 else f
    return tm, tf


def _ffn(xf, xb, w, layer, g, b, hosted=None):
    wg, wu, wd = w
    t, d = xf.shape
    f = wg.shape[2]
    tm, tf = _ffn_tiles(t, f)
    nf, ni = f // tf, t // tm
    row = pl.BlockSpec((tm, d), lambda i, j: (i, 0))
    vec = pl.BlockSpec((1, d), lambda i, j: (0, 0))
    in_specs = [row] + ([row] if xb is not None else []) + [
        pl.BlockSpec((1, d, tf), lambda i, j: (layer, 0, j)),
        pl.BlockSpec((1, d, tf), lambda i, j: (layer, 0, j)),
        pl.BlockSpec((1, tf, d), lambda i, j: (layer, j, 0)),
        vec, vec]
    args = [xf] + ([xb] if xb is not None else []) + [wg, wu, wd, g, b]
    out_specs = [row, row]
    out_shape = [jax.ShapeDtypeStruct((t, d), F32), jax.ShapeDtypeStruct((t, d), BF16)]
    n_hosted = 0
    if hosted is not None:
        hg, hu, hd, src = hosted
        dr = d // ni
        assert d % ni == 0 and dr % LANES == 0
        in_specs += [pl.BlockSpec((1, dr, tf), lambda i, j: (src, i, j)),
                     pl.BlockSpec((1, dr, tf), lambda i, j: (src, i, j)),
                     pl.BlockSpec((1, tf, dr), lambda i, j: (src, j, i))]
        args += [hg, hu, hd]
        out_specs += [pl.BlockSpec((1, dr, tf), lambda i, j: (0, i, j)),
                      pl.BlockSpec((1, dr, tf), lambda i, j: (0, i, j)),
                      pl.BlockSpec((1, tf, dr), lambda i, j: (0, j, i))]
        out_shape += [jax.ShapeDtypeStruct((1,) + a.shape[1:], BF16) for a in (hg, hu, hd)]
        n_hosted = 3
    outs = pl.pallas_call(
        functools.partial(_ffn_kernel, nf=nf, has_xb=xb is not None, n_hosted=n_hosted),
        grid=(ni, nf),
        in_specs=in_specs,
        out_specs=out_specs,
        out_shape=out_shape,
        scratch_shapes=[pltpu.VMEM((tm, d), F32)],
        compiler_params=_cparams(("parallel", "arbitrary")),
        name="ffn",
    )(*args)
    return outs[0], outs[1], (tuple(outs[2:]) if hosted is not None else None)


def _proj_rope_kernel(xb_ref, w_ref, cos_ref, sin_ref, p_ref, *, rot_dim, q_scale, q_tiles):
    j = pl.program_id(1)
    scale = jnp.where(j < q_tiles, q_scale, 1.0).astype(F32)
    half = rot_dim // 2
    tm = xb_ref.shape[0]
    rc = PROJ_ROW_CHUNK if tm % PROJ_ROW_CHUNK == 0 else tm
    for r0 in range(0, tm, rc):
        rows = slice(r0, r0 + rc)
        acc = _dot_nt(xb_ref[rows, :], w_ref[0])
        cos, sin = cos_ref[rows, :], sin_ref[rows, :]
        for hh in range(PROJ_TN // LANES):
            t = acc[:, _head_cols(hh)]
            if rot_dim == LANES:
                rot = pltpu.roll(t, half, 1)
            else:
                lane = lax.broadcasted_iota(jnp.int32, t.shape, 1)
                rot = jnp.where((lane // half) % 2 == 0, pltpu.roll(t, LANES - half, 1), pltpu.roll(t, half, 1))
            p_ref[rows, _head_cols(hh)] = ((t * cos + rot * sin) * scale).astype(BF16)


def _proj_plain_kernel(xb_ref, w_ref, p_ref, *, gq_cols):
    j = pl.program_id(1)
    acc = _dot_nt(xb_ref[...], w_ref[0])
    col = j * PROJ_TN + lax.broadcasted_iota(jnp.int32, (1, PROJ_TN), 1)
    scale = jnp.where((col >= gq_cols[0]) & (col < gq_cols[1]), GLA_DK ** -0.5, 1.0).astype(F32)
    p_ref[...] = (acc * scale).astype(BF16)


def _proj(xb, w, layer, tables):
    t, d = xb.shape
    tm = 2048 if t % 2048 == 0 else t

    def tiles(c):
        assert (c * LANES) % PROJ_TN == 0
        return c * LANES // PROJ_TN

    cos_a, sin_a, cos_b, sin_b = tables
    x_spec = pl.BlockSpec((tm, d), lambda i, j: (i, 0))
    tab = pl.BlockSpec((tm, LANES), lambda i, j: (i, 0))
    out = pl.BlockSpec((tm, PROJ_TN), lambda i, j: (i, j))
    w_spec = lambda first: pl.BlockSpec((1, PROJ_TN, d), lambda i, j: (layer, first + j, 0))

    def rope_group(first_col, q_heads, n_heads, rot_dim, q_scale, cos, sin, name):
        nj = tiles(n_heads)
        return pl.pallas_call(
            functools.partial(_proj_rope_kernel, rot_dim=rot_dim, q_scale=q_scale, q_tiles=tiles(q_heads)),
            grid=(t // tm, nj),
            in_specs=[x_spec, w_spec(tiles(first_col)), tab, tab],
            out_specs=out,
            out_shape=jax.ShapeDtypeStruct((t, nj * PROJ_TN), BF16),
            compiler_params=_cparams(("parallel", "arbitrary")),
            name=name,
        )(xb, w, cos, sin)

    pa = rope_group(COL_DQ, DIFF_HEADS, 2 * DIFF_HEADS, DIFF_QK_DIM, DIFF_QK_DIM ** -0.5 * LOG2_E, cos_a, sin_a,
                    "in_proj_diff")
    pb = rope_group(COL_MQ, MOBA_HEADS, 2 * MOBA_HEADS, HEAD_DIM, HEAD_DIM ** -0.5 * LOG2_E, cos_b, sin_b,
                    "in_proj_moba")
    nj = tiles(N_COL_BLOCKS - COL_DV)
    pc = pl.pallas_call(
        functools.partial(_proj_plain_kernel, gq_cols=((COL_GQ - COL_DV) * LANES, (COL_GK - COL_DV) * LANES)),
        grid=(t // tm, nj),
        in_specs=[x_spec, w_spec(tiles(COL_DV))],
        out_specs=out,
        out_shape=jax.ShapeDtypeStruct((t, nj * PROJ_TN), BF16),
        compiler_params=_cparams(("parallel", "arbitrary")),
        name="in_proj_plain",
    )(xb, w)
    return pa, pb, pc


_IN_WIDTHS = (DIFF_HEADS * 128, DIFF_HEADS * 128, DIFF_HEADS * 128, GLA_HEADS * GLA_DK, GLA_HEADS * GLA_DK,
              GLA_HEADS * GLA_DV, GLA_HEADS * GLA_DV, GLA_GATE_RANK, MOBA_HEADS * 128, MOBA_HEADS * 128,
              MOBA_HEADS * 128)
_IN_OFFS = tuple(int(v) for v in np.concatenate([[0], np.cumsum(_IN_WIDTHS)]))
D_IN = _IN_OFFS[-1]


def _reorder_w_in_kernel(w_ref, o_ref):
    src = dict(zip(("dq", "dk", "dv", "gq", "gk", "gv", "gr", "gg", "mq", "mk", "mv"), zip(_IN_OFFS[:-1], _IN_WIDTHS)))
    for name, col in (("dq", COL_DQ), ("dk", COL_DK), ("mq", COL_MQ), ("mk", COL_MK), ("dv", COL_DV), ("mv", COL_MV),
                      ("gq", COL_GQ), ("gk", COL_GK), ("gv", COL_GV), ("gr", COL_GR)):
        s0, width = src[name]
        o_ref[0, col * LANES:col * LANES + width, :] = w_ref[0, s0:s0 + width, :].astype(BF16)


def _reorder_w_in(w_in):
    depth, d, d_in = w_in.shape
    assert d_in == D_IN and all(o % 16 == 0 for o in _IN_OFFS)
    wt = jnp.swapaxes(w_in, 1, 2)
    tc = 128 if d % 128 == 0 else d
    return pl.pallas_call(
        _reorder_w_in_kernel,
        grid=(depth, d // tc),
        in_specs=[pl.BlockSpec((1, d_in, tc), lambda l, c: (l, 0, c))],
        out_specs=pl.BlockSpec((1, PROJ_WIDTH, tc), lambda l, c: (l, 0, c)),
        out_shape=jax.ShapeDtypeStruct((depth, PROJ_WIDTH, d), BF16),
        compiler_params=_cparams(("parallel", "parallel")),
        name="reorder_w_in",
    )(wt)


def _cast_kernel(x_ref, o_ref):
    o_ref[...] = x_ref[...].astype(o_ref.dtype)


def _cast_bf16(x, layer=None):
    l, r, c = x.shape
    tr = next((t for t in (1024, 512, 256, 128) if r % t == 0 and t * c * 4 <= (4 << 20)), r)
    first, n = (0, l) if layer is None else (layer, 1)
    return pl.pallas_call(
        _cast_kernel,
        grid=(n, r // tr),
        in_specs=[pl.BlockSpec((1, tr, c), lambda i, j: (first + i, j, 0))],
        out_specs=pl.BlockSpec((1, tr, c), lambda i, j: (i, j, 0)),
        out_shape=jax.ShapeDtypeStruct((n, r, c), BF16),
        compiler_params=_cparams(("parallel", "parallel")),
        name="cast_bf16",
    )(x)


def _store_v_transposed(v_ref, vt_ref, tk):
    seq = v_ref.shape[0]
    for blk in range(seq // tk):
        vt_ref[0:HEAD_DIM, blk * tk:(blk + 1) * tk] = v_ref[blk * tk:(blk + 1) * tk, :].astype(F32).T.astype(BF16)
    vt_ref[HEAD_DIM:V_ROWS, :] = jnp.ones((V_ROWS - HEAD_DIM, seq), BF16)


def _run_tiles(n, scores, finish):
    pending = [scores(t) for t in range(min(ATTN_LOOKAHEAD, n))]
    for t in range(n):
        if t + ATTN_LOOKAHEAD < n:
            pending.append(scores(t + ATTN_LOOKAHEAD))
        finish(t, pending.pop(0))


def _col_max(parts):
    m = jnp.max(parts[0], axis=0, keepdims=True)
    for s in parts[1:]:
        m = jnp.maximum(m, jnp.max(s, axis=0, keepdims=True))
    return m


def _diff_kernel(lam_ref, g_ref, q_ref, k_ref, v_ref, o_ref, vt_ref, *, tq, lam_init):
    seq = k_ref.shape[0]
    _store_v_transposed(v_ref, vt_ref, tq)
    lf = lam_ref[...]
    lam = (jnp.exp(jnp.sum(lf[0:1] * lf[1:2], axis=1, keepdims=True))
           - jnp.exp(jnp.sum(lf[2:3] * lf[3:4], axis=1, keepdims=True)) + lam_init)
    lane = lax.broadcasted_iota(jnp.int32, (tq, LANES), 1)
    causal = (lax.broadcasted_iota(jnp.int32, (tq, 2 * tq), 0)
              <= lax.broadcasted_iota(jnp.int32, (tq, 2 * tq), 1) % tq)
    def scores(qi):
        lo, hi = qi * tq, (qi + 1) * tq
        q = q_ref[lo:hi, :].astype(F32)
        qq = jnp.concatenate([jnp.where(lane < DIFF_QK_DIM, q, 0.0),
                              jnp.where(lane >= DIFF_QK_DIM, q, 0.0)], axis=0).astype(BF16)
        parts = [jnp.where(causal, _dot_nt(k_ref[lo:hi, :], qq), NEG_INF)]
        if qi > 0:
            parts.append(_dot_nt(k_ref[0:lo, :], qq))
        return parts

    def finish(qi, parts):
        lo, hi = qi * tq, (qi + 1) * tq
        m = _col_max(parts)
        acc = _dot(vt_ref[:, lo:hi], jnp.exp2(parts[0] - m).astype(BF16))
        if qi > 0:
            acc = acc + _dot(vt_ref[:, 0:lo], jnp.exp2(parts[1] - m).astype(BF16))
        o_t = acc[0:HEAD_DIM] / acc[HEAD_DIM:HEAD_DIM + 1]
        o = (o_t[:, :tq] - lam * o_t[:, tq:]).T
        o = o * lax.rsqrt(jnp.mean(o * o, axis=-1, keepdims=True) + LN_EPS) * g_ref[...] * (1.0 - lam_init)
        o_ref[lo:hi, :] = o.astype(BF16)

    _run_tiles(seq // tq, scores, finish)


def _diff_attention(pa, pc, lam, g, batch, seq, layer):
    tq = 256
    lam_init = 0.8 - 0.6 * math.exp(-0.3 * layer)
    col = lambda c0: pl.BlockSpec((seq, LANES), lambda b, h: (b, c0 + h))
    return pl.pallas_call(
        functools.partial(_diff_kernel, tq=tq, lam_init=lam_init),
        grid=(batch, DIFF_HEADS),
        in_specs=[pl.BlockSpec((4, DIFF_QK_DIM), lambda b, h: (0, 0)),
                  pl.BlockSpec((1, LANES), lambda b, h: (0, 0)),
                  col(0), col(DIFF_HEADS), col(0)],
        out_specs=pl.BlockSpec((seq, LANES), lambda b, h: (b, h)),
        out_shape=jax.ShapeDtypeStruct((batch * seq, DIFF_HEADS * LANES), BF16),
        scratch_shapes=[pltpu.VMEM((V_ROWS, seq), BF16)],
        compiler_params=_cparams(("parallel", "parallel")),
        name="diff_attn",
    )(lam, g, pa, pa, pc)


def _moba_kernel(q_ref, k_ref, v_ref, o_ref, vt_ref, *, nb):
    tq = MOBA_BLOCK
    _store_v_transposed(v_ref, vt_ref, tq)
    nbp = -(-nb // SUBLANES) * SUBLANES
    rows = [jnp.mean(k_ref[n * tq:(n + 1) * tq, :].astype(F32), axis=0, keepdims=True) for n in range(nb)]
    if nbp > nb:
        rows.append(jnp.zeros((nbp - nb, LANES), F32))
    km = jnp.concatenate(rows, axis=0)
    km_hi = km.astype(BF16)
    km_lo = (km - km_hi.astype(F32)).astype(BF16)
    causal = (lax.broadcasted_iota(jnp.int32, (tq, tq), 0) <= lax.broadcasted_iota(jnp.int32, (tq, tq), 1))
    def scores(qi):
        lo, hi = qi * tq, (qi + 1) * tq
        q = q_ref[lo:hi, :]
        parts = [jnp.where(causal, _dot_nt(k_ref[lo:hi, :], q), NEG_INF)]
        if qi > 0:
            s_past = _dot_nt(k_ref[0:lo, :], q)
            if qi > MOBA_TOPK:
                gate = _dot_nt(km_hi, q) + _dot_nt(km_lo, q)
                n_idx = lax.broadcasted_iota(jnp.int32, gate.shape, 0)
                rank = jnp.zeros(gate.shape, jnp.int32)
                for mb in range(qi):
                    gm = gate[mb:mb + 1, :]
                    rank = rank + jnp.where((gm > gate) | ((gm == gate) & (mb < n_idx)), 1, 0)
                bias = jnp.where(rank < MOBA_TOPK, 0.0, NEG_INF)
                s_past = jnp.concatenate([s_past[n * tq:(n + 1) * tq] + bias[n:n + 1, :] for n in range(qi)], axis=0)
            parts.append(s_past)
        return parts

    def finish(qi, parts):
        lo, hi = qi * tq, (qi + 1) * tq
        m = _col_max(parts)
        acc = _dot(vt_ref[:, lo:hi], jnp.exp2(parts[0] - m).astype(BF16))
        if qi > 0:
            acc = acc + _dot(vt_ref[:, 0:lo], jnp.exp2(parts[1] - m).astype(BF16))
        o_ref[lo:hi, :] = (acc[0:HEAD_DIM] / acc[HEAD_DIM:HEAD_DIM + 1]).T.astype(BF16)

    _run_tiles(nb, scores, finish)


def _moba_attention(pb, pc, batch, seq):
    assert seq % MOBA_BLOCK == 0
    col = lambda c0: pl.BlockSpec((seq, LANES), lambda b, h: (b, c0 + h))
    return pl.pallas_call(
        functools.partial(_moba_kernel, nb=seq // MOBA_BLOCK),
        grid=(batch, MOBA_HEADS),
        in_specs=[col(0), col(MOBA_HEADS), col(COL_MV - COL_DV)],
        out_specs=pl.BlockSpec((seq, LANES), lambda b, h: (b, h)),
        out_shape=jax.ShapeDtypeStruct((batch * seq, MOBA_HEADS * LANES), BF16),
        scratch_shapes=[pltpu.VMEM((V_ROWS, seq), BF16)],
        compiler_params=_cparams(("parallel", "parallel")),
        name="moba_attn",
    )(pb, pb, pc)


def _gla_constants():
    c = GLA_CHUNK
    idx = np.arange(c)
    mats = [np.tril(np.ones((c, c), np.float32))]
    bm = []
    for lvl in range(6):
        m = 32 >> lvl
        parent, half = idx // (2 * m), (idx // m) % 2
        ref = parent * 2 * m + m - 1
        t = idx[None, :]
        second = (half == 1)[:, None] & (t > ref[:, None]) & (t <= idx[:, None])
        first = (half == 0)[:, None] & (t > idx[:, None]) & (t <= ref[:, None])
        mats.append((second | first).astype(np.float32))
        bm.append(((parent[:, None] == parent[None, :]) & (half == 1)[:, None] & (half == 0)[None, :]).astype(np.float32))
    bm.append(np.eye(c, dtype=np.float32))
    cmat = np.concatenate(mats, axis=0)
    return jnp.asarray(cmat, BF16), jnp.asarray(np.concatenate(bm, axis=0), F32)


def _split2(x):
    hi = x.astype(BF16)
    return hi, (x - hi.astype(F32)).astype(BF16)


def _gla_kernel(cmat_ref, bm_ref, wgg_ref, up_ref, bias_ref, gn_ref, x_ref, q_ref, k_ref, v_ref, r_ref, o_ref,
                la_ref, st_ref, *, seq, heads):
    c = GLA_CHUNK
    wg = wgg_ref[...].astype(BF16)
    up_hi, up_lo = _split2(up_ref[...])
    kh = wg.shape[0] // 2
    rows = 512 if seq % 512 == 0 else seq
    for r0 in range(0, seq, rows):
        gg = _dot(x_ref[r0:r0 + rows, 0:kh], wg[0:kh]) + _dot(x_ref[r0:r0 + rows, kh:], wg[kh:])
        g_hi, g_lo = _split2(gg)
        z = _dot(g_hi, up_hi) + _dot(g_hi, up_lo) + _dot(g_lo, up_hi) + bias_ref[...]
        la_ref[r0:r0 + rows, :] = (jnp.minimum(z, 0.0) - jnp.log(1.0 + jnp.exp(-jnp.abs(z)))) * (1.0 / GLA_TAU)

    st_ref[...] = jnp.zeros_like(st_ref)
    cmat = cmat_ref[...]
    lane = lax.broadcasted_iota(jnp.int32, (c, LANES), 1)
    own = [lane < GLA_DK, lane >= GLA_DK]

    def halves(t):
        return [jnp.where(m, t, 0.0).astype(BF16) for m in own]

    def chunk_group(cp, carry):
        sls = [pl.ds(pl.multiple_of((cp * GLA_UNROLL + u) * c, c), c) for u in range(GLA_UNROLL)]
        e_alls = []
        for sl in sls:
            g_hi, g_lo = _split2(la_ref[sl, :])
            e_alls.append(_dot(cmat, g_hi) + _dot(cmat, g_lo))
        ids = [(sl, p, e_all[:, _head_cols(p)]) for sl, e_all in zip(sls, e_alls) for p in range(heads // 2)]
        qs = [q_ref[sl, _head_cols(p)].astype(F32) for sl, p, _ in ids]
        ks = [k_ref[sl, _head_cols(p)].astype(F32) for sl, p, _ in ids]
        atts = []
        for q, k in zip(qs, ks):
            kb = k.astype(BF16)
            atts.append([bm_ref[6 * c:7 * c, :] * _dot_nt(qh, kb) for qh in halves(q)])
        for lvl in range(6):
            for i, (_, _, e) in enumerate(ids):
                w = jnp.exp(e[(1 + lvl) * c:(2 + lvl) * c])
                kw = (ks[i] * w).astype(BF16)
                for hf, qh in enumerate(halves(qs[i] * w)):
                    atts[i][hf] = atts[i][hf] + bm_ref[lvl * c:(lvl + 1) * c, :] * _dot_nt(qh, kw)
        streams = []
        for (sl, p, e), q, k, att in zip(ids, qs, ks, atts):
            cum = e[0:c]
            last = cum[c - 1:c, :]
            qes, kds = halves(q * jnp.exp(cum)), halves(k * jnp.exp(last - cum))
            for hf in range(2):
                h = 2 * p + hf
                v = v_ref[sl, _head_cols(h)]
                o_intra = _dot(att[hf].astype(BF16), v)
                upd = _dot(v.astype(F32).T.astype(BF16), kds[hf])
                streams.append((sl, h, qes[hf], o_intra, upd, jnp.exp(last)))
        for args in streams:
            recur(*args)
        return carry

    def recur(sl, h, qe, o_intra, upd, decay):
        hc = _head_cols(h)
        st = st_ref[h]
        o = _dot_nt(qe, st.astype(BF16)) + o_intra
        st_ref[h] = st * decay + upd
        o = o * lax.rsqrt(jnp.mean(o * o, axis=-1, keepdims=True) + LN_EPS) * gn_ref[...]
        r = r_ref[sl, hc].astype(F32)
        o_ref[sl, hc] = (o * (r * jax.nn.sigmoid(r))).astype(BF16)

    assert (seq // c) % GLA_UNROLL == 0 and heads % 2 == 0
    lax.fori_loop(0, seq // c // GLA_UNROLL, chunk_group, 0)


def _gla(xb, pc, w_gg, up, bias, gn, batch, seq):
    cmat, bm = _gla_constants()
    heads = GLA_HEADS
    d = xb.shape[1]
    qk_w, v_w = heads * GLA_DK, heads * GLA_DV
    wgg_p = jnp.pad(w_gg, ((0, 0), (0, GG_PAD - GLA_GATE_RANK)))
    up_p = jnp.pad(up, ((0, GG_PAD - GLA_GATE_RANK), (0, 0)))
    const = lambda shape: pl.BlockSpec(shape, lambda b: (0, 0))
    col = lambda c0, width: pl.BlockSpec((seq, width), lambda b: (b, (c0 - COL_DV) * LANES // width))
    return pl.pallas_call(
        functools.partial(_gla_kernel, seq=seq, heads=heads),
        grid=(batch,),
        in_specs=[const(cmat.shape), const(bm.shape), const((d, GG_PAD)), const((GG_PAD, qk_w)), const((1, qk_w)),
                  const((1, LANES)), pl.BlockSpec((seq, d), lambda b: (b, 0)),
                  col(COL_GQ, qk_w), col(COL_GK, qk_w), col(COL_GV, v_w), col(COL_GR, v_w)],
        out_specs=pl.BlockSpec((seq, v_w), lambda b: (b, 0)),
        out_shape=jax.ShapeDtypeStruct((batch * seq, v_w), BF16),
        scratch_shapes=[pltpu.VMEM((seq, qk_w), F32), pltpu.VMEM((heads, GLA_DV, LANES), F32)],
        compiler_params=_cparams(("parallel",)),
        name="gla",
    )(cmat, bm, wgg_p, up_p, bias.reshape(1, qk_w), gn, xb, pc, pc, pc, pc)


def _emit_stream(of_ref, ob_ref, y, g_ref, b_ref, rows=slice(None)):
    out = _layer_norm(y, g_ref[...], b_ref[...])
    of_ref[rows, :] = out
    ob_ref[rows, :] = out.astype(BF16)


def _row_chunks(tm):
    rc = STREAM_ROW_CHUNK if tm % STREAM_ROW_CHUNK == 0 else tm
    return [slice(r0, r0 + rc) for r0 in range(0, tm, rc)]


def _out_kernel(od_ref, og_ref, om_ref, w_ref, xf_ref, g_ref, b_ref, of_ref, ob_ref):
    k0 = od_ref.shape[1]
    k1 = k0 + og_ref.shape[1]
    for rows in _row_chunks(xf_ref.shape[0]):
        y = (_dot(od_ref[rows, :], w_ref[0, 0:k0, :]) + _dot(og_ref[rows, :], w_ref[0, k0:k1, :])
             + _dot(om_ref[rows, :], w_ref[0, k1:, :]))
        _emit_stream(of_ref, ob_ref, DEEPNORM_ALPHA * xf_ref[rows, :] + y, g_ref, b_ref, rows)


def _row_tile(t):
    return 512 if t % 512 == 0 else t


def _out_proj(o_diff, o_gla, o_moba, w, layer, xf, g, b):
    t, d = xf.shape
    tm = _row_tile(t)
    row = pl.BlockSpec((tm, d), lambda i: (i, 0))
    vec = pl.BlockSpec((1, d), lambda i: (0, 0))
    part = lambda a: pl.BlockSpec((tm, a.shape[1]), lambda i: (i, 0))
    return pl.pallas_call(
        _out_kernel,
        grid=(t // tm,),
        in_specs=[part(o_diff), part(o_gla), part(o_moba),
                  pl.BlockSpec((1,) + w.shape[1:], lambda i: (layer, 0, 0)),
                  row, vec, vec],
        out_specs=[row, row],
        out_shape=[jax.ShapeDtypeStruct((t, d), F32), jax.ShapeDtypeStruct((t, d), BF16)],
        compiler_params=_cparams(("parallel",)),
        name="out_proj",
    )(o_diff, o_gla, o_moba, w, xf, g, b)


def _ple_kernel(xb_ref, wg_ref, pb_ref, we_ref, xf_ref, g_ref, b_ref, of_ref, ob_ref):
    for rows in _row_chunks(xf_ref.shape[0]):
        e = _dot(pb_ref[0, rows, :], we_ref[0]) * jax.nn.sigmoid(_dot(xb_ref[rows, :], wg_ref[0]))
        _emit_stream(of_ref, ob_ref, DEEPNORM_ALPHA * xf_ref[rows, :] + e, g_ref, b_ref, rows)


def _ple(xf, xb, pb, w_pe, w_pg, layer, g, b):
    t, d = xf.shape
    tm = _row_tile(t)
    row = pl.BlockSpec((tm, d), lambda i: (i, 0))
    vec = pl.BlockSpec((1, d), lambda i: (0, 0))
    whole = lambda a: pl.BlockSpec((1,) + a.shape[1:], lambda i: (layer, 0, 0))
    return pl.pallas_call(
        _ple_kernel,
        grid=(t // tm,),
        in_specs=[row, whole(w_pg),
                  pl.BlockSpec((1, tm, pb.shape[2]), lambda i: (layer, i, 0)),
                  whole(w_pe), row, vec, vec],
        out_specs=[row, row],
        out_shape=[jax.ShapeDtypeStruct((t, d), F32), jax.ShapeDtypeStruct((t, d), BF16)],
        compiler_params=_cparams(("parallel",)),
        name="ple",
    )(xb, w_pg, pb, w_pe, xf, g, b)


def kernel(x, p, positions, w_in, w_out, diff_lambda, diff_norm_g, gla_gate_up, gla_gate_b, gla_norm_g,
           ffn1_gate, ffn1_up, ffn1_down, ffn2_gate, ffn2_up, ffn2_down, w_pe, w_pg, ln_g, ln_b):
    batch, seq, d = x.shape
    t = batch * seq
    assert d == (DIFF_HEADS + GLA_HEADS + MOBA_HEADS) * HEAD_DIM and seq % MOBA_BLOCK == 0
    xf, xb = x.reshape(t, d), None
    tables = _rope_tables(positions.reshape(t, 1))
    vec = lambda v: v.reshape(1, -1)
    w_in_b, w_out_b, w_pe_b, w_pg_b = _reorder_w_in(w_in), _cast_bf16(w_out), _cast_bf16(w_pe), _cast_bf16(w_pg)
    p_b = _cast_bf16(p.reshape(DEPTH, t, -1))
    ffn_params = [prm for i in range(DEPTH) for prm in ((ffn1_gate, ffn1_up, ffn1_down, i), (ffn2_gate, ffn2_up, ffn2_down, i))]
    w_next = tuple(_cast_bf16(a, layer=ffn_params[0][3]) for a in ffn_params[0][:3])

    def ffn(k, xf, xb, g, b):
        hosted = ffn_params[k + 1] if k + 1 < len(ffn_params) else None
        return _ffn(xf, xb, w_next, 0, g, b, hosted)

    for i in range(DEPTH):
        xf, xb, w_next = ffn(2 * i, xf, xb, vec(ln_g[i, 0]), vec(ln_b[i, 0]))
        pa, pb, pc = _proj(xb, w_in_b, i, tables)
        o_diff = _diff_attention(pa, pc, diff_lambda[i], vec(diff_norm_g[i]), batch, seq, i)
        gg0 = _IN_OFFS[7]
        o_gla = _gla(xb, pc, w_in[i, :, gg0:gg0 + GLA_GATE_RANK], gla_gate_up[i], gla_gate_b[i],
                     vec(gla_norm_g[i]), batch, seq)
        o_moba = _moba_attention(pb, pc, batch, seq)
        xf, xb = _out_proj(o_diff, o_gla, o_moba, w_out_b, i, xf, vec(ln_g[i, 1]), vec(ln_b[i, 1]))
        xf, xb, w_next = ffn(2 * i + 1, xf, xb, vec(ln_g[i, 2]), vec(ln_b[i, 2]))
        xf, xb = _ple(xf, xb, p_b, w_pe_b, w_pg_b, i, vec(ln_g[i, 3]), vec(ln_b[i, 3]))
    return xf.reshape(batch, seq, d)
```

```python
import functools
import math

import numpy as np
import jax
import jax.numpy as jnp
from jax import lax
from jax.experimental import pallas as pl
from jax.experimental.pallas import tpu as pltpu

F32 = jnp.float32
BF16 = jnp.bfloat16

DEPTH = 2
HEAD_DIM = 128
DIFF_HEADS = 6
GLA_HEADS = 4
MOBA_HEADS = 6
DIFF_QK_DIM = 64
GLA_DK = 64
GLA_DV = 128
GLA_GATE_RANK = 16
GLA_TAU = 16.0
GLA_CHUNK = 64
GLA_UNROLL = 4
MOBA_BLOCK = 256
MOBA_TOPK = 3
ROPE_THETA = 10000.0
LN_EPS = 1e-5
DEEPNORM_ALPHA = (2 * DEPTH) ** 0.25

LANES = 128
SUBLANES = 8
VMEM_LIMIT_BYTES = 56 * 1024 * 1024

PROJ_TN = 768
PROJ_ROW_CHUNK = 256
STREAM_ROW_CHUNK = 128
_CB = LANES
COL_DQ = 0
COL_DK = COL_DQ + DIFF_HEADS
COL_MQ = COL_DK + DIFF_HEADS
COL_MK = COL_MQ + MOBA_HEADS
COL_DV = COL_MK + MOBA_HEADS
COL_MV = COL_DV + DIFF_HEADS
COL_GQ = COL_MV + MOBA_HEADS
COL_GK = COL_GQ + GLA_HEADS * GLA_DK // LANES
COL_GV = COL_GK + GLA_HEADS * GLA_DK // LANES
COL_GR = COL_GV + GLA_HEADS
N_COL_BLOCKS = COL_GR + GLA_HEADS
PROJ_WIDTH = N_COL_BLOCKS * _CB
assert PROJ_WIDTH % PROJ_TN == 0
GG_PAD = LANES

NEG_INF = float("-inf")
LOG2_E = math.log2(math.e)
ATTN_LOOKAHEAD = 2
V_ROWS = HEAD_DIM + 16


def _cparams(sem):
    return pltpu.CompilerParams(dimension_semantics=sem, vmem_limit_bytes=VMEM_LIMIT_BYTES)


def _layer_norm(y, g, b):
    mu = jnp.mean(y, axis=-1, keepdims=True)
    d = y - mu
    var = jnp.mean(d * d, axis=-1, keepdims=True)
    return d * lax.rsqrt(var + LN_EPS) * g + b


def _dot_nt(a, b):
    return lax.dot_general(a, b, (((1,), (1,)), ((), ())), preferred_element_type=F32)


def _dot(a, b):
    return jnp.dot(a, b, preferred_element_type=F32)


def _head_cols(g):
    return slice(g * LANES, (g + 1) * LANES)


def _rope_table_kernel(pos_ref, inv_a_ref, inv_b_ref, sgn_a_ref, sgn_b_ref,
                       cos_a_ref, sin_a_ref, cos_b_ref, sin_b_ref):
    pos = pos_ref[...].astype(F32)
    ang_a = pos * inv_a_ref[...]
    ang_b = pos * inv_b_ref[...]
    cos_a_ref[...] = jnp.cos(ang_a)
    sin_a_ref[...] = jnp.sin(ang_a) * sgn_a_ref[...]
    cos_b_ref[...] = jnp.cos(ang_b)
    sin_b_ref[...] = jnp.sin(ang_b) * sgn_b_ref[...]


def _rope_tables(pos):
    t = pos.shape[0]
    tm = next(c for c in (1024, 512, MOBA_BLOCK) if t % c == 0)

    def pattern(d):
        inv = ROPE_THETA ** (-jnp.arange(0, d, 2, dtype=F32) / d)
        inv = jnp.tile(inv, 2 * LANES // d)[None, :]
        sgn = jnp.tile(jnp.concatenate([-jnp.ones(d // 2, F32), jnp.ones(d // 2, F32)]), LANES // d)[None, :]
        return inv, sgn

    inv_a, sgn_a = pattern(DIFF_QK_DIM)
    inv_b, sgn_b = pattern(HEAD_DIM)
    row = pl.BlockSpec((1, LANES), lambda i: (0, 0))
    out = pl.BlockSpec((tm, LANES), lambda i: (i, 0))
    return pl.pallas_call(
        _rope_table_kernel,
        grid=(t // tm,),
        in_specs=[pl.BlockSpec((tm, 1), lambda i: (i, 0)), row, row, row, row],
        out_specs=[out, out, out, out],
        out_shape=[jax.ShapeDtypeStruct((t, LANES), F32)] * 4,
        compiler_params=_cparams(("parallel",)),
        name="rope_tables",
    )(pos, inv_a, inv_b, sgn_a, sgn_b)


def _ffn_kernel(*refs, nf, has_xb, n_hosted):
    refs = list(refs)
    xf_ref = refs.pop(0)
    xb_ref = refs.pop(0) if has_xb else None
    wg_ref, wu_ref, wd_ref, g_ref, b_ref = refs[:5]
    src_refs = refs[5:5 + n_hosted]
    of_ref, ob_ref = refs[5 + n_hosted:7 + n_hosted]
    dst_refs = refs[7 + n_hosted:7 + 2 * n_hosted]
    acc_ref = refs[-1]
    j = pl.program_id(1)

    @pl.when(j == 0)
    def _():
        acc_ref[...] = jnp.zeros_like(acc_ref)

    x = xb_ref[...] if has_xb else xf_ref[...].astype(BF16)
    gt = _dot(x, wg_ref[0])
    ut = _dot(x, wu_ref[0])
    h = (gt * jax.nn.sigmoid(gt) * ut).astype(BF16)
    acc_ref[...] += _dot(h, wd_ref[0])
    for src, dst in zip(src_refs, dst_refs):
        dst[...] = src[...].astype(BF16)

    @pl.when(j == nf - 1)
    def _():
        _emit_stream(of_ref, ob_ref, DEEPNORM_ALPHA * xf_ref[...] + 0.5 * acc_ref[...], g_ref, b_ref)


def _ffn_tiles(t, f):
    tm = 512 if t % 512 == 0 else t
    tf = 512 if f % 512 == 0 else f
    return tm, tf


def _ffn(xf, xb, w, layer, g, b, hosted=None):
    wg, wu, wd = w
    t, d = xf.shape
    f = wg.shape[2]
    tm, tf = _ffn_tiles(t, f)
    nf, ni = f // tf, t // tm
    row = pl.BlockSpec((tm, d), lambda i, j: (i, 0))
    vec = pl.BlockSpec((1, d), lambda i, j: (0, 0))
    in_specs = [row] + ([row] if xb is not None else []) + [
        pl.BlockSpec((1, d, tf), lambda i, j: (layer, 0, j)),
        pl.BlockSpec((1, d, tf), lambda i, j: (layer, 0, j)),
        pl.BlockSpec((1, tf, d), lambda i, j: (layer, j, 0)),
        vec, vec]
    args = [xf] + ([xb] if xb is not None else []) + [wg, wu, wd, g, b]
    out_specs = [row, row]
    out_shape = [jax.ShapeDtypeStruct((t, d), F32), jax.ShapeDtypeStruct((t, d), BF16)]
    n_hosted = 0
    if hosted is not None:
        hg, hu, hd, src = hosted
        dr = d // ni
        assert d % ni == 0 and dr % LANES == 0
        in_specs += [pl.BlockSpec((1, dr, tf), lambda i, j: (src, i, j)),
                     pl.BlockSpec((1, dr, tf), lambda i, j: (src, i, j)),
                     pl.BlockSpec((1, tf, dr), lambda i, j: (src, j, i))]
        args += [hg, hu, hd]
        out_specs += [pl.BlockSpec((1, dr, tf), lambda i, j: (0, i, j)),
                      pl.BlockSpec((1, dr, tf), lambda i, j: (0, i, j)),
                      pl.BlockSpec((1, tf, dr), lambda i, j: (0, j, i))]
        out_shape += [jax.ShapeDtypeStruct((1,) + a.shape[1:], BF16) for a in (hg, hu, hd)]
        n_hosted = 3
    outs = pl.pallas_call(
        functools.partial(_ffn_kernel, nf=nf, has_xb=xb is not None, n_hosted=n_hosted),
        grid=(ni, nf),
        in_specs=in_specs,
        out_specs=out_specs,
        out_shape=out_shape,
        scratch_shapes=[pltpu.VMEM((tm, d), F32)],
        compiler_params=_cparams(("parallel", "arbitrary")),
        name="ffn",
    )(*args)
    return outs[0], outs[1], (tuple(outs[2:]) if hosted is not None else None)


def _proj_rope_kernel(xb_ref, w_ref, cos_ref, sin_ref, p_ref, *, rot_dim, q_scale, q_tiles):
    j = pl.program_id(1)
    scale = jnp.where(j < q_tiles, q_scale, 1.0).astype(F32)
    half = rot_dim // 2
    tm = xb_ref.shape[0]
    rc = PROJ_ROW_CHUNK if tm % PROJ_ROW_CHUNK == 0 else tm
    for r0 in range(0, tm, rc):
        rows = slice(r0, r0 + rc)
        acc = _dot_nt(xb_ref[rows, :], w_ref[0])
        cos, sin = cos_ref[rows, :], sin_ref[rows, :]
        for hh in range(PROJ_TN // LANES):
            t = acc[:, _head_cols(hh)]
            if rot_dim == LANES:
                rot = pltpu.roll(t, half, 1)
            else:
                lane = lax.broadcasted_iota(jnp.int32, t.shape, 1)
                rot = jnp.where((lane // half) % 2 == 0, pltpu.roll(t, LANES - half, 1), pltpu.roll(t, half, 1))
            p_ref[rows, _head_cols(hh)] = ((t * cos + rot * sin) * scale).astype(BF16)


def _proj_plain_kernel(xb_ref, w_ref, p_ref, *, gq_cols):
    j = pl.program_id(1)
    acc = _dot_nt(xb_ref[...], w_ref[0])
    col = j * PROJ_TN + lax.broadcasted_iota(jnp.int32, (1, PROJ_TN), 1)
    scale = jnp.where((col >= gq_cols[0]) & (col < gq_cols[1]), GLA_DK ** -0.5, 1.0).astype(F32)
    p_ref[...] = (acc * scale).astype(BF16)


def _proj(xb, w, layer, tables):
    t, d = xb.shape
    tm = 2048 if t % 2048 == 0 else t

    def tiles(c):
        assert (c * LANES) % PROJ_TN == 0
        return c * LANES // PROJ_TN

    cos_a, sin_a, cos_b, sin_b = tables
    x_spec = pl.BlockSpec((tm, d), lambda i, j: (i, 0))
    tab = pl.BlockSpec((tm, LANES), lambda i, j: (i, 0))
    out = pl.BlockSpec((tm, PROJ_TN), lambda i, j: (i, j))
    w_spec = lambda first: pl.BlockSpec((1, PROJ_TN, d), lambda i, j: (layer, first + j, 0))

    def rope_group(first_col, q_heads, n_heads, rot_dim, q_scale, cos, sin, name):
        nj = tiles(n_heads)
        return pl.pallas_call(
            functools.partial(_proj_rope_kernel, rot_dim=rot_dim, q_scale=q_scale, q_tiles=tiles(q_heads)),
            grid=(t // tm, nj),
            in_specs=[x_spec, w_spec(tiles(first_col)), tab, tab],
            out_specs=out,
            out_shape=jax.ShapeDtypeStruct((t, nj * PROJ_TN), BF16),
            compiler_params=_cparams(("parallel", "arbitrary")),
            name=name,
        )(xb, w, cos, sin)

    pa = rope_group(COL_DQ, DIFF_HEADS, 2 * DIFF_HEADS, DIFF_QK_DIM, DIFF_QK_DIM ** -0.5 * LOG2_E, cos_a, sin_a,
                    "in_proj_diff")
    pb = rope_group(COL_MQ, MOBA_HEADS, 2 * MOBA_HEADS, HEAD_DIM, HEAD_DIM ** -0.5 * LOG2_E, cos_b, sin_b,
                    "in_proj_moba")
    nj = tiles(N_COL_BLOCKS - COL_DV)
    pc = pl.pallas_call(
        functools.partial(_proj_plain_kernel, gq_cols=((COL_GQ - COL_DV) * LANES, (COL_GK - COL_DV) * LANES)),
        grid=(t // tm, nj),
        in_specs=[x_spec, w_spec(tiles(COL_DV))],
        out_specs=out,
        out_shape=jax.ShapeDtypeStruct((t, nj * PROJ_TN), BF16),
        compiler_params=_cparams(("parallel", "arbitrary")),
        name="in_proj_plain",
    )(xb, w)
    return pa, pb, pc


_IN_WIDTHS = (DIFF_HEADS * 128, DIFF_HEADS * 128, DIFF_HEADS * 128, GLA_HEADS * GLA_DK, GLA_HEADS * GLA_DK,
              GLA_HEADS * GLA_DV, GLA_HEADS * GLA_DV, GLA_GATE_RANK, MOBA_HEADS * 128, MOBA_HEADS * 128,
              MOBA_HEADS * 128)
_IN_OFFS = tuple(int(v) for v in np.concatenate([[0], np.cumsum(_IN_WIDTHS)]))
D_IN = _IN_OFFS[-1]


def _reorder_w_in_kernel(w_ref, o_ref):
    src = dict(zip(("dq", "dk", "dv", "gq", "gk", "gv", "gr", "gg", "mq", "mk", "mv"), zip(_IN_OFFS[:-1], _IN_WIDTHS)))
    for name, col in (("dq", COL_DQ), ("dk", COL_DK), ("mq", COL_MQ), ("mk", COL_MK), ("dv", COL_DV), ("mv", COL_MV),
                      ("gq", COL_GQ), ("gk", COL_GK), ("gv", COL_GV), ("gr", COL_GR)):
        s0, width = src[name]
        o_ref[0, col * LANES:col * LANES + width, :] = w_ref[0, s0:s0 + width, :].astype(BF16)


def _reorder_w_in(w_in):
    depth, d, d_in = w_in.shape
    assert d_in == D_IN and all(o % 16 == 0 for o in _IN_OFFS)
    wt = jnp.swapaxes(w_in, 1, 2)
    tc = 128 if d % 128 == 0 else d
    return pl.pallas_call(
        _reorder_w_in_kernel,
        grid=(depth, d // tc),
        in_specs=[pl.BlockSpec((1, d_in, tc), lambda l, c: (l, 0, c))],
        out_specs=pl.BlockSpec((1, PROJ_WIDTH, tc), lambda l, c: (l, 0, c)),
        out_shape=jax.ShapeDtypeStruct((depth, PROJ_WIDTH, d), BF16),
        compiler_params=_cparams(("parallel", "parallel")),
        name="reorder_w_in",
    )(wt)


def _cast_kernel(x_ref, o_ref):
    o_ref[...] = x_ref[...].astype(o_ref.dtype)


def _cast_bf16(x, layer=None):
    l, r, c = x.shape
    tr = next((t for t in (1024, 512, 256, 128) if r % t == 0 and t * c * 4 <= (4 << 20)), r)
    first, n = (0, l) if layer is None else (layer, 1)
    return pl.pallas_call(
        _cast_kernel,
        grid=(n, r // tr),
        in_specs=[pl.BlockSpec((1, tr, c), lambda i, j: (first + i, j, 0))],
        out_specs=pl.BlockSpec((1, tr, c), lambda i, j: (i, j, 0)),
        out_shape=jax.ShapeDtypeStruct((n, r, c), BF16),
        compiler_params=_cparams(("parallel", "parallel")),
        name="cast_bf16",
    )(x)


def _store_v_transposed(v_ref, vt_ref, tk):
    seq = v_ref.shape[0]
    for blk in range(seq // tk):
        vt_ref[0:HEAD_DIM, blk * tk:(blk + 1) * tk] = v_ref[blk * tk:(blk + 1) * tk, :].astype(F32).T.astype(BF16)
    vt_ref[HEAD_DIM:V_ROWS, :] = jnp.ones((V_ROWS - HEAD_DIM, seq), BF16)


def _run_tiles(n, scores, finish):
    pending = [scores(t) for t in range(min(ATTN_LOOKAHEAD, n))]
    for t in range(n):
        if t + ATTN_LOOKAHEAD < n:
            pending.append(scores(t + ATTN_LOOKAHEAD))
        finish(t, pending.pop(0))


def _col_max(parts):
    m = jnp.max(parts[0], axis=0, keepdims=True)
    for s in parts[1:]:
        m = jnp.maximum(m, jnp.max(s, axis=0, keepdims=True))
    return m


def _diff_kernel(lam_ref, g_ref, q_ref, k_ref, v_ref, o_ref, vt_ref, *, tq, lam_init):
    seq = k_ref.shape[0]
    _store_v_transposed(v_ref, vt_ref, tq)
    lf = lam_ref[...]
    lam = (jnp.exp(jnp.sum(lf[0:1] * lf[1:2], axis=1, keepdims=True))
           - jnp.exp(jnp.sum(lf[2:3] * lf[3:4], axis=1, keepdims=True)) + lam_init)
    lane = lax.broadcasted_iota(jnp.int32, (tq, LANES), 1)
    causal = (lax.broadcasted_iota(jnp.int32, (tq, 2 * tq), 0)
              <= lax.broadcasted_iota(jnp.int32, (tq, 2 * tq), 1) % tq)
    def scores(qi):
        lo, hi = qi * tq, (qi + 1) * tq
        q = q_ref[lo:hi, :].astype(F32)
        qq = jnp.concatenate([jnp.where(lane < DIFF_QK_DIM, q, 0.0),
                              jnp.where(lane >= DIFF_QK_DIM, q, 0.0)], axis=0).astype(BF16)
        parts = [jnp.where(causal, _dot_nt(k_ref[lo:hi, :], qq), NEG_INF)]
        if qi > 0:
            parts.append(_dot_nt(k_ref[0:lo, :], qq))
        return parts

    def finish(qi, parts):
        lo, hi = qi * tq, (qi + 1) * tq
        m = _col_max(parts)
        acc = _dot(vt_ref[:, lo:hi], jnp.exp2(parts[0] - m).astype(BF16))
        if qi > 0:
            acc = acc + _dot(vt_ref[:, 0:lo], jnp.exp2(parts[1] - m).astype(BF16))
        o_t = acc[0:HEAD_DIM] / acc[HEAD_DIM:HEAD_DIM + 1]
        o = (o_t[:, :tq] - lam * o_t[:, tq:]).T
        o = o * lax.rsqrt(jnp.mean(o * o, axis=-1, keepdims=True) + LN_EPS) * g_ref[...] * (1.0 - lam_init)
        o_ref[lo:hi, :] = o.astype(BF16)

    _run_tiles(seq // tq, scores, finish)


def _with_hosted_casts(kernel_fn, n_in, n_out, n_hosted):
    def wrapped(*refs):
        ins, rest = refs[:n_in], refs[n_in:]
        srcs, rest = rest[:n_hosted], rest[n_hosted:]
        outs, rest = rest[:n_out], rest[n_out:]
        dsts, scratch = rest[:n_hosted], rest[n_hosted:]
        kernel_fn(*ins, *outs, *scratch)
        for src, dst in zip(srcs, dsts):
            dst[...] = src[...].astype(BF16)
    return wrapped


def _hosted_specs(hosted, n_heads):
    ins, outs, shapes = [], [], []
    for arr, layer, nblk in hosted:
        _, r, c = arr.shape
        assert r % nblk == 0
        blk = lambda b, h, nblk=nblk: jnp.minimum(b * n_heads + h, nblk - 1)
        ins.append(pl.BlockSpec((1, r // nblk, c), lambda b, h, layer=layer, blk=blk: (layer, blk(b, h), 0)))
        outs.append(pl.BlockSpec((1, r // nblk, c), lambda b, h, blk=blk: (0, blk(b, h), 0)))
        shapes.append(jax.ShapeDtypeStruct((1, r, c), BF16))
    return ins, outs, shapes, [arr for arr, _, _ in hosted]


def _diff_attention(pa, pc, lam, g, batch, seq, layer, hosted=()):
    tq = 256
    lam_init = 0.8 - 0.6 * math.exp(-0.3 * layer)
    col = lambda c0: pl.BlockSpec((seq, LANES), lambda b, h: (b, c0 + h))
    h_in, h_out, h_shape, h_args = _hosted_specs(hosted, DIFF_HEADS)
    assert batch * DIFF_HEADS >= max([n for _, _, n in hosted], default=0)
    return pl.pallas_call(
        _with_hosted_casts(functools.partial(_diff_kernel, tq=tq, lam_init=lam_init), 5, 1, len(hosted)),
        grid=(batch, DIFF_HEADS),
        in_specs=[pl.BlockSpec((4, DIFF_QK_DIM), lambda b, h: (0, 0)),
                  pl.BlockSpec((1, LANES), lambda b, h: (0, 0)),
                  col(0), col(DIFF_HEADS), col(0)] + h_in,
        out_specs=[pl.BlockSpec((seq, LANES), lambda b, h: (b, h))] + h_out,
        out_shape=[jax.ShapeDtypeStruct((batch * seq, DIFF_HEADS * LANES), BF16)] + h_shape,
        scratch_shapes=[pltpu.VMEM((V_ROWS, seq), BF16)],
        compiler_params=_cparams(("arbitrary", "arbitrary")),
        name="diff_attn",
    )(lam, g, pa, pa, pc, *h_args)


def _moba_kernel(q_ref, k_ref, v_ref, o_ref, vt_ref, *, nb):
    tq = MOBA_BLOCK
    _store_v_transposed(v_ref, vt_ref, tq)
    nbp = -(-nb // SUBLANES) * SUBLANES
    rows = [jnp.mean(k_ref[n * tq:(n + 1) * tq, :].astype(F32), axis=0, keepdims=True) for n in range(nb)]
    if nbp > nb:
        rows.append(jnp.zeros((nbp - nb, LANES), F32))
    km = jnp.concatenate(rows, axis=0)
    km_hi = km.astype(BF16)
    km_lo = (km - km_hi.astype(F32)).astype(BF16)
    causal = (lax.broadcasted_iota(jnp.int32, (tq, tq), 0) <= lax.broadcasted_iota(jnp.int32, (tq, tq), 1))
    def scores(qi):
        lo, hi = qi * tq, (qi + 1) * tq
        q = q_ref[lo:hi, :]
        parts = [jnp.where(causal, _dot_nt(k_ref[lo:hi, :], q), NEG_INF)]
        if qi > 0:
            s_past = _dot_nt(k_ref[0:lo, :], q)
            if qi > MOBA_TOPK:
                gate = _dot_nt(km_hi, q) + _dot_nt(km_lo, q)
                n_idx = lax.broadcasted_iota(jnp.int32, gate.shape, 0)
                rank = jnp.zeros(gate.shape, jnp.int32)
                for mb in range(qi):
                    gm = gate[mb:mb + 1, :]
                    rank = rank + jnp.where((gm > gate) | ((gm == gate) & (mb < n_idx)), 1, 0)
                bias = jnp.where(rank < MOBA_TOPK, 0.0, NEG_INF)
                s_past = jnp.concatenate([s_past[n * tq:(n + 1) * tq] + bias[n:n + 1, :] for n in range(qi)], axis=0)
            parts.append(s_past)
        return parts

    def finish(qi, parts):
        lo, hi = qi * tq, (qi + 1) * tq
        m = _col_max(parts)
        acc = _dot(vt_ref[:, lo:hi], jnp.exp2(parts[0] - m).astype(BF16))
        if qi > 0:
            acc = acc + _dot(vt_ref[:, 0:lo], jnp.exp2(parts[1] - m).astype(BF16))
        o_ref[lo:hi, :] = (acc[0:HEAD_DIM] / acc[HEAD_DIM:HEAD_DIM + 1]).T.astype(BF16)

    _run_tiles(nb, scores, finish)


def _moba_attention(pb, pc, batch, seq, hosted=()):
    assert seq % MOBA_BLOCK == 0
    col = lambda c0: pl.BlockSpec((seq, LANES), lambda b, h: (b, c0 + h))
    h_in, h_out, h_shape, h_args = _hosted_specs(hosted, MOBA_HEADS)
    assert batch * MOBA_HEADS >= max([n for _, _, n in hosted], default=0)
    return pl.pallas_call(
        _with_hosted_casts(functools.partial(_moba_kernel, nb=seq // MOBA_BLOCK), 3, 1, len(hosted)),
        grid=(batch, MOBA_HEADS),
        in_specs=[col(0), col(MOBA_HEADS), col(COL_MV - COL_DV)] + h_in,
        out_specs=[pl.BlockSpec((seq, LANES), lambda b, h: (b, h))] + h_out,
        out_shape=[jax.ShapeDtypeStruct((batch * seq, MOBA_HEADS * LANES), BF16)] + h_shape,
        scratch_shapes=[pltpu.VMEM((V_ROWS, seq), BF16)],
        compiler_params=_cparams(("arbitrary", "arbitrary")),
        name="moba_attn",
    )(pb, pb, pc, *h_args)


def _gla_constants():
    c = GLA_CHUNK
    idx = np.arange(c)
    mats = [np.tril(np.ones((c, c), np.float32))]
    bm = []
    for lvl in range(6):
        m = 32 >> lvl
        parent, half = idx // (2 * m), (idx // m) % 2
        ref = parent * 2 * m + m - 1
        t = idx[None, :]
        second = (half == 1)[:, None] & (t > ref[:, None]) & (t <= idx[:, None])
        first = (half == 0)[:, None] & (t > idx[:, None]) & (t <= ref[:, None])
        mats.append((second | first).astype(np.float32))
        bm.append(((parent[:, None] == parent[None, :]) & (half == 1)[:, None] & (half == 0)[None, :]).astype(np.float32))
    bm.append(np.eye(c, dtype=np.float32))
    cmat = np.concatenate(mats, axis=0)
    return jnp.asarray(cmat, BF16), jnp.asarray(np.concatenate(bm, axis=0), F32)


def _split2(x):
    hi = x.astype(BF16)
    return hi, (x - hi.astype(F32)).astype(BF16)


def _gla_kernel(cmat_ref, bm_ref, wgg_ref, up_ref, bias_ref, gn_ref, x_ref, q_ref, k_ref, v_ref, r_ref, o_ref,
                la_ref, st_ref, *, seq, heads):
    c = GLA_CHUNK
    wg = wgg_ref[...].astype(BF16)
    up_hi, up_lo = _split2(up_ref[...])
    kh = wg.shape[0] // 2
    rows = 512 if seq % 512 == 0 else seq
    for r0 in range(0, seq, rows):
        gg = _dot(x_ref[r0:r0 + rows, 0:kh], wg[0:kh]) + _dot(x_ref[r0:r0 + rows, kh:], wg[kh:])
        g_hi, g_lo = _split2(gg)
        z = _dot(g_hi, up_hi) + _dot(g_hi, up_lo) + _dot(g_lo, up_hi) + bias_ref[...]
        la_ref[r0:r0 + rows, :] = (jnp.minimum(z, 0.0) - jnp.log(1.0 + jnp.exp(-jnp.abs(z)))) * (1.0 / GLA_TAU)

    st_ref[...] = jnp.zeros_like(st_ref)
    cmat = cmat_ref[...]
    lane = lax.broadcasted_iota(jnp.int32, (c, LANES), 1)
    own = [lane < GLA_DK, lane >= GLA_DK]

    def halves(t):
        return [jnp.where(m, t, 0.0).astype(BF16) for m in own]

    def chunk_group(cp, carry):
        sls = [pl.ds(pl.multiple_of((cp * GLA_UNROLL + u) * c, c), c) for u in range(GLA_UNROLL)]
        e_alls = []
        for sl in sls:
            g_hi, g_lo = _split2(la_ref[sl, :])
            e_alls.append(_dot(cmat, g_hi) + _dot(cmat, g_lo))
        ids = [(sl, p, e_all[:, _head_cols(p)]) for sl, e_all in zip(sls, e_alls) for p in range(heads // 2)]
        qs = [q_ref[sl, _head_cols(p)].astype(F32) for sl, p, _ in ids]
        ks = [k_ref[sl, _head_cols(p)].astype(F32) for sl, p, _ in ids]
        atts = []
        for q, k in zip(qs, ks):
            kb = k.astype(BF16)
            atts.append([bm_ref[6 * c:7 * c, :] * _dot_nt(qh, kb) for qh in halves(q)])
        for lvl in range(6):
            for i, (_, _, e) in enumerate(ids):
                w = jnp.exp(e[(1 + lvl) * c:(2 + lvl) * c])
                kw = (ks[i] * w).astype(BF16)
                for hf, qh in enumerate(halves(qs[i] * w)):
                    atts[i][hf] = atts[i][hf] + bm_ref[lvl * c:(lvl + 1) * c, :] * _dot_nt(qh, kw)
        streams = []
        for (sl, p, e), q, k, att in zip(ids, qs, ks, atts):
            cum = e[0:c]
            last = cum[c - 1:c, :]
            qes, kds = halves(q * jnp.exp(cum)), halves(k * jnp.exp(last - cum))
            for hf in range(2):
                h = 2 * p + hf
                v = v_ref[sl, _head_cols(h)]
                o_intra = _dot(att[hf].astype(BF16), v)
                upd = _dot(v.astype(F32).T.astype(BF16), kds[hf])
                streams.append((sl, h, qes[hf], o_intra, upd, jnp.exp(last)))
        for args in streams:
            recur(*args)
        return carry

    def recur(sl, h, qe, o_intra, upd, decay):
        hc = _head_cols(h)
        st = st_ref[h]
        o = _dot_nt(qe, st.astype(BF16)) + o_intra
        st_ref[h] = st * decay + upd
        o = o * lax.rsqrt(jnp.mean(o * o, axis=-1, keepdims=True) + LN_EPS) * gn_ref[...]
        r = r_ref[sl, hc].astype(F32)
        o_ref[sl, hc] = (o * (r * jax.nn.sigmoid(r))).astype(BF16)

    assert (seq // c) % GLA_UNROLL == 0 and heads % 2 == 0
    lax.fori_loop(0, seq // c // GLA_UNROLL, chunk_group, 0)


def _gla(xb, pc, w_gg, up, bias, gn, batch, seq):
    cmat, bm = _gla_constants()
    heads = GLA_HEADS
    d = xb.shape[1]
    qk_w, v_w = heads * GLA_DK, heads * GLA_DV
    wgg_p = jnp.pad(w_gg, ((0, 0), (0, GG_PAD - GLA_GATE_RANK)))
    up_p = jnp.pad(up, ((0, GG_PAD - GLA_GATE_RANK), (0, 0)))
    const = lambda shape: pl.BlockSpec(shape, lambda b: (0, 0))
    col = lambda c0, width: pl.BlockSpec((seq, width), lambda b: (b, (c0 - COL_DV) * LANES // width))
    return pl.pallas_call(
        functools.partial(_gla_kernel, seq=seq, heads=heads),
        grid=(batch,),
        in_specs=[const(cmat.shape), const(bm.shape), const((d, GG_PAD)), const((GG_PAD, qk_w)), const((1, qk_w)),
                  const((1, LANES)), pl.BlockSpec((seq, d), lambda b: (b, 0)),
                  col(COL_GQ, qk_w), col(COL_GK, qk_w), col(COL_GV, v_w), col(COL_GR, v_w)],
        out_specs=pl.BlockSpec((seq, v_w), lambda b: (b, 0)),
        out_shape=jax.ShapeDtypeStruct((batch * seq, v_w), BF16),
        scratch_shapes=[pltpu.VMEM((seq, qk_w), F32), pltpu.VMEM((heads, GLA_DV, LANES), F32)],
        compiler_params=_cparams(("parallel",)),
        name="gla",
    )(cmat, bm, wgg_p, up_p, bias.reshape(1, qk_w), gn, xb, pc, pc, pc, pc)


def _emit_stream(of_ref, ob_ref, y, g_ref, b_ref, rows=slice(None)):
    out = _layer_norm(y, g_ref[...], b_ref[...])
    of_ref[rows, :] = out
    ob_ref[rows, :] = out.astype(BF16)


def _row_chunks(tm):
    rc = STREAM_ROW_CHUNK if tm % STREAM_ROW_CHUNK == 0 else tm
    return [slice(r0, r0 + rc) for r0 in range(0, tm, rc)]


def _out_kernel(od_ref, og_ref, om_ref, w_ref, xf_ref, g_ref, b_ref, of_ref, ob_ref):
    k0 = od_ref.shape[1]
    k1 = k0 + og_ref.shape[1]
    for rows in _row_chunks(xf_ref.shape[0]):
        y = (_dot(od_ref[rows, :], w_ref[0, 0:k0, :]) + _dot(og_ref[rows, :], w_ref[0, k0:k1, :])
             + _dot(om_ref[rows, :], w_ref[0, k1:, :]))
        _emit_stream(of_ref, ob_ref, DEEPNORM_ALPHA * xf_ref[rows, :] + y, g_ref, b_ref, rows)


def _row_tile(t):
    return 512 if t % 512 == 0 else t


def _out_proj(o_diff, o_gla, o_moba, w, layer, xf, g, b):
    t, d = xf.shape
    tm = _row_tile(t)
    row = pl.BlockSpec((tm, d), lambda i: (i, 0))
    vec = pl.BlockSpec((1, d), lambda i: (0, 0))
    part = lambda a: pl.BlockSpec((tm, a.shape[1]), lambda i: (i, 0))
    return pl.pallas_call(
        _out_kernel,
        grid=(t // tm,),
        in_specs=[part(o_diff), part(o_gla), part(o_moba),
                  pl.BlockSpec((1,) + w.shape[1:], lambda i: (layer, 0, 0)),
                  row, vec, vec],
        out_specs=[row, row],
        out_shape=[jax.ShapeDtypeStruct((t, d), F32), jax.ShapeDtypeStruct((t, d), BF16)],
        compiler_params=_cparams(("parallel",)),
        name="out_proj",
    )(o_diff, o_gla, o_moba, w, xf, g, b)


def _ple_kernel(xb_ref, wg_ref, pb_ref, we_ref, xf_ref, g_ref, b_ref, of_ref, ob_ref):
    for rows in _row_chunks(xf_ref.shape[0]):
        e = _dot(pb_ref[0, rows, :], we_ref[0]) * jax.nn.sigmoid(_dot(xb_ref[rows, :], wg_ref[0]))
        _emit_stream(of_ref, ob_ref, DEEPNORM_ALPHA * xf_ref[rows, :] + e, g_ref, b_ref, rows)


def _ple(xf, xb, pb, w_pe, pe_layer, w_pg, g, b):
    t, d = xf.shape
    tm = _row_tile(t)
    row = pl.BlockSpec((tm, d), lambda i: (i, 0))
    vec = pl.BlockSpec((1, d), lambda i: (0, 0))
    whole = lambda a, layer: pl.BlockSpec((1,) + a.shape[1:], lambda i: (layer, 0, 0))
    return pl.pallas_call(
        _ple_kernel,
        grid=(t // tm,),
        in_specs=[row, whole(w_pg, 0),
                  pl.BlockSpec((1, tm, pb.shape[2]), lambda i: (0, i, 0)),
                  whole(w_pe, pe_layer), row, vec, vec],
        out_specs=[row, row],
        out_shape=[jax.ShapeDtypeStruct((t, d), F32), jax.ShapeDtypeStruct((t, d), BF16)],
        compiler_params=_cparams(("parallel",)),
        name="ple",
    )(xb, w_pg, pb, w_pe, xf, g, b)


def kernel(x, p, positions, w_in, w_out, diff_lambda, diff_norm_g, gla_gate_up, gla_gate_b, gla_norm_g,
           ffn1_gate, ffn1_up, ffn1_down, ffn2_gate, ffn2_up, ffn2_down, w_pe, w_pg, ln_g, ln_b):
    batch, seq, d = x.shape
    t = batch * seq
    assert d == (DIFF_HEADS + GLA_HEADS + MOBA_HEADS) * HEAD_DIM and seq % MOBA_BLOCK == 0
    xf, xb = x.reshape(t, d), None
    tables = _rope_tables(positions.reshape(t, 1))
    vec = lambda v: v.reshape(1, -1)
    w_in_b, w_pe_b = _reorder_w_in(w_in), _cast_bf16(w_pe)
    p3 = p.reshape(DEPTH, t, -1)
    side_blocks = max(n for n in (1, 2, 4, 8, 16) if n <= batch * min(DIFF_HEADS, MOBA_HEADS))
    ffn_params = [prm for i in range(DEPTH) for prm in ((ffn1_gate, ffn1_up, ffn1_down, i), (ffn2_gate, ffn2_up, ffn2_down, i))]
    w_next = tuple(_cast_bf16(a, layer=ffn_params[0][3]) for a in ffn_params[0][:3])

    def ffn(k, xf, xb, g, b):
        hosted = ffn_params[k + 1] if k + 1 < len(ffn_params) else None
        return _ffn(xf, xb, w_next, 0, g, b, hosted)

    for i in range(DEPTH):
        xf, xb, w_next = ffn(2 * i, xf, xb, vec(ln_g[i, 0]), vec(ln_b[i, 0]))
        pa, pb, pc = _proj(xb, w_in_b, i, tables)
        o_diff, w_out_b, p_b = _diff_attention(pa, pc, diff_lambda[i], vec(diff_norm_g[i]), batch, seq, i,
                                               hosted=[(w_out, i, side_blocks), (p3, i, side_blocks)])
        gg0 = _IN_OFFS[7]
        o_gla = _gla(xb, pc, w_in[i, :, gg0:gg0 + GLA_GATE_RANK], gla_gate_up[i], gla_gate_b[i],
                     vec(gla_norm_g[i]), batch, seq)
        o_moba, w_pg_b = _moba_attention(pb, pc, batch, seq, hosted=[(w_pg, i, side_blocks)])
        xf, xb = _out_proj(o_diff, o_gla, o_moba, w_out_b, 0, xf, vec(ln_g[i, 1]), vec(ln_b[i, 1]))
        xf, xb, w_next = ffn(2 * i + 1, xf, xb, vec(ln_g[i, 2]), vec(ln_b[i, 2]))
        xf, xb = _ple(xf, xb, p_b, w_pe_b, i, w_pg_b, vec(ln_g[i, 3]), vec(ln_b[i, 3]))
    return xf.reshape(batch, seq, d)
```

```python
import functools
import math

import numpy as np
import jax
import jax.numpy as jnp
from jax import lax
from jax.experimental import pallas as pl
from jax.experimental.pallas import tpu as pltpu

F32 = jnp.float32
BF16 = jnp.bfloat16

DEPTH = 2
HEAD_DIM = 128
DIFF_HEADS = 6
GLA_HEADS = 4
MOBA_HEADS = 6
DIFF_QK_DIM = 64
GLA_DK = 64
GLA_DV = 128
GLA_GATE_RANK = 16
GLA_TAU = 16.0
GLA_CHUNK = 64
GLA_UNROLL = 4
MOBA_BLOCK = 256
MOBA_TOPK = 3
ROPE_THETA = 10000.0
LN_EPS = 1e-5
DEEPNORM_ALPHA = (2 * DEPTH) ** 0.25

LANES = 128
SUBLANES = 8
VMEM_LIMIT_BYTES = 56 * 1024 * 1024

PROJ_TN = 768
PROJ_ROW_CHUNK = 256
STREAM_ROW_CHUNK = 128
_CB = LANES
COL_DQ = 0
COL_DK = COL_DQ + DIFF_HEADS
COL_MQ = COL_DK + DIFF_HEADS
COL_MK = COL_MQ + MOBA_HEADS
COL_DV = COL_MK + MOBA_HEADS
COL_MV = COL_DV + DIFF_HEADS
COL_GQ = COL_MV + MOBA_HEADS
COL_GK = COL_GQ + GLA_HEADS * GLA_DK // LANES
COL_GV = COL_GK + GLA_HEADS * GLA_DK // LANES
COL_GR = COL_GV + GLA_HEADS
N_COL_BLOCKS = COL_GR + GLA_HEADS
PROJ_WIDTH = N_COL_BLOCKS * _CB
assert PROJ_WIDTH % PROJ_TN == 0
GG_PAD = LANES

NEG_INF = float("-inf")
LOG2_E = math.log2(math.e)
ATTN_LOOKAHEAD = 2
V_ROWS = HEAD_DIM + 16


def _cparams(sem):
    return pltpu.CompilerParams(dimension_semantics=sem, vmem_limit_bytes=VMEM_LIMIT_BYTES)


def _layer_norm(y, g, b):
    mu = jnp.mean(y, axis=-1, keepdims=True)
    d = y - mu
    var = jnp.mean(d * d, axis=-1, keepdims=True)
    return d * lax.rsqrt(var + LN_EPS) * g + b


def _dot_nt(a, b):
    return lax.dot_general(a, b, (((1,), (1,)), ((), ())), preferred_element_type=F32)


def _dot(a, b):
    return jnp.dot(a, b, preferred_element_type=F32)


def _head_cols(g):
    return slice(g * LANES, (g + 1) * LANES)


def _rope_table_kernel(pos_ref, inv_a_ref, inv_b_ref, sgn_a_ref, sgn_b_ref,
                       cos_a_ref, sin_a_ref, cos_b_ref, sin_b_ref):
    pos = pos_ref[...].astype(F32)
    ang_a = pos * inv_a_ref[...]
    ang_b = pos * inv_b_ref[...]
    cos_a_ref[...] = jnp.cos(ang_a)
    sin_a_ref[...] = jnp.sin(ang_a) * sgn_a_ref[...]
    cos_b_ref[...] = jnp.cos(ang_b)
    sin_b_ref[...] = jnp.sin(ang_b) * sgn_b_ref[...]


def _rope_tables(pos):
    t = pos.shape[0]
    tm = next(c for c in (1024, 512, MOBA_BLOCK) if t % c == 0)

    def pattern(d):
        inv = ROPE_THETA ** (-jnp.arange(0, d, 2, dtype=F32) / d)
        inv = jnp.tile(inv, 2 * LANES // d)[None, :]
        sgn = jnp.tile(jnp.concatenate([-jnp.ones(d // 2, F32), jnp.ones(d // 2, F32)]), LANES // d)[None, :]
        return inv, sgn

    inv_a, sgn_a = pattern(DIFF_QK_DIM)
    inv_b, sgn_b = pattern(HEAD_DIM)
    row = pl.BlockSpec((1, LANES), lambda i: (0, 0))
    out = pl.BlockSpec((tm, LANES), lambda i: (i, 0))
    return pl.pallas_call(
        _rope_table_kernel,
        grid=(t // tm,),
        in_specs=[pl.BlockSpec((tm, 1), lambda i: (i, 0)), row, row, row, row],
        out_specs=[out, out, out, out],
        out_shape=[jax.ShapeDtypeStruct((t, LANES), F32)] * 4,
        compiler_params=_cparams(("parallel",)),
        name="rope_tables",
    )(pos, inv_a, inv_b, sgn_a, sgn_b)


def _ffn_kernel(*refs, nf, has_xb):
    xf_ref = refs[0]
    xb_ref = refs[1] if has_xb else None
    wg_ref, wu_ref, wd_ref, g_ref, b_ref, of_ref, ob_ref, acc_ref = refs[1 + has_xb:]
    j = pl.program_id(1)

    @pl.when(j == 0)
    def _():
        acc_ref[...] = jnp.zeros_like(acc_ref)

    x = xb_ref[...] if has_xb else xf_ref[...].astype(BF16)
    gt = _dot(x, wg_ref[0])
    ut = _dot(x, wu_ref[0])
    h = (gt * jax.nn.sigmoid(gt) * ut).astype(BF16)
    acc_ref[...] += _dot(h, wd_ref[0])

    @pl.when(j == nf - 1)
    def _():
        _emit_stream(of_ref, ob_ref, DEEPNORM_ALPHA * xf_ref[...] + 0.5 * acc_ref[...], g_ref, b_ref)


def _ffn_tiles(t, f):
    tm = 512 if t % 512 == 0 else t
    tf = 512 if f % 512 == 0 else f
    return tm, tf


def _ffn(xf, xb, w, g, b):
    wg, wu, wd = w
    t, d = xf.shape
    f = wg.shape[2]
    tm, tf = _ffn_tiles(t, f)
    nf = f // tf
    row = pl.BlockSpec((tm, d), lambda i, j: (i, 0))
    vec = pl.BlockSpec((1, d), lambda i, j: (0, 0))
    has_xb = xb is not None
    return pl.pallas_call(
        functools.partial(_ffn_kernel, nf=nf, has_xb=has_xb),
        grid=(t // tm, nf),
        in_specs=[row] * (1 + has_xb) + [
            pl.BlockSpec((1, d, tf), lambda i, j: (0, 0, j)),
            pl.BlockSpec((1, d, tf), lambda i, j: (0, 0, j)),
            pl.BlockSpec((1, tf, d), lambda i, j: (0, j, 0)),
            vec, vec],
        out_specs=[row, row],
        out_shape=[jax.ShapeDtypeStruct((t, d), F32), jax.ShapeDtypeStruct((t, d), BF16)],
        scratch_shapes=[pltpu.VMEM((tm, d), F32)],
        compiler_params=_cparams(("parallel", "arbitrary")),
        name="ffn",
    )(*([xf] + ([xb] if has_xb else []) + [wg, wu, wd, g, b]))


def _proj_rope_kernel(xb_ref, w_ref, cos_ref, sin_ref, p_ref, *, rot_dim, q_scale, q_tiles):
    j = pl.program_id(1)
    scale = jnp.where(j < q_tiles, q_scale, 1.0).astype(F32)
    half = rot_dim // 2
    tm = xb_ref.shape[0]
    rc = PROJ_ROW_CHUNK if tm % PROJ_ROW_CHUNK == 0 else tm
    for r0 in range(0, tm, rc):
        rows = slice(r0, r0 + rc)
        acc = _dot_nt(xb_ref[rows, :], w_ref[0])
        cos, sin = cos_ref[rows, :], sin_ref[rows, :]
        for hh in range(PROJ_TN // LANES):
            t = acc[:, _head_cols(hh)]
            if rot_dim == LANES:
                rot = pltpu.roll(t, half, 1)
            else:
                lane = lax.broadcasted_iota(jnp.int32, t.shape, 1)
                rot = jnp.where((lane // half) % 2 == 0, pltpu.roll(t, LANES - half, 1), pltpu.roll(t, half, 1))
            p_ref[rows, _head_cols(hh)] = ((t * cos + rot * sin) * scale).astype(BF16)


def _proj_plain_kernel(xb_ref, w_ref, p_ref, *, gq_cols):
    j = pl.program_id(1)
    acc = _dot_nt(xb_ref[...], w_ref[0])
    col = j * PROJ_TN + lax.broadcasted_iota(jnp.int32, (1, PROJ_TN), 1)
    scale = jnp.where((col >= gq_cols[0]) & (col < gq_cols[1]), GLA_DK ** -0.5, 1.0).astype(F32)
    p_ref[...] = (acc * scale).astype(BF16)


def _proj(xb, w, layer, tables):
    t, d = xb.shape
    tm = 2048 if t % 2048 == 0 else t

    def tiles(c):
        assert (c * LANES) % PROJ_TN == 0
        return c * LANES // PROJ_TN

    cos_a, sin_a, cos_b, sin_b = tables
    x_spec = pl.BlockSpec((tm, d), lambda i, j: (i, 0))
    tab = pl.BlockSpec((tm, LANES), lambda i, j: (i, 0))
    out = pl.BlockSpec((tm, PROJ_TN), lambda i, j: (i, j))
    w_spec = lambda first: pl.BlockSpec((1, PROJ_TN, d), lambda i, j: (layer, first + j, 0))

    def rope_group(first_col, q_heads, n_heads, rot_dim, q_scale, cos, sin, name):
        nj = tiles(n_heads)
        return pl.pallas_call(
            functools.partial(_proj_rope_kernel, rot_dim=rot_dim, q_scale=q_scale, q_tiles=tiles(q_heads)),
            grid=(t // tm, nj),
            in_specs=[x_spec, w_spec(tiles(first_col)), tab, tab],
            out_specs=out,
            out_shape=jax.ShapeDtypeStruct((t, nj * PROJ_TN), BF16),
            compiler_params=_cparams(("parallel", "arbitrary")),
            name=name,
        )(xb, w, cos, sin)

    pa = rope_group(COL_DQ, DIFF_HEADS, 2 * DIFF_HEADS, DIFF_QK_DIM, DIFF_QK_DIM ** -0.5 * LOG2_E, cos_a, sin_a,
                    "in_proj_diff")
    pb = rope_group(COL_MQ, MOBA_HEADS, 2 * MOBA_HEADS, HEAD_DIM, HEAD_DIM ** -0.5 * LOG2_E, cos_b, sin_b,
                    "in_proj_moba")
    nj = tiles(N_COL_BLOCKS - COL_DV)
    pc = pl.pallas_call(
        functools.partial(_proj_plain_kernel, gq_cols=((COL_GQ - COL_DV) * LANES, (COL_GK - COL_DV) * LANES)),
        grid=(t // tm, nj),
        in_specs=[x_spec, w_spec(tiles(COL_DV))],
        out_specs=out,
        out_shape=jax.ShapeDtypeStruct((t, nj * PROJ_TN), BF16),
        compiler_params=_cparams(("parallel", "arbitrary")),
        name="in_proj_plain",
    )(xb, w)
    return pa, pb, pc


_IN_WIDTHS = (DIFF_HEADS * 128, DIFF_HEADS * 128, DIFF_HEADS * 128, GLA_HEADS * GLA_DK, GLA_HEADS * GLA_DK,
              GLA_HEADS * GLA_DV, GLA_HEADS * GLA_DV, GLA_GATE_RANK, MOBA_HEADS * 128, MOBA_HEADS * 128,
              MOBA_HEADS * 128)
_IN_OFFS = tuple(int(v) for v in np.concatenate([[0], np.cumsum(_IN_WIDTHS)]))
D_IN = _IN_OFFS[-1]


def _reorder_w_in_kernel(w_ref, o_ref):
    src = dict(zip(("dq", "dk", "dv", "gq", "gk", "gv", "gr", "gg", "mq", "mk", "mv"), zip(_IN_OFFS[:-1], _IN_WIDTHS)))
    for name, col in (("dq", COL_DQ), ("dk", COL_DK), ("mq", COL_MQ), ("mk", COL_MK), ("dv", COL_DV), ("mv", COL_MV),
                      ("gq", COL_GQ), ("gk", COL_GK), ("gv", COL_GV), ("gr", COL_GR)):
        s0, width = src[name]
        o_ref[0, col * LANES:col * LANES + width, :] = w_ref[0, s0:s0 + width, :].astype(BF16)


def _reorder_w_in(w_in):
    depth, d, d_in = w_in.shape
    assert d_in == D_IN and all(o % 16 == 0 for o in _IN_OFFS)
    wt = jnp.swapaxes(w_in, 1, 2)
    tc = 128 if d % 128 == 0 else d
    return pl.pallas_call(
        _reorder_w_in_kernel,
        grid=(depth, d // tc),
        in_specs=[pl.BlockSpec((1, d_in, tc), lambda l, c: (l, 0, c))],
        out_specs=pl.BlockSpec((1, PROJ_WIDTH, tc), lambda l, c: (l, 0, c)),
        out_shape=jax.ShapeDtypeStruct((depth, PROJ_WIDTH, d), BF16),
        compiler_params=_cparams(("parallel", "parallel")),
        name="reorder_w_in",
    )(wt)


def _cast_kernel(x_ref, o_ref):
    o_ref[...] = x_ref[...].astype(o_ref.dtype)


def _cast_bf16(x, layer=None):
    l, r, c = x.shape
    tr = next((t for t in (1024, 512, 256, 128) if r % t == 0 and t * c * 4 <= (4 << 20)), r)
    first, n = (0, l) if layer is None else (layer, 1)
    return pl.pallas_call(
        _cast_kernel,
        grid=(n, r // tr),
        in_specs=[pl.BlockSpec((1, tr, c), lambda i, j: (first + i, j, 0))],
        out_specs=pl.BlockSpec((1, tr, c), lambda i, j: (i, j, 0)),
        out_shape=jax.ShapeDtypeStruct((n, r, c), BF16),
        compiler_params=_cparams(("parallel", "parallel")),
        name="cast_bf16",
    )(x)


def _store_v_transposed(v_ref, vt_ref, tk):
    seq = v_ref.shape[0]
    for blk in range(seq // tk):
        vt_ref[0:HEAD_DIM, blk * tk:(blk + 1) * tk] = v_ref[blk * tk:(blk + 1) * tk, :].astype(F32).T.astype(BF16)
    vt_ref[HEAD_DIM:V_ROWS, :] = jnp.ones((V_ROWS - HEAD_DIM, seq), BF16)


def _run_tiles(n, scores, finish):
    pending = [scores(t) for t in range(min(ATTN_LOOKAHEAD, n))]
    for t in range(n):
        if t + ATTN_LOOKAHEAD < n:
            pending.append(scores(t + ATTN_LOOKAHEAD))
        finish(t, pending.pop(0))


def _col_max(parts):
    m = jnp.max(parts[0], axis=0, keepdims=True)
    for s in parts[1:]:
        m = jnp.maximum(m, jnp.max(s, axis=0, keepdims=True))
    return m


def _diff_kernel(lam_ref, g_ref, q_ref, k_ref, v_ref, o_ref, vt_ref, *, tq, lam_init):
    seq = k_ref.shape[0]
    _store_v_transposed(v_ref, vt_ref, tq)
    lf = lam_ref[...]
    lam = (jnp.exp(jnp.sum(lf[0:1] * lf[1:2], axis=1, keepdims=True))
           - jnp.exp(jnp.sum(lf[2:3] * lf[3:4], axis=1, keepdims=True)) + lam_init)
    lane = lax.broadcasted_iota(jnp.int32, (tq, LANES), 1)
    causal = (lax.broadcasted_iota(jnp.int32, (tq, 2 * tq), 0)
              <= lax.broadcasted_iota(jnp.int32, (tq, 2 * tq), 1) % tq)
    def scores(qi):
        lo, hi = qi * tq, (qi + 1) * tq
        q = q_ref[lo:hi, :].astype(F32)
        qq = jnp.concatenate([jnp.where(lane < DIFF_QK_DIM, q, 0.0),
                              jnp.where(lane >= DIFF_QK_DIM, q, 0.0)], axis=0).astype(BF16)
        parts = [jnp.where(causal, _dot_nt(k_ref[lo:hi, :], qq), NEG_INF)]
        if qi > 0:
            parts.append(_dot_nt(k_ref[0:lo, :], qq))
        return parts

    def finish(qi, parts):
        lo, hi = qi * tq, (qi + 1) * tq
        m = _col_max(parts)
        acc = _dot(vt_ref[:, lo:hi], jnp.exp2(parts[0] - m).astype(BF16))
        if qi > 0:
            acc = acc + _dot(vt_ref[:, 0:lo], jnp.exp2(parts[1] - m).astype(BF16))
        o_t = acc[0:HEAD_DIM] / acc[HEAD_DIM:HEAD_DIM + 1]
        o = (o_t[:, :tq] - lam * o_t[:, tq:]).T
        o = o * lax.rsqrt(jnp.mean(o * o, axis=-1, keepdims=True) + LN_EPS) * g_ref[...] * (1.0 - lam_init)
        o_ref[lo:hi, :] = o.astype(BF16)

    _run_tiles(seq // tq, scores, finish)


def _with_hosted_casts(kernel_fn, n_in, n_out, n_hosted):
    def wrapped(*refs):
        ins, rest = refs[:n_in], refs[n_in:]
        srcs, rest = rest[:n_hosted], rest[n_hosted:]
        outs, rest = rest[:n_out], rest[n_out:]
        dsts, scratch = rest[:n_hosted], rest[n_hosted:]
        kernel_fn(*ins, *outs, *scratch)
        for src, dst in zip(srcs, dsts):
            dst[...] = src[...].astype(BF16)
    return wrapped


def _hosted_specs(hosted, n_heads):
    ins, outs, shapes = [], [], []
    for arr, layer, nblk in hosted:
        _, r, c = arr.shape
        assert r % nblk == 0
        blk = lambda b, h, nblk=nblk: jnp.minimum(b * n_heads + h, nblk - 1)
        ins.append(pl.BlockSpec((1, r // nblk, c), lambda b, h, layer=layer, blk=blk: (layer, blk(b, h), 0)))
        outs.append(pl.BlockSpec((1, r // nblk, c), lambda b, h, blk=blk: (0, blk(b, h), 0)))
        shapes.append(jax.ShapeDtypeStruct((1, r, c), BF16))
    return ins, outs, shapes, [arr for arr, _, _ in hosted]


def _diff_attention(pa, pc, lam, g, batch, seq, layer, hosted=()):
    tq = 256
    lam_init = 0.8 - 0.6 * math.exp(-0.3 * layer)
    col = lambda c0: pl.BlockSpec((seq, LANES), lambda b, h: (b, c0 + h))
    h_in, h_out, h_shape, h_args = _hosted_specs(hosted, DIFF_HEADS)
    assert batch * DIFF_HEADS >= max([n for _, _, n in hosted], default=0)
    return pl.pallas_call(
        _with_hosted_casts(functools.partial(_diff_kernel, tq=tq, lam_init=lam_init), 5, 1, len(hosted)),
        grid=(batch, DIFF_HEADS),
        in_specs=[pl.BlockSpec((4, DIFF_QK_DIM), lambda b, h: (0, 0)),
                  pl.BlockSpec((1, LANES), lambda b, h: (0, 0)),
                  col(0), col(DIFF_HEADS), col(0)] + h_in,
        out_specs=[pl.BlockSpec((seq, LANES), lambda b, h: (b, h))] + h_out,
        out_shape=[jax.ShapeDtypeStruct((batch * seq, DIFF_HEADS * LANES), BF16)] + h_shape,
        scratch_shapes=[pltpu.VMEM((V_ROWS, seq), BF16)],
        compiler_params=_cparams(("arbitrary", "arbitrary")),
        name="diff_attn",
    )(lam, g, pa, pa, pc, *h_args)


def _moba_kernel(q_ref, k_ref, v_ref, o_ref, vt_ref, *, nb):
    tq = MOBA_BLOCK
    _store_v_transposed(v_ref, vt_ref, tq)
    nbp = -(-nb // SUBLANES) * SUBLANES
    rows = [jnp.mean(k_ref[n * tq:(n + 1) * tq, :].astype(F32), axis=0, keepdims=True) for n in range(nb)]
    if nbp > nb:
        rows.append(jnp.zeros((nbp - nb, LANES), F32))
    km = jnp.concatenate(rows, axis=0)
    km_hi = km.astype(BF16)
    km_lo = (km - km_hi.astype(F32)).astype(BF16)
    causal = (lax.broadcasted_iota(jnp.int32, (tq, tq), 0) <= lax.broadcasted_iota(jnp.int32, (tq, tq), 1))
    def scores(qi):
        lo, hi = qi * tq, (qi + 1) * tq
        q = q_ref[lo:hi, :]
        parts = [jnp.where(causal, _dot_nt(k_ref[lo:hi, :], q), NEG_INF)]
        if qi > 0:
            s_past = _dot_nt(k_ref[0:lo, :], q)
            if qi > MOBA_TOPK:
                gate = _dot_nt(km_hi, q) + _dot_nt(km_lo, q)
                n_idx = lax.broadcasted_iota(jnp.int32, gate.shape, 0)
                rank = jnp.zeros(gate.shape, jnp.int32)
                for mb in range(qi):
                    gm = gate[mb:mb + 1, :]
                    rank = rank + jnp.where((gm > gate) | ((gm == gate) & (mb < n_idx)), 1, 0)
                bias = jnp.where(rank < MOBA_TOPK, 0.0, NEG_INF)
                s_past = jnp.concatenate([s_past[n * tq:(n + 1) * tq] + bias[n:n + 1, :] for n in range(qi)], axis=0)
            parts.append(s_past)
        return parts

    def finish(qi, parts):
        lo, hi = qi * tq, (qi + 1) * tq
        m = _col_max(parts)
        acc = _dot(vt_ref[:, lo:hi], jnp.exp2(parts[0] - m).astype(BF16))
        if qi > 0:
            acc = acc + _dot(vt_ref[:, 0:lo], jnp.exp2(parts[1] - m).astype(BF16))
        o_ref[lo:hi, :] = (acc[0:HEAD_DIM] / acc[HEAD_DIM:HEAD_DIM + 1]).T.astype(BF16)

    _run_tiles(nb, scores, finish)


def _moba_attention(pb, pc, batch, seq, hosted=()):
    assert seq % MOBA_BLOCK == 0
    col = lambda c0: pl.BlockSpec((seq, LANES), lambda b, h: (b, c0 + h))
    h_in, h_out, h_shape, h_args = _hosted_specs(hosted, MOBA_HEADS)
    assert batch * MOBA_HEADS >= max([n for _, _, n in hosted], default=0)
    return pl.pallas_call(
        _with_hosted_casts(functools.partial(_moba_kernel, nb=seq // MOBA_BLOCK), 3, 1, len(hosted)),
        grid=(batch, MOBA_HEADS),
        in_specs=[col(0), col(MOBA_HEADS), col(COL_MV - COL_DV)] + h_in,
        out_specs=[pl.BlockSpec((seq, LANES), lambda b, h: (b, h))] + h_out,
        out_shape=[jax.ShapeDtypeStruct((batch * seq, MOBA_HEADS * LANES), BF16)] + h_shape,
        scratch_shapes=[pltpu.VMEM((V_ROWS, seq), BF16)],
        compiler_params=_cparams(("arbitrary", "arbitrary")),
        name="moba_attn",
    )(pb, pb, pc, *h_args)


def _gla_constants():
    c = GLA_CHUNK
    idx = np.arange(c)
    mats = [np.tril(np.ones((c, c), np.float32))]
    bm = []
    for lvl in range(6):
        m = 32 >> lvl
        parent, half = idx // (2 * m), (idx // m) % 2
        ref = parent * 2 * m + m - 1
        t = idx[None, :]
        second = (half == 1)[:, None] & (t > ref[:, None]) & (t <= idx[:, None])
        first = (half == 0)[:, None] & (t > idx[:, None]) & (t <= ref[:, None])
        mats.append((second | first).astype(np.float32))
        bm.append(((parent[:, None] == parent[None, :]) & (half == 1)[:, None] & (half == 0)[None, :]).astype(np.float32))
    bm.append(np.eye(c, dtype=np.float32))
    cmat = np.concatenate(mats, axis=0)
    return jnp.asarray(cmat, BF16), jnp.asarray(np.concatenate(bm, axis=0), F32)


def _split2(x):
    hi = x.astype(BF16)
    return hi, (x - hi.astype(F32)).astype(BF16)


def _gla_kernel(cmat_ref, bm_ref, wgg_ref, up_ref, bias_ref, gn_ref, x_ref, q_ref, k_ref, v_ref, r_ref, o_ref,
                la_ref, st_ref, *, seq, heads):
    c = GLA_CHUNK
    wg = wgg_ref[...].astype(BF16)
    up_hi, up_lo = _split2(up_ref[...])
    kh = wg.shape[0] // 2
    rows = 512 if seq % 512 == 0 else seq
    for r0 in range(0, seq, rows):
        gg = _dot(x_ref[r0:r0 + rows, 0:kh], wg[0:kh]) + _dot(x_ref[r0:r0 + rows, kh:], wg[kh:])
        g_hi, g_lo = _split2(gg)
        z = _dot(g_hi, up_hi) + _dot(g_hi, up_lo) + _dot(g_lo, up_hi) + bias_ref[...]
        la_ref[r0:r0 + rows, :] = (jnp.minimum(z, 0.0) - jnp.log(1.0 + jnp.exp(-jnp.abs(z)))) * (1.0 / GLA_TAU)

    st_ref[...] = jnp.zeros_like(st_ref)
    cmat = cmat_ref[...]
    lane = lax.broadcasted_iota(jnp.int32, (c, LANES), 1)
    own = [lane < GLA_DK, lane >= GLA_DK]

    def halves(t):
        return [jnp.where(m, t, 0.0).astype(BF16) for m in own]

    def chunk_group(cp, carry):
        sls = [pl.ds(pl.multiple_of((cp * GLA_UNROLL + u) * c, c), c) for u in range(GLA_UNROLL)]
        e_alls = []
        for sl in sls:
            g_hi, g_lo = _split2(la_ref[sl, :])
            e_alls.append(_dot(cmat, g_hi) + _dot(cmat, g_lo))
        ids = [(sl, p, e_all[:, _head_cols(p)]) for sl, e_all in zip(sls, e_alls) for p in range(heads // 2)]
        qs = [q_ref[sl, _head_cols(p)].astype(F32) for sl, p, _ in ids]
        ks = [k_ref[sl, _head_cols(p)].astype(F32) for sl, p, _ in ids]
        atts = []
        for q, k in zip(qs, ks):
            kb = k.astype(BF16)
            atts.append([bm_ref[6 * c:7 * c, :] * _dot_nt(qh, kb) for qh in halves(q)])
        for lvl in range(6):
            for i, (_, _, e) in enumerate(ids):
                w = jnp.exp(e[(1 + lvl) * c:(2 + lvl) * c])
                kw = (ks[i] * w).astype(BF16)
                for hf, qh in enumerate(halves(qs[i] * w)):
                    atts[i][hf] = atts[i][hf] + bm_ref[lvl * c:(lvl + 1) * c, :] * _dot_nt(qh, kw)
        streams = []
        for (sl, p, e), q, k, att in zip(ids, qs, ks, atts):
            cum = e[0:c]
            last = cum[c - 1:c, :]
            qes, kds = halves(q * jnp.exp(cum)), halves(k * jnp.exp(last - cum))
            for hf in range(2):
                h = 2 * p + hf
                v = v_ref[sl, _head_cols(h)]
                o_intra = _dot(att[hf].astype(BF16), v)
                upd = _dot(v.astype(F32).T.astype(BF16), kds[hf])
                streams.append((sl, h, qes[hf], o_intra, upd, jnp.exp(last)))
        for args in streams:
            recur(*args)
        return carry

    def recur(sl, h, qe, o_intra, upd, decay):
        hc = _head_cols(h)
        st = st_ref[h]
        o = _dot_nt(qe, st.astype(BF16)) + o_intra
        st_ref[h] = st * decay + upd
        o = o * lax.rsqrt(jnp.mean(o * o, axis=-1, keepdims=True) + LN_EPS) * gn_ref[...]
        r = r_ref[sl, hc].astype(F32)
        o_ref[sl, hc] = (o * (r * jax.nn.sigmoid(r))).astype(BF16)

    assert (seq // c) % GLA_UNROLL == 0 and heads % 2 == 0
    lax.fori_loop(0, seq // c // GLA_UNROLL, chunk_group, 0)


def _gla(xb, pc, w_gg, up, bias, gn, batch, seq):
    cmat, bm = _gla_constants()
    heads = GLA_HEADS
    d = xb.shape[1]
    qk_w, v_w = heads * GLA_DK, heads * GLA_DV
    wgg_p = jnp.pad(w_gg, ((0, 0), (0, GG_PAD - GLA_GATE_RANK)))
    up_p = jnp.pad(up, ((0, GG_PAD - GLA_GATE_RANK), (0, 0)))
    const = lambda shape: pl.BlockSpec(shape, lambda b: (0, 0))
    col = lambda c0, width: pl.BlockSpec((seq, width), lambda b: (b, (c0 - COL_DV) * LANES // width))
    return pl.pallas_call(
        functools.partial(_gla_kernel, seq=seq, heads=heads),
        grid=(batch,),
        in_specs=[const(cmat.shape), const(bm.shape), const((d, GG_PAD)), const((GG_PAD, qk_w)), const((1, qk_w)),
                  const((1, LANES)), pl.BlockSpec((seq, d), lambda b: (b, 0)),
                  col(COL_GQ, qk_w), col(COL_GK, qk_w), col(COL_GV, v_w), col(COL_GR, v_w)],
        out_specs=pl.BlockSpec((seq, v_w), lambda b: (b, 0)),
        out_shape=jax.ShapeDtypeStruct((batch * seq, v_w), BF16),
        scratch_shapes=[pltpu.VMEM((seq, qk_w), F32), pltpu.VMEM((heads, GLA_DV, LANES), F32)],
        compiler_params=_cparams(("parallel",)),
        name="gla",
    )(cmat, bm, wgg_p, up_p, bias.reshape(1, qk_w), gn, xb, pc, pc, pc, pc)


def _emit_stream(of_ref, ob_ref, y, g_ref, b_ref, rows=slice(None)):
    out = _layer_norm(y, g_ref[...], b_ref[...])
    of_ref[rows, :] = out
    ob_ref[rows, :] = out.astype(BF16)


def _row_chunks(tm):
    rc = STREAM_ROW_CHUNK if tm % STREAM_ROW_CHUNK == 0 else tm
    return [slice(r0, r0 + rc) for r0 in range(0, tm, rc)]


def _out_kernel(od_ref, og_ref, om_ref, w_ref, xf_ref, g_ref, b_ref, of_ref, ob_ref):
    k0 = od_ref.shape[1]
    k1 = k0 + og_ref.shape[1]
    for rows in _row_chunks(xf_ref.shape[0]):
        y = (_dot(od_ref[rows, :], w_ref[0, 0:k0, :]) + _dot(og_ref[rows, :], w_ref[0, k0:k1, :])
             + _dot(om_ref[rows, :], w_ref[0, k1:, :]))
        _emit_stream(of_ref, ob_ref, DEEPNORM_ALPHA * xf_ref[rows, :] + y, g_ref, b_ref, rows)


def _row_tile(t):
    return 512 if t % 512 == 0 else t


def _out_proj(o_diff, o_gla, o_moba, w, layer, xf, g, b):
    t, d = xf.shape
    tm = _row_tile(t)
    row = pl.BlockSpec((tm, d), lambda i: (i, 0))
    vec = pl.BlockSpec((1, d), lambda i: (0, 0))
    part = lambda a: pl.BlockSpec((tm, a.shape[1]), lambda i: (i, 0))
    return pl.pallas_call(
        _out_kernel,
        grid=(t // tm,),
        in_specs=[part(o_diff), part(o_gla), part(o_moba),
                  pl.BlockSpec((1,) + w.shape[1:], lambda i: (layer, 0, 0)),
                  row, vec, vec],
        out_specs=[row, row],
        out_shape=[jax.ShapeDtypeStruct((t, d), F32), jax.ShapeDtypeStruct((t, d), BF16)],
        compiler_params=_cparams(("parallel",)),
        name="out_proj",
    )(o_diff, o_gla, o_moba, w, xf, g, b)


def _ple_kernel(xb_ref, wg_ref, pb_ref, we_ref, xf_ref, g_ref, b_ref, of_ref, ob_ref):
    for rows in _row_chunks(xf_ref.shape[0]):
        e = _dot(pb_ref[0, rows, :], we_ref[0]) * jax.nn.sigmoid(_dot(xb_ref[rows, :], wg_ref[0]))
        _emit_stream(of_ref, ob_ref, DEEPNORM_ALPHA * xf_ref[rows, :] + e, g_ref, b_ref, rows)


def _ple(xf, xb, pb, w_pe, pe_layer, w_pg, g, b):
    t, d = xf.shape
    tm = _row_tile(t)
    row = pl.BlockSpec((tm, d), lambda i: (i, 0))
    vec = pl.BlockSpec((1, d), lambda i: (0, 0))
    whole = lambda a, layer: pl.BlockSpec((1,) + a.shape[1:], lambda i: (layer, 0, 0))
    return pl.pallas_call(
        _ple_kernel,
        grid=(t // tm,),
        in_specs=[row, whole(w_pg, 0),
                  pl.BlockSpec((1, tm, pb.shape[2]), lambda i: (0, i, 0)),
                  whole(w_pe, pe_layer), row, vec, vec],
        out_specs=[row, row],
        out_shape=[jax.ShapeDtypeStruct((t, d), F32), jax.ShapeDtypeStruct((t, d), BF16)],
        compiler_params=_cparams(("parallel",)),
        name="ple",
    )(xb, w_pg, pb, w_pe, xf, g, b)


def kernel(x, p, positions, w_in, w_out, diff_lambda, diff_norm_g, gla_gate_up, gla_gate_b, gla_norm_g,
           ffn1_gate, ffn1_up, ffn1_down, ffn2_gate, ffn2_up, ffn2_down, w_pe, w_pg, ln_g, ln_b):
    batch, seq, d = x.shape
    t = batch * seq
    assert d == (DIFF_HEADS + GLA_HEADS + MOBA_HEADS) * HEAD_DIM and seq % MOBA_BLOCK == 0
    xf, xb = x.reshape(t, d), None
    tables = _rope_tables(positions.reshape(t, 1))
    vec = lambda v: v.reshape(1, -1)
    w_in_b, w_pe_b = _reorder_w_in(w_in), _cast_bf16(w_pe)
    p3 = p.reshape(DEPTH, t, -1)
    side_blocks = max(n for n in (1, 2, 4, 8, 16) if n <= batch * min(DIFF_HEADS, MOBA_HEADS))
    side = lambda arrs, layer: [(a, layer, side_blocks) for a in arrs]
    w_ffn1 = tuple(_cast_bf16(a, layer=0) for a in (ffn1_gate, ffn1_up, ffn1_down))
    for i in range(DEPTH):
        more = i + 1 < DEPTH
        xf, xb = _ffn(xf, xb, w_ffn1, vec(ln_g[i, 0]), vec(ln_b[i, 0]))
        pa, pb, pc = _proj(xb, w_in_b, i, tables)
        o_diff, w_out_b, p_b, *w_side = _diff_attention(
            pa, pc, diff_lambda[i], vec(diff_norm_g[i]), batch, seq, i,
            hosted=side((w_out, p3, ffn2_gate, ffn2_up, ffn2_down), i) + (side((ffn1_down,), i + 1) if more else []))
        w_ffn2, w_down_next = tuple(w_side[:3]), w_side[3:]
        gg0 = _IN_OFFS[7]
        o_gla = _gla(xb, pc, w_in[i, :, gg0:gg0 + GLA_GATE_RANK], gla_gate_up[i], gla_gate_b[i],
                     vec(gla_norm_g[i]), batch, seq)
        o_moba, w_pg_b, *w_gu_next = _moba_attention(
            pb, pc, batch, seq, hosted=side((w_pg,), i) + (side((ffn1_gate, ffn1_up), i + 1) if more else []))
        xf, xb = _out_proj(o_diff, o_gla, o_moba, w_out_b, 0, xf, vec(ln_g[i, 1]), vec(ln_b[i, 1]))
        xf, xb = _ffn(xf, xb, w_ffn2, vec(ln_g[i, 2]), vec(ln_b[i, 2]))
        xf, xb = _ple(xf, xb, p_b, w_pe_b, i, w_pg_b, vec(ln_g[i, 3]), vec(ln_b[i, 3]))
        w_ffn1 = tuple(w_gu_next) + tuple(w_down_next)
    return xf.reshape(batch, seq, d)
```

```python
import functools
import math

import numpy as np
import jax
import jax.numpy as jnp
from jax import lax
from jax.experimental import pallas as pl
from jax.experimental.pallas import tpu as pltpu

F32 = jnp.float32
BF16 = jnp.bfloat16

DEPTH = 2
HEAD_DIM = 128
DIFF_HEADS = 6
GLA_HEADS = 4
MOBA_HEADS = 6
DIFF_QK_DIM = 64
GLA_DK = 64
GLA_DV = 128
GLA_GATE_RANK = 16
GLA_TAU = 16.0
GLA_CHUNK = 64
GLA_UNROLL = 4
MOBA_BLOCK = 256
MOBA_TOPK = 3
ROPE_THETA = 10000.0
LN_EPS = 1e-5
DEEPNORM_ALPHA = (2 * DEPTH) ** 0.25

LANES = 128
SUBLANES = 8
VMEM_LIMIT_BYTES = 56 * 1024 * 1024

PROJ_TN = 768
PROJ_ROW_CHUNK = 256
STREAM_ROW_CHUNK = 128
_CB = LANES
COL_DQ = 0
COL_DK = COL_DQ + DIFF_HEADS
COL_MQ = COL_DK + DIFF_HEADS
COL_MK = COL_MQ + MOBA_HEADS
COL_DV = COL_MK + MOBA_HEADS
COL_MV = COL_DV + DIFF_HEADS
COL_GQ = COL_MV + MOBA_HEADS
COL_GK = COL_GQ + GLA_HEADS * GLA_DK // LANES
COL_GV = COL_GK + GLA_HEADS * GLA_DK // LANES
COL_GR = COL_GV + GLA_HEADS
N_COL_BLOCKS = COL_GR + GLA_HEADS
PROJ_WIDTH = N_COL_BLOCKS * _CB
assert PROJ_WIDTH % PROJ_TN == 0
GG_PAD = LANES

NEG_INF = float("-inf")
LOG2_E = math.log2(math.e)
ATTN_LOOKAHEAD = 2
V_ROWS = HEAD_DIM + 16


def _cparams(sem):
    return pltpu.CompilerParams(dimension_semantics=sem, vmem_limit_bytes=VMEM_LIMIT_BYTES)


def _layer_norm(y, g, b):
    mu = jnp.mean(y, axis=-1, keepdims=True)
    d = y - mu
    var = jnp.mean(d * d, axis=-1, keepdims=True)
    return d * lax.rsqrt(var + LN_EPS) * g + b


def _dot_nt(a, b):
    return lax.dot_general(a, b, (((1,), (1,)), ((), ())), preferred_element_type=F32)


def _dot(a, b):
    return jnp.dot(a, b, preferred_element_type=F32)


def _head_cols(g):
    return slice(g * LANES, (g + 1) * LANES)


def _rope_table_kernel(pos_ref, inv_a_ref, inv_b_ref, sgn_a_ref, sgn_b_ref,
                       cos_a_ref, sin_a_ref, cos_b_ref, sin_b_ref):
    pos = pos_ref[...].astype(F32)
    ang_a = pos * inv_a_ref[...]
    ang_b = pos * inv_b_ref[...]
    cos_a_ref[...] = jnp.cos(ang_a)
    sin_a_ref[...] = jnp.sin(ang_a) * sgn_a_ref[...]
    cos_b_ref[...] = jnp.cos(ang_b)
    sin_b_ref[...] = jnp.sin(ang_b) * sgn_b_ref[...]


def _rope_tables(pos, hosted=(), steps=None):
    t = pos.shape[0]
    tm = next(c for c in (1024, 512, MOBA_BLOCK) if t % c == 0 and (steps is None or t // c >= steps))

    def pattern(d):
        inv = ROPE_THETA ** (-jnp.arange(0, d, 2, dtype=F32) / d)
        inv = jnp.tile(inv, 2 * LANES // d)[None, :]
        sgn = jnp.tile(jnp.concatenate([-jnp.ones(d // 2, F32), jnp.ones(d // 2, F32)]), LANES // d)[None, :]
        return inv, sgn

    inv_a, sgn_a = pattern(DIFF_QK_DIM)
    inv_b, sgn_b = pattern(HEAD_DIM)
    row = pl.BlockSpec((1, LANES), lambda i: (0, 0))
    out = pl.BlockSpec((tm, LANES), lambda i: (i, 0))
    h_in, h_out, h_shape, h_args = _hosted_specs(hosted)
    assert t // tm >= max([n for _, _, n in hosted], default=0)
    return pl.pallas_call(
        _with_hosted_casts(_rope_table_kernel, 5, 4, len(hosted)),
        grid=(t // tm,),
        in_specs=[pl.BlockSpec((tm, 1), lambda i: (i, 0)), row, row, row, row] + h_in,
        out_specs=[out, out, out, out] + h_out,
        out_shape=[jax.ShapeDtypeStruct((t, LANES), F32)] * 4 + h_shape,
        compiler_params=_cparams(("arbitrary",)),
        name="rope_tables",
    )(pos, inv_a, inv_b, sgn_a, sgn_b, *h_args)


def _ffn_kernel(*refs, nf, has_xb):
    xf_ref = refs[0]
    xb_ref = refs[1] if has_xb else None
    wg_ref, wu_ref, wd_ref, g_ref, b_ref, of_ref, ob_ref, acc_ref = refs[1 + has_xb:]
    j = pl.program_id(1)

    @pl.when(j == 0)
    def _():
        acc_ref[...] = jnp.zeros_like(acc_ref)

    x = xb_ref[...] if has_xb else xf_ref[...].astype(BF16)
    gt = _dot(x, wg_ref[0])
    ut = _dot(x, wu_ref[0])
    h = (gt * jax.nn.sigmoid(gt) * ut).astype(BF16)
    acc_ref[...] += _dot(h, wd_ref[0])

    @pl.when(j == nf - 1)
    def _():
        _emit_stream(of_ref, ob_ref, DEEPNORM_ALPHA * xf_ref[...] + 0.5 * acc_ref[...], g_ref, b_ref)


def _ffn_tiles(t, f):
    tm = 512 if t % 512 == 0 else t
    tf = 512 if f % 512 == 0 else f
    return tm, tf


def _ffn(xf, xb, w, g, b):
    wg, wu, wd = w
    t, d = xf.shape
    f = wg.shape[2]
    tm, tf = _ffn_tiles(t, f)
    nf = f // tf
    row = pl.BlockSpec((tm, d), lambda i, j: (i, 0))
    vec = pl.BlockSpec((1, d), lambda i, j: (0, 0))
    has_xb = xb is not None
    return pl.pallas_call(
        functools.partial(_ffn_kernel, nf=nf, has_xb=has_xb),
        grid=(t // tm, nf),
        in_specs=[row] * (1 + has_xb) + [
            pl.BlockSpec((1, d, tf), lambda i, j: (0, 0, j)),
            pl.BlockSpec((1, d, tf), lambda i, j: (0, 0, j)),
            pl.BlockSpec((1, tf, d), lambda i, j: (0, j, 0)),
            vec, vec],
        out_specs=[row, row],
        out_shape=[jax.ShapeDtypeStruct((t, d), F32), jax.ShapeDtypeStruct((t, d), BF16)],
        scratch_shapes=[pltpu.VMEM((tm, d), F32)],
        compiler_params=_cparams(("parallel", "arbitrary")),
        name="ffn",
    )(*([xf] + ([xb] if has_xb else []) + [wg, wu, wd, g, b]))


def _proj_rope_kernel(xb_ref, w_ref, cos_ref, sin_ref, p_ref, *, rot_dim, q_scale, q_tiles):
    j = pl.program_id(1)
    scale = jnp.where(j < q_tiles, q_scale, 1.0).astype(F32)
    half = rot_dim // 2
    tm = xb_ref.shape[0]
    rc = PROJ_ROW_CHUNK if tm % PROJ_ROW_CHUNK == 0 else tm
    for r0 in range(0, tm, rc):
        rows = slice(r0, r0 + rc)
        acc = _dot_nt(xb_ref[rows, :], w_ref[0])
        cos, sin = cos_ref[rows, :], sin_ref[rows, :]
        for hh in range(PROJ_TN // LANES):
            t = acc[:, _head_cols(hh)]
            if rot_dim == LANES:
                rot = pltpu.roll(t, half, 1)
            else:
                lane = lax.broadcasted_iota(jnp.int32, t.shape, 1)
                rot = jnp.where((lane // half) % 2 == 0, pltpu.roll(t, LANES - half, 1), pltpu.roll(t, half, 1))
            p_ref[rows, _head_cols(hh)] = ((t * cos + rot * sin) * scale).astype(BF16)


def _proj_plain_kernel(xb_ref, w_ref, p_ref, *, gq_cols):
    j = pl.program_id(1)
    col = j * PROJ_TN + lax.broadcasted_iota(jnp.int32, (1, PROJ_TN), 1)
    scale = jnp.where((col >= gq_cols[0]) & (col < gq_cols[1]), GLA_DK ** -0.5, 1.0).astype(F32)
    tm = xb_ref.shape[0]
    rc = PROJ_ROW_CHUNK if tm % PROJ_ROW_CHUNK == 0 else tm
    for r0 in range(0, tm, rc):
        rows = slice(r0, r0 + rc)
        p_ref[rows, :] = (_dot_nt(xb_ref[rows, :], w_ref[0]) * scale).astype(BF16)


def _proj_row_tile(t):
    return 2048 if t % 2048 == 0 else t


def _proj_plain_steps(t):
    return (t // _proj_row_tile(t)) * ((N_COL_BLOCKS - COL_DV) * LANES // PROJ_TN)


def _proj(xb, w, layer, tables, hosted=()):
    t, d = xb.shape
    tm = _proj_row_tile(t)

    def tiles(c):
        assert (c * LANES) % PROJ_TN == 0
        return c * LANES // PROJ_TN

    cos_a, sin_a, cos_b, sin_b = tables
    x_spec = pl.BlockSpec((tm, d), lambda i, j: (i, 0))
    tab = pl.BlockSpec((tm, LANES), lambda i, j: (i, 0))
    out = pl.BlockSpec((tm, PROJ_TN), lambda i, j: (i, j))
    w_spec = lambda first: pl.BlockSpec((1, PROJ_TN, d), lambda i, j: (layer, first + j, 0))

    def rope_group(first_col, q_heads, n_heads, rot_dim, q_scale, cos, sin, name):
        nj = tiles(n_heads)
        return pl.pallas_call(
            functools.partial(_proj_rope_kernel, rot_dim=rot_dim, q_scale=q_scale, q_tiles=tiles(q_heads)),
            grid=(t // tm, nj),
            in_specs=[x_spec, w_spec(tiles(first_col)), tab, tab],
            out_specs=out,
            out_shape=jax.ShapeDtypeStruct((t, nj * PROJ_TN), BF16),
            compiler_params=_cparams(("parallel", "arbitrary")),
            name=name,
        )(xb, w, cos, sin)

    pa = rope_group(COL_DQ, DIFF_HEADS, 2 * DIFF_HEADS, DIFF_QK_DIM, DIFF_QK_DIM ** -0.5 * LOG2_E, cos_a, sin_a,
                    "in_proj_diff")
    pb = rope_group(COL_MQ, MOBA_HEADS, 2 * MOBA_HEADS, HEAD_DIM, HEAD_DIM ** -0.5 * LOG2_E, cos_b, sin_b,
                    "in_proj_moba")
    nj = tiles(N_COL_BLOCKS - COL_DV)
    h_in, h_out, h_shape, h_args = _hosted_specs(hosted, nj)
    assert (t // tm) * nj >= max([n for _, _, n in hosted], default=0)
    pc, *side = pl.pallas_call(
        _with_hosted_casts(functools.partial(_proj_plain_kernel, gq_cols=((COL_GQ - COL_DV) * LANES,
                                                                           (COL_GK - COL_DV) * LANES)),
                           2, 1, len(hosted)),
        grid=(t // tm, nj),
        in_specs=[x_spec, w_spec(tiles(COL_DV))] + h_in,
        out_specs=[out] + h_out,
        out_shape=[jax.ShapeDtypeStruct((t, nj * PROJ_TN), BF16)] + h_shape,
        compiler_params=_cparams(("arbitrary", "arbitrary")),
        name="in_proj_plain",
    )(xb, w, *h_args)
    return pa, pb, pc, side


_IN_WIDTHS = (DIFF_HEADS * 128, DIFF_HEADS * 128, DIFF_HEADS * 128, GLA_HEADS * GLA_DK, GLA_HEADS * GLA_DK,
              GLA_HEADS * GLA_DV, GLA_HEADS * GLA_DV, GLA_GATE_RANK, MOBA_HEADS * 128, MOBA_HEADS * 128,
              MOBA_HEADS * 128)
_IN_OFFS = tuple(int(v) for v in np.concatenate([[0], np.cumsum(_IN_WIDTHS)]))
D_IN = _IN_OFFS[-1]


def _reorder_w_in_kernel(w_ref, o_ref):
    src = dict(zip(("dq", "dk", "dv", "gq", "gk", "gv", "gr", "gg", "mq", "mk", "mv"), zip(_IN_OFFS[:-1], _IN_WIDTHS)))
    for name, col in (("dq", COL_DQ), ("dk", COL_DK), ("mq", COL_MQ), ("mk", COL_MK), ("dv", COL_DV), ("mv", COL_MV),
                      ("gq", COL_GQ), ("gk", COL_GK), ("gv", COL_GV), ("gr", COL_GR)):
        s0, width = src[name]
        o_ref[0, col * LANES:col * LANES + width, :] = w_ref[0, s0:s0 + width, :].astype(BF16)


def _reorder_w_in(w_in):
    depth, d, d_in = w_in.shape
    assert d_in == D_IN and all(o % 16 == 0 for o in _IN_OFFS)
    wt = jnp.swapaxes(w_in, 1, 2)
    tc = 128 if d % 128 == 0 else d
    return pl.pallas_call(
        _reorder_w_in_kernel,
        grid=(depth, d // tc),
        in_specs=[pl.BlockSpec((1, d_in, tc), lambda l, c: (l, 0, c))],
        out_specs=pl.BlockSpec((1, PROJ_WIDTH, tc), lambda l, c: (l, 0, c)),
        out_shape=jax.ShapeDtypeStruct((depth, PROJ_WIDTH, d), BF16),
        compiler_params=_cparams(("parallel", "parallel")),
        name="reorder_w_in",
    )(wt)


def _cast_kernel(x_ref, o_ref):
    o_ref[...] = x_ref[...].astype(o_ref.dtype)


def _cast_bf16(x, layer=None):
    l, r, c = x.shape
    tr = next((t for t in (1024, 512, 256, 128) if r % t == 0 and t * c * 4 <= (4 << 20)), r)
    first, n = (0, l) if layer is None else (layer, 1)
    return pl.pallas_call(
        _cast_kernel,
        grid=(n, r // tr),
        in_specs=[pl.BlockSpec((1, tr, c), lambda i, j: (first + i, j, 0))],
        out_specs=pl.BlockSpec((1, tr, c), lambda i, j: (i, j, 0)),
        out_shape=jax.ShapeDtypeStruct((n, r, c), BF16),
        compiler_params=_cparams(("parallel", "parallel")),
        name="cast_bf16",
    )(x)


def _store_v_transposed(v_ref, vt_ref, tk):
    seq = v_ref.shape[0]
    for blk in range(seq // tk):
        vt_ref[0:HEAD_DIM, blk * tk:(blk + 1) * tk] = v_ref[blk * tk:(blk + 1) * tk, :].astype(F32).T.astype(BF16)
    vt_ref[HEAD_DIM:V_ROWS, :] = jnp.ones((V_ROWS - HEAD_DIM, seq), BF16)


def _run_tiles(n, scores, finish):
    pending = [scores(t) for t in range(min(ATTN_LOOKAHEAD, n))]
    for t in range(n):
        if t + ATTN_LOOKAHEAD < n:
            pending.append(scores(t + ATTN_LOOKAHEAD))
        finish(t, pending.pop(0))


def _col_max(parts):
    m = jnp.max(parts[0], axis=0, keepdims=True)
    for s in parts[1:]:
        m = jnp.maximum(m, jnp.max(s, axis=0, keepdims=True))
    return m


def _diff_kernel(lam_ref, g_ref, q_ref, k_ref, v_ref, o_ref, vt_ref, *, tq, lam_init):
    seq = k_ref.shape[0]
    _store_v_transposed(v_ref, vt_ref, tq)
    lf = lam_ref[...]
    lam = (jnp.exp(jnp.sum(lf[0:1] * lf[1:2], axis=1, keepdims=True))
           - jnp.exp(jnp.sum(lf[2:3] * lf[3:4], axis=1, keepdims=True)) + lam_init)
    lane = lax.broadcasted_iota(jnp.int32, (tq, LANES), 1)
    causal = (lax.broadcasted_iota(jnp.int32, (tq, 2 * tq), 0)
              <= lax.broadcasted_iota(jnp.int32, (tq, 2 * tq), 1) % tq)
    def scores(qi):
        lo, hi = qi * tq, (qi + 1) * tq
        q = q_ref[lo:hi, :].astype(F32)
        qq = jnp.concatenate([jnp.where(lane < DIFF_QK_DIM, q, 0.0),
                              jnp.where(lane >= DIFF_QK_DIM, q, 0.0)], axis=0).astype(BF16)
        parts = [jnp.where(causal, _dot_nt(k_ref[lo:hi, :], qq), NEG_INF)]
        if qi > 0:
            parts.append(_dot_nt(k_ref[0:lo, :], qq))
        return parts

    def finish(qi, parts):
        lo, hi = qi * tq, (qi + 1) * tq
        m = _col_max(parts)
        acc = _dot(vt_ref[:, lo:hi], jnp.exp2(parts[0] - m).astype(BF16))
        if qi > 0:
            acc = acc + _dot(vt_ref[:, 0:lo], jnp.exp2(parts[1] - m).astype(BF16))
        o_t = acc[0:HEAD_DIM] / acc[HEAD_DIM:HEAD_DIM + 1]
        o = (o_t[:, :tq] - lam * o_t[:, tq:]).T
        o = o * lax.rsqrt(jnp.mean(o * o, axis=-1, keepdims=True) + LN_EPS) * g_ref[...] * (1.0 - lam_init)
        o_ref[lo:hi, :] = o.astype(BF16)

    _run_tiles(seq // tq, scores, finish)


def _with_hosted_casts(kernel_fn, n_in, n_out, n_hosted):
    def wrapped(*refs):
        ins, rest = refs[:n_in], refs[n_in:]
        srcs, rest = rest[:n_hosted], rest[n_hosted:]
        outs, rest = rest[:n_out], rest[n_out:]
        dsts, scratch = rest[:n_hosted], rest[n_hosted:]
        for src, dst in zip(srcs, dsts):
            dst[...] = src[...].astype(BF16)
        kernel_fn(*ins, *outs, *scratch)
    return wrapped


def _hosted_specs(hosted, n_inner=None):
    step = (lambda i: i) if n_inner is None else (lambda b, h: b * n_inner + h)
    ins, outs, shapes = [], [], []
    for arr, layer, nblk in hosted:
        _, r, c = arr.shape
        assert r % nblk == 0
        blk = lambda *g, nblk=nblk: jnp.minimum(step(*g), nblk - 1)
        ins.append(pl.BlockSpec((1, r // nblk, c), lambda *g, layer=layer, blk=blk: (layer, blk(*g), 0)))
        outs.append(pl.BlockSpec((1, r // nblk, c), lambda *g, blk=blk: (0, blk(*g), 0)))
        shapes.append(jax.ShapeDtypeStruct((1, r, c), BF16))
    return ins, outs, shapes, [arr for arr, _, _ in hosted]


def _diff_attention(pa, pc, lam, g, batch, seq, layer, hosted=()):
    tq = 256
    lam_init = 0.8 - 0.6 * math.exp(-0.3 * layer)
    col = lambda c0: pl.BlockSpec((seq, LANES), lambda b, h: (b, c0 + h))
    h_in, h_out, h_shape, h_args = _hosted_specs(hosted, DIFF_HEADS)
    assert batch * DIFF_HEADS >= max([n for _, _, n in hosted], default=0)
    return pl.pallas_call(
        _with_hosted_casts(functools.partial(_diff_kernel, tq=tq, lam_init=lam_init), 5, 1, len(hosted)),
        grid=(batch, DIFF_HEADS),
        in_specs=[pl.BlockSpec((4, DIFF_QK_DIM), lambda b, h: (0, 0)),
                  pl.BlockSpec((1, LANES), lambda b, h: (0, 0)),
                  col(0), col(DIFF_HEADS), col(0)] + h_in,
        out_specs=[pl.BlockSpec((seq, LANES), lambda b, h: (b, h))] + h_out,
        out_shape=[jax.ShapeDtypeStruct((batch * seq, DIFF_HEADS * LANES), BF16)] + h_shape,
        scratch_shapes=[pltpu.VMEM((V_ROWS, seq), BF16)],
        compiler_params=_cparams(("arbitrary", "arbitrary")),
        name="diff_attn",
    )(lam, g, pa, pa, pc, *h_args)


def _moba_kernel(q_ref, k_ref, v_ref, o_ref, vt_ref, *, nb):
    tq = MOBA_BLOCK
    _store_v_transposed(v_ref, vt_ref, tq)
    nbp = -(-nb // SUBLANES) * SUBLANES
    rows = [jnp.mean(k_ref[n * tq:(n + 1) * tq, :].astype(F32), axis=0, keepdims=True) for n in range(nb)]
    if nbp > nb:
        rows.append(jnp.zeros((nbp - nb, LANES), F32))
    km = jnp.concatenate(rows, axis=0)
    km_hi = km.astype(BF16)
    km_lo = (km - km_hi.astype(F32)).astype(BF16)
    causal = (lax.broadcasted_iota(jnp.int32, (tq, tq), 0) <= lax.broadcasted_iota(jnp.int32, (tq, tq), 1))
    def scores(qi):
        lo, hi = qi * tq, (qi + 1) * tq
        q = q_ref[lo:hi, :]
        parts = [jnp.where(causal, _dot_nt(k_ref[lo:hi, :], q), NEG_INF)]
        if qi > 0:
            s_past = _dot_nt(k_ref[0:lo, :], q)
            if qi > MOBA_TOPK:
                gate = _dot_nt(km_hi, q) + _dot_nt(km_lo, q)
                n_idx = lax.broadcasted_iota(jnp.int32, gate.shape, 0)
                rank = jnp.zeros(gate.shape, jnp.int32)
                for mb in range(qi):
                    gm = gate[mb:mb + 1, :]
                    rank = rank + jnp.where((gm > gate) | ((gm == gate) & (mb < n_idx)), 1, 0)
                bias = jnp.where(rank < MOBA_TOPK, 0.0, NEG_INF)
                s_past = jnp.concatenate([s_past[n * tq:(n + 1) * tq] + bias[n:n + 1, :] for n in range(qi)], axis=0)
            parts.append(s_past)
        return parts

    def finish(qi, parts):
        lo, hi = qi * tq, (qi + 1) * tq
        m = _col_max(parts)
        acc = _dot(vt_ref[:, lo:hi], jnp.exp2(parts[0] - m).astype(BF16))
        if qi > 0:
            acc = acc + _dot(vt_ref[:, 0:lo], jnp.exp2(parts[1] - m).astype(BF16))
        o_ref[lo:hi, :] = (acc[0:HEAD_DIM] / acc[HEAD_DIM:HEAD_DIM + 1]).T.astype(BF16)

    _run_tiles(nb, scores, finish)


def _moba_attention(pb, pc, batch, seq, hosted=()):
    assert seq % MOBA_BLOCK == 0
    col = lambda c0: pl.BlockSpec((seq, LANES), lambda b, h: (b, c0 + h))
    h_in, h_out, h_shape, h_args = _hosted_specs(hosted, MOBA_HEADS)
    assert batch * MOBA_HEADS >= max([n for _, _, n in hosted], default=0)
    return pl.pallas_call(
        _with_hosted_casts(functools.partial(_moba_kernel, nb=seq // MOBA_BLOCK), 3, 1, len(hosted)),
        grid=(batch, MOBA_HEADS),
        in_specs=[col(0), col(MOBA_HEADS), col(COL_MV - COL_DV)] + h_in,
        out_specs=[pl.BlockSpec((seq, LANES), lambda b, h: (b, h))] + h_out,
        out_shape=[jax.ShapeDtypeStruct((batch * seq, MOBA_HEADS * LANES), BF16)] + h_shape,
        scratch_shapes=[pltpu.VMEM((V_ROWS, seq), BF16)],
        compiler_params=_cparams(("arbitrary", "arbitrary")),
        name="moba_attn",
    )(pb, pb, pc, *h_args)


def _gla_constants():
    c = GLA_CHUNK
    idx = np.arange(c)
    mats = [np.tril(np.ones((c, c), np.float32))]
    bm = []
    for lvl in range(6):
        m = 32 >> lvl
        parent, half = idx // (2 * m), (idx // m) % 2
        ref = parent * 2 * m + m - 1
        t = idx[None, :]
        second = (half == 1)[:, None] & (t > ref[:, None]) & (t <= idx[:, None])
        first = (half == 0)[:, None] & (t > idx[:, None]) & (t <= ref[:, None])
        mats.append((second | first).astype(np.float32))
        bm.append(((parent[:, None] == parent[None, :]) & (half == 1)[:, None] & (half == 0)[None, :]).astype(np.float32))
    bm.append(np.eye(c, dtype=np.float32))
    cmat = np.concatenate(mats, axis=0)
    return jnp.asarray(cmat, BF16), jnp.asarray(np.concatenate(bm, axis=0), F32)


def _split2(x):
    hi = x.astype(BF16)
    return hi, (x - hi.astype(F32)).astype(BF16)


def _gla_kernel(cmat_ref, bm_ref, wgg_ref, up_ref, bias_ref, gn_ref, x_ref, q_ref, k_ref, v_ref, r_ref, o_ref,
                la_ref, st_ref, *, seq, heads):
    c = GLA_CHUNK
    wg = wgg_ref[...].astype(BF16)
    up_hi, up_lo = _split2(up_ref[...])
    rows = 512 if seq % 512 == 0 else seq
    for r0 in range(0, seq, rows):
        g_hi, g_lo = _split2(_dot(x_ref[r0:r0 + rows, :], wg))
        z = _dot(g_hi, up_hi) + _dot(g_hi, up_lo) + _dot(g_lo, up_hi) + bias_ref[...]
        la_ref[r0:r0 + rows, :] = (jnp.minimum(z, 0.0) - jnp.log(1.0 + jnp.exp(-jnp.abs(z)))) * (1.0 / GLA_TAU)

    st_ref[...] = jnp.zeros_like(st_ref)
    cmat = cmat_ref[...]
    lane = lax.broadcasted_iota(jnp.int32, (c, LANES), 1)
    own = [lane < GLA_DK, lane >= GLA_DK]

    def halves(t):
        return [jnp.where(m, t, 0.0).astype(BF16) for m in own]

    def chunk_group(cp, carry):
        sls = [pl.ds(pl.multiple_of((cp * GLA_UNROLL + u) * c, c), c) for u in range(GLA_UNROLL)]
        e_alls = []
        for sl in sls:
            g_hi, g_lo = _split2(la_ref[sl, :])
            e_alls.append(_dot(cmat, g_hi) + _dot(cmat, g_lo))
        ids = [(sl, p, e_all[:, _head_cols(p)]) for sl, e_all in zip(sls, e_alls) for p in range(heads // 2)]
        qs = [q_ref[sl, _head_cols(p)].astype(F32) for sl, p, _ in ids]
        ks = [k_ref[sl, _head_cols(p)].astype(F32) for sl, p, _ in ids]
        atts = []
        for q, k in zip(qs, ks):
            kb = k.astype(BF16)
            atts.append([bm_ref[6 * c:7 * c, :] * _dot_nt(qh, kb) for qh in halves(q)])
        for lvl in range(6):
            for i, (_, _, e) in enumerate(ids):
                w = jnp.exp(e[(1 + lvl) * c:(2 + lvl) * c])
                kw = (ks[i] * w).astype(BF16)
                for hf, qh in enumerate(halves(qs[i] * w)):
                    atts[i][hf] = atts[i][hf] + bm_ref[lvl * c:(lvl + 1) * c, :] * _dot_nt(qh, kw)
        streams = []
        for (sl, p, e), q, k, att in zip(ids, qs, ks, atts):
            cum = e[0:c]
            last = cum[c - 1:c, :]
            qes, kds = halves(q * jnp.exp(cum)), halves(k * jnp.exp(last - cum))
            for hf in range(2):
                h = 2 * p + hf
                v = v_ref[sl, _head_cols(h)]
                o_intra = _dot(att[hf].astype(BF16), v)
                upd = _dot(v.astype(F32).T.astype(BF16), kds[hf])
                streams.append((sl, h, qes[hf], o_intra, upd, jnp.exp(last)))
        for args in streams:
            recur(*args)
        return carry

    def recur(sl, h, qe, o_intra, upd, decay):
        hc = _head_cols(h)
        st = st_ref[h]
        o = _dot_nt(qe, st.astype(BF16)) + o_intra
        st_ref[h] = st * decay + upd
        o = o * lax.rsqrt(jnp.mean(o * o, axis=-1, keepdims=True) + LN_EPS) * gn_ref[...]
        r = r_ref[sl, hc].astype(F32)
        o_ref[sl, hc] = (o * (r * jax.nn.sigmoid(r))).astype(BF16)

    assert (seq // c) % GLA_UNROLL == 0 and heads % 2 == 0
    lax.fori_loop(0, seq // c // GLA_UNROLL, chunk_group, 0)


def _gla(xb, pc, w_gg, up, bias, gn, batch, seq):
    cmat, bm = _gla_constants()
    heads = GLA_HEADS
    d = xb.shape[1]
    qk_w, v_w = heads * GLA_DK, heads * GLA_DV
    wgg_p = jnp.pad(w_gg, ((0, 0), (0, GG_PAD - GLA_GATE_RANK)))
    up_p = jnp.pad(up, ((0, GG_PAD - GLA_GATE_RANK), (0, 0)))
    const = lambda shape: pl.BlockSpec(shape, lambda b: (0, 0))
    col = lambda c0, width: pl.BlockSpec((seq, width), lambda b: (b, (c0 - COL_DV) * LANES // width))
    return pl.pallas_call(
        functools.partial(_gla_kernel, seq=seq, heads=heads),
        grid=(batch,),
        in_specs=[const(cmat.shape), const(bm.shape), const((d, GG_PAD)), const((GG_PAD, qk_w)), const((1, qk_w)),
                  const((1, LANES)), pl.BlockSpec((seq, d), lambda b: (b, 0)),
                  col(COL_GQ, qk_w), col(COL_GK, qk_w), col(COL_GV, v_w), col(COL_GR, v_w)],
        out_specs=pl.BlockSpec((seq, v_w), lambda b: (b, 0)),
        out_shape=jax.ShapeDtypeStruct((batch * seq, v_w), BF16),
        scratch_shapes=[pltpu.VMEM((seq, qk_w), F32), pltpu.VMEM((heads, GLA_DV, LANES), F32)],
        compiler_params=_cparams(("parallel",)),
        name="gla",
    )(cmat, bm, wgg_p, up_p, bias.reshape(1, qk_w), gn, xb, pc, pc, pc, pc)


def _emit_stream(of_ref, ob_ref, y, g_ref, b_ref, rows=slice(None)):
    out = _layer_norm(y, g_ref[...], b_ref[...])
    of_ref[rows, :] = out
    ob_ref[rows, :] = out.astype(BF16)


def _row_chunks(tm):
    rc = STREAM_ROW_CHUNK if tm % STREAM_ROW_CHUNK == 0 else tm
    return [slice(r0, r0 + rc) for r0 in range(0, tm, rc)]


def _out_kernel(od_ref, og_ref, om_ref, w_ref, xf_ref, g_ref, b_ref, of_ref, ob_ref):
    k0 = od_ref.shape[1]
    k1 = k0 + og_ref.shape[1]
    for rows in _row_chunks(xf_ref.shape[0]):
        y = (_dot(od_ref[rows, :], w_ref[0, 0:k0, :]) + _dot(og_ref[rows, :], w_ref[0, k0:k1, :])
             + _dot(om_ref[rows, :], w_ref[0, k1:, :]))
        _emit_stream(of_ref, ob_ref, DEEPNORM_ALPHA * xf_ref[rows, :] + y, g_ref, b_ref, rows)


def _row_tile(t):
    return 512 if t % 512 == 0 else t


def _out_proj(o_diff, o_gla, o_moba, w, layer, xf, g, b):
    t, d = xf.shape
    tm = _row_tile(t)
    row = pl.BlockSpec((tm, d), lambda i: (i, 0))
    vec = pl.BlockSpec((1, d), lambda i: (0, 0))
    part = lambda a: pl.BlockSpec((tm, a.shape[1]), lambda i: (i, 0))
    return pl.pallas_call(
        _out_kernel,
        grid=(t // tm,),
        in_specs=[part(o_diff), part(o_gla), part(o_moba),
                  pl.BlockSpec((1,) + w.shape[1:], lambda i: (layer, 0, 0)),
                  row, vec, vec],
        out_specs=[row, row],
        out_shape=[jax.ShapeDtypeStruct((t, d), F32), jax.ShapeDtypeStruct((t, d), BF16)],
        compiler_params=_cparams(("parallel",)),
        name="out_proj",
    )(o_diff, o_gla, o_moba, w, xf, g, b)


def _ple_kernel(xb_ref, wg_ref, pb_ref, we_ref, xf_ref, g_ref, b_ref, of_ref, ob_ref):
    for rows in _row_chunks(xf_ref.shape[0]):
        e = _dot(pb_ref[0, rows, :], we_ref[0]) * jax.nn.sigmoid(_dot(xb_ref[rows, :], wg_ref[0]))
        _emit_stream(of_ref, ob_ref, DEEPNORM_ALPHA * xf_ref[rows, :] + e, g_ref, b_ref, rows)


def _ple(xf, xb, pb, w_pe, pe_layer, w_pg, g, b):
    t, d = xf.shape
    tm = _row_tile(t)
    row = pl.BlockSpec((tm, d), lambda i: (i, 0))
    vec = pl.BlockSpec((1, d), lambda i: (0, 0))
    whole = lambda a, layer: pl.BlockSpec((1,) + a.shape[1:], lambda i: (layer, 0, 0))
    return pl.pallas_call(
        _ple_kernel,
        grid=(t // tm,),
        in_specs=[row, whole(w_pg, 0),
                  pl.BlockSpec((1, tm, pb.shape[2]), lambda i: (0, i, 0)),
                  whole(w_pe, pe_layer), row, vec, vec],
        out_specs=[row, row],
        out_shape=[jax.ShapeDtypeStruct((t, d), F32), jax.ShapeDtypeStruct((t, d), BF16)],
        compiler_params=_cparams(("parallel",)),
        name="ple",
    )(xb, w_pg, pb, w_pe, xf, g, b)


def kernel(x, p, positions, w_in, w_out, diff_lambda, diff_norm_g, gla_gate_up, gla_gate_b, gla_norm_g,
           ffn1_gate, ffn1_up, ffn1_down, ffn2_gate, ffn2_up, ffn2_down, w_pe, w_pg, ln_g, ln_b):
    batch, seq, d = x.shape
    t = batch * seq
    assert d == (DIFF_HEADS + GLA_HEADS + MOBA_HEADS) * HEAD_DIM and seq % MOBA_BLOCK == 0
    xf, xb = x.reshape(t, d), None
    rope_blocks = max(n for n in (1, 2, 4, 8, 16) if n <= t // MOBA_BLOCK)
    *tables, w_g, w_u, w_d = _rope_tables(positions.reshape(t, 1), steps=rope_blocks,
                                          hosted=[(a, 0, rope_blocks) for a in (ffn1_gate, ffn1_up, ffn1_down)])
    w_ffn1 = (w_g, w_u, w_d)
    vec = lambda v: v.reshape(1, -1)
    w_in_b, w_pe_b = _reorder_w_in(w_in), _cast_bf16(w_pe)
    p3 = p.reshape(DEPTH, t, -1)
    side_blocks = max(n for n in (1, 2, 4, 8, 16) if n <= batch * min(DIFF_HEADS, MOBA_HEADS))
    side = lambda arrs, layer: [(a, layer, side_blocks) for a in arrs]
    for i in range(DEPTH):
        more = i + 1 < DEPTH
        xf, xb = _ffn(xf, xb, w_ffn1, vec(ln_g[i, 0]), vec(ln_b[i, 0]))
        pa, pb, pc, w_gu2 = _proj(xb, w_in_b, i, tables,
                                  hosted=[(a, i, min(side_blocks, _proj_plain_steps(t))) for a in (ffn2_gate, ffn2_up)])
        o_diff, w_out_b, p_b, w_d2, *w_down_next = _diff_attention(
            pa, pc, diff_lambda[i], vec(diff_norm_g[i]), batch, seq, i,
            hosted=side((w_out, p3, ffn2_down), i) + (side((ffn1_down,), i + 1) if more else []))
        w_ffn2 = tuple(w_gu2) + (w_d2,)
        gg0 = _IN_OFFS[7]
        o_gla = _gla(xb, pc, w_in[i, :, gg0:gg0 + GLA_GATE_RANK], gla_gate_up[i], gla_gate_b[i],
                     vec(gla_norm_g[i]), batch, seq)
        o_moba, w_pg_b, *w_gu_next = _moba_attention(
            pb, pc, batch, seq, hosted=side((w_pg,), i) + (side((ffn1_gate, ffn1_up), i + 1) if more else []))
        xf, xb = _out_proj(o_diff, o_gla, o_moba, w_out_b, 0, xf, vec(ln_g[i, 1]), vec(ln_b[i, 1]))
        xf, xb = _ffn(xf, xb, w_ffn2, vec(ln_g[i, 2]), vec(ln_b[i, 2]))
        xf, xb = _ple(xf, xb, p_b, w_pe_b, i, w_pg_b, vec(ln_g[i, 3]), vec(ln_b[i, 3]))
        w_ffn1 = tuple(w_gu_next) + tuple(w_down_next)
    return xf.reshape(batch, seq, d)
```

```python
import functools
import math

import numpy as np
import jax
import jax.numpy as jnp
from jax import lax
from jax.experimental import pallas as pl
from jax.experimental.pallas import tpu as pltpu

F32 = jnp.float32
BF16 = jnp.bfloat16

DEPTH = 2
HEAD_DIM = 128
DIFF_HEADS = 6
GLA_HEADS = 4
MOBA_HEADS = 6
DIFF_QK_DIM = 64
GLA_DK = 64
GLA_DV = 128
GLA_GATE_RANK = 16
GLA_TAU = 16.0
GLA_CHUNK = 64
GLA_UNROLL = 4
MOBA_BLOCK = 256
MOBA_TOPK = 3
ROPE_THETA = 10000.0
LN_EPS = 1e-5
DEEPNORM_ALPHA = (2 * DEPTH) ** 0.25

LANES = 128
SUBLANES = 8
VMEM_LIMIT_BYTES = 56 * 1024 * 1024

PROJ_TN = 768
PROJ_ROW_CHUNK = 256
STREAM_ROW_CHUNK = 128
_CB = LANES
COL_DQ = 0
COL_DK = COL_DQ + DIFF_HEADS
COL_MQ = COL_DK + DIFF_HEADS
COL_MK = COL_MQ + MOBA_HEADS
COL_DV = COL_MK + MOBA_HEADS
COL_MV = COL_DV + DIFF_HEADS
COL_GQ = COL_MV + MOBA_HEADS
COL_GK = COL_GQ + GLA_HEADS * GLA_DK // LANES
COL_GV = COL_GK + GLA_HEADS * GLA_DK // LANES
COL_GR = COL_GV + GLA_HEADS
N_COL_BLOCKS = COL_GR + GLA_HEADS
PROJ_WIDTH = N_COL_BLOCKS * _CB
assert PROJ_WIDTH % PROJ_TN == 0
GG_PAD = LANES

NEG_INF = float("-inf")
LOG2_E = math.log2(math.e)
ATTN_LOOKAHEAD = 2
V_ROWS = HEAD_DIM + 16


def _cparams(sem):
    return pltpu.CompilerParams(dimension_semantics=sem, vmem_limit_bytes=VMEM_LIMIT_BYTES)


def _layer_norm(y, g, b):
    mu = jnp.mean(y, axis=-1, keepdims=True)
    d = y - mu
    var = jnp.mean(d * d, axis=-1, keepdims=True)
    return d * lax.rsqrt(var + LN_EPS) * g + b


def _dot_nt(a, b):
    return lax.dot_general(a, b, (((1,), (1,)), ((), ())), preferred_element_type=F32)


def _dot(a, b):
    return jnp.dot(a, b, preferred_element_type=F32)


def _head_cols(g):
    return slice(g * LANES, (g + 1) * LANES)


def _rope_table_kernel(pos_ref, inv_a_ref, inv_b_ref, sgn_a_ref, sgn_b_ref,
                       cos_a_ref, sin_a_ref, cos_b_ref, sin_b_ref):
    pos = pos_ref[...].astype(F32)
    ang_a = pos * inv_a_ref[...]
    ang_b = pos * inv_b_ref[...]
    cos_a_ref[...] = jnp.cos(ang_a)
    sin_a_ref[...] = jnp.sin(ang_a) * sgn_a_ref[...]
    cos_b_ref[...] = jnp.cos(ang_b)
    sin_b_ref[...] = jnp.sin(ang_b) * sgn_b_ref[...]


def _rope_tables(pos, hosted=(), steps=None):
    t = pos.shape[0]
    tm = next(c for c in (1024, 512, MOBA_BLOCK) if t % c == 0 and (steps is None or t // c >= steps))

    def pattern(d):
        inv = ROPE_THETA ** (-jnp.arange(0, d, 2, dtype=F32) / d)
        inv = jnp.tile(inv, 2 * LANES // d)[None, :]
        sgn = jnp.tile(jnp.concatenate([-jnp.ones(d // 2, F32), jnp.ones(d // 2, F32)]), LANES // d)[None, :]
        return inv, sgn

    inv_a, sgn_a = pattern(DIFF_QK_DIM)
    inv_b, sgn_b = pattern(HEAD_DIM)
    row = pl.BlockSpec((1, LANES), lambda i: (0, 0))
    out = pl.BlockSpec((tm, LANES), lambda i: (i, 0))
    h_in, h_out, h_shape, h_args = _hosted_specs(hosted)
    assert t // tm >= max([n for _, _, n in hosted], default=0)
    return pl.pallas_call(
        _with_hosted_casts(_rope_table_kernel, 5, 4, len(hosted)),
        grid=(t // tm,),
        in_specs=[pl.BlockSpec((tm, 1), lambda i: (i, 0)), row, row, row, row] + h_in,
        out_specs=[out, out, out, out] + h_out,
        out_shape=[jax.ShapeDtypeStruct((t, LANES), F32)] * 4 + h_shape,
        compiler_params=_cparams(("arbitrary",)),
        name="rope_tables",
    )(pos, inv_a, inv_b, sgn_a, sgn_b, *h_args)


def _ffn_kernel(xf_ref, wg_ref, wu_ref, wd_ref, g_ref, b_ref, of_ref, ob_ref, acc_ref, *, nf):
    j = pl.program_id(1)

    @pl.when(j == 0)
    def _():
        acc_ref[...] = jnp.zeros_like(acc_ref)

    x = xf_ref[...].astype(BF16)
    gt = _dot(x, wg_ref[0])
    ut = _dot(x, wu_ref[0])
    h = (gt * jax.nn.sigmoid(gt) * ut).astype(BF16)
    acc_ref[...] += _dot(h, wd_ref[0])

    @pl.when(j == nf - 1)
    def _():
        _emit_stream(of_ref, ob_ref, DEEPNORM_ALPHA * xf_ref[...] + 0.5 * acc_ref[...], g_ref, b_ref)


def _ffn_tiles(t, f):
    tm = 512 if t % 512 == 0 else t
    tf = 512 if f % 512 == 0 else f
    return tm, tf


def _ffn(xf, w, g, b):
    wg, wu, wd = w
    t, d = xf.shape
    f = wg.shape[2]
    tm, tf = _ffn_tiles(t, f)
    nf = f // tf
    row = pl.BlockSpec((tm, d), lambda i, j: (i, 0))
    vec = pl.BlockSpec((1, d), lambda i, j: (0, 0))
    return pl.pallas_call(
        functools.partial(_ffn_kernel, nf=nf),
        grid=(t // tm, nf),
        in_specs=[row,
                  pl.BlockSpec((1, d, tf), lambda i, j: (0, 0, j)),
                  pl.BlockSpec((1, d, tf), lambda i, j: (0, 0, j)),
                  pl.BlockSpec((1, tf, d), lambda i, j: (0, j, 0)),
                  vec, vec],
        out_specs=[row, row],
        out_shape=[jax.ShapeDtypeStruct((t, d), F32), jax.ShapeDtypeStruct((t, d), BF16)],
        scratch_shapes=[pltpu.VMEM((tm, d), F32)],
        compiler_params=_cparams(("parallel", "arbitrary")),
        name="ffn",
    )(xf, wg, wu, wd, g, b)


def _proj_rope_kernel(xb_ref, w_ref, cos_ref, sin_ref, p_ref, *, rot_dim, q_scale, q_tiles):
    j = pl.program_id(1)
    scale = jnp.where(j < q_tiles, q_scale, 1.0).astype(F32)
    half = rot_dim // 2
    tm = xb_ref.shape[0]
    rc = PROJ_ROW_CHUNK if tm % PROJ_ROW_CHUNK == 0 else tm
    for r0 in range(0, tm, rc):
        rows = slice(r0, r0 + rc)
        acc = _dot_nt(xb_ref[rows, :], w_ref[0])
        cos, sin = cos_ref[rows, :], sin_ref[rows, :]
        for hh in range(PROJ_TN // LANES):
            t = acc[:, _head_cols(hh)]
            if rot_dim == LANES:
                rot = pltpu.roll(t, half, 1)
            else:
                lane = lax.broadcasted_iota(jnp.int32, t.shape, 1)
                rot = jnp.where((lane // half) % 2 == 0, pltpu.roll(t, LANES - half, 1), pltpu.roll(t, half, 1))
            p_ref[rows, _head_cols(hh)] = ((t * cos + rot * sin) * scale).astype(BF16)


def _proj_plain_kernel(xb_ref, w_ref, p_ref, *, gq_cols):
    j = pl.program_id(1)
    col = j * PROJ_TN + lax.broadcasted_iota(jnp.int32, (1, PROJ_TN), 1)
    scale = jnp.where((col >= gq_cols[0]) & (col < gq_cols[1]), GLA_DK ** -0.5, 1.0).astype(F32)
    tm = xb_ref.shape[0]
    rc = PROJ_ROW_CHUNK if tm % PROJ_ROW_CHUNK == 0 else tm
    for r0 in range(0, tm, rc):
        rows = slice(r0, r0 + rc)
        p_ref[rows, :] = (_dot_nt(xb_ref[rows, :], w_ref[0]) * scale).astype(BF16)


def _proj_row_tile(t):
    return 2048 if t % 2048 == 0 else t


def _proj_plain_steps(t):
    return (t // _proj_row_tile(t)) * ((N_COL_BLOCKS - COL_DV) * LANES // PROJ_TN)


def _proj(xb, w, layer, tables, hosted=()):
    t, d = xb.shape
    tm = _proj_row_tile(t)

    def tiles(c):
        assert (c * LANES) % PROJ_TN == 0
        return c * LANES // PROJ_TN

    cos_a, sin_a, cos_b, sin_b = tables
    x_spec = pl.BlockSpec((tm, d), lambda i, j: (i, 0))
    tab = pl.BlockSpec((tm, LANES), lambda i, j: (i, 0))
    out = pl.BlockSpec((tm, PROJ_TN), lambda i, j: (i, j))
    w_spec = lambda first: pl.BlockSpec((1, PROJ_TN, d), lambda i, j: (layer, first + j, 0))

    def rope_group(first_col, q_heads, n_heads, rot_dim, q_scale, cos, sin, name):
        nj = tiles(n_heads)
        return pl.pallas_call(
            functools.partial(_proj_rope_kernel, rot_dim=rot_dim, q_scale=q_scale, q_tiles=tiles(q_heads)),
            grid=(t // tm, nj),
            in_specs=[x_spec, w_spec(tiles(first_col)), tab, tab],
            out_specs=out,
            out_shape=jax.ShapeDtypeStruct((t, nj * PROJ_TN), BF16),
            compiler_params=_cparams(("parallel", "arbitrary")),
            name=name,
        )(xb, w, cos, sin)

    pa = rope_group(COL_DQ, DIFF_HEADS, 2 * DIFF_HEADS, DIFF_QK_DIM, DIFF_QK_DIM ** -0.5 * LOG2_E, cos_a, sin_a,
                    "in_proj_diff")
    pb = rope_group(COL_MQ, MOBA_HEADS, 2 * MOBA_HEADS, HEAD_DIM, HEAD_DIM ** -0.5 * LOG2_E, cos_b, sin_b,
                    "in_proj_moba")
    nj = tiles(N_COL_BLOCKS - COL_DV)
    h_in, h_out, h_shape, h_args = _hosted_specs(hosted, nj)
    assert (t // tm) * nj >= max([n for _, _, n in hosted], default=0)
    pc, *side = pl.pallas_call(
        _with_hosted_casts(functools.partial(_proj_plain_kernel, gq_cols=((COL_GQ - COL_DV) * LANES,
                                                                           (COL_GK - COL_DV) * LANES)),
                           2, 1, len(hosted)),
        grid=(t // tm, nj),
        in_specs=[x_spec, w_spec(tiles(COL_DV))] + h_in,
        out_specs=[out] + h_out,
        out_shape=[jax.ShapeDtypeStruct((t, nj * PROJ_TN), BF16)] + h_shape,
        compiler_params=_cparams(("arbitrary", "arbitrary")),
        name="in_proj_plain",
    )(xb, w, *h_args)
    return pa, pb, pc, side


_IN_WIDTHS = (DIFF_HEADS * 128, DIFF_HEADS * 128, DIFF_HEADS * 128, GLA_HEADS * GLA_DK, GLA_HEADS * GLA_DK,
              GLA_HEADS * GLA_DV, GLA_HEADS * GLA_DV, GLA_GATE_RANK, MOBA_HEADS * 128, MOBA_HEADS * 128,
              MOBA_HEADS * 128)
_IN_OFFS = tuple(int(v) for v in np.concatenate([[0], np.cumsum(_IN_WIDTHS)]))
D_IN = _IN_OFFS[-1]


def _reorder_w_in_kernel(w_ref, o_ref):
    src = dict(zip(("dq", "dk", "dv", "gq", "gk", "gv", "gr", "gg", "mq", "mk", "mv"), zip(_IN_OFFS[:-1], _IN_WIDTHS)))
    for name, col in (("dq", COL_DQ), ("dk", COL_DK), ("mq", COL_MQ), ("mk", COL_MK), ("dv", COL_DV), ("mv", COL_MV),
                      ("gq", COL_GQ), ("gk", COL_GK), ("gv", COL_GV), ("gr", COL_GR)):
        s0, width = src[name]
        o_ref[0, col * LANES:col * LANES + width, :] = w_ref[0, s0:s0 + width, :].astype(BF16)


def _reorder_w_in(w_in):
    depth, d, d_in = w_in.shape
    assert d_in == D_IN and all(o % 16 == 0 for o in _IN_OFFS)
    wt = jnp.swapaxes(w_in, 1, 2)
    tc = 128 if d % 128 == 0 else d
    return pl.pallas_call(
        _reorder_w_in_kernel,
        grid=(depth, d // tc),
        in_specs=[pl.BlockSpec((1, d_in, tc), lambda l, c: (l, 0, c))],
        out_specs=pl.BlockSpec((1, PROJ_WIDTH, tc), lambda l, c: (l, 0, c)),
        out_shape=jax.ShapeDtypeStruct((depth, PROJ_WIDTH, d), BF16),
        compiler_params=_cparams(("parallel", "parallel")),
        name="reorder_w_in",
    )(wt)


def _cast_kernel(x_ref, o_ref):
    o_ref[...] = x_ref[...].astype(o_ref.dtype)


def _cast_bf16(x):
    l, r, c = x.shape
    tr = next((t for t in (1024, 512, 256, 128) if r % t == 0 and t * c * 4 <= (4 << 20)), r)
    spec = pl.BlockSpec((1, tr, c), lambda i, j: (i, j, 0))
    return pl.pallas_call(
        _cast_kernel,
        grid=(l, r // tr),
        in_specs=[spec],
        out_specs=spec,
        out_shape=jax.ShapeDtypeStruct(x.shape, BF16),
        compiler_params=_cparams(("parallel", "parallel")),
        name="cast_bf16",
    )(x)


def _store_v_transposed(v_ref, vt_ref, tk):
    seq = v_ref.shape[0]
    for blk in range(seq // tk):
        vt_ref[0:HEAD_DIM, blk * tk:(blk + 1) * tk] = v_ref[blk * tk:(blk + 1) * tk, :].astype(F32).T.astype(BF16)
    vt_ref[HEAD_DIM:V_ROWS, :] = jnp.ones((V_ROWS - HEAD_DIM, seq), BF16)


def _run_tiles(n, scores, finish):
    pending = [scores(t) for t in range(min(ATTN_LOOKAHEAD, n))]
    for t in range(n):
        if t + ATTN_LOOKAHEAD < n:
            pending.append(scores(t + ATTN_LOOKAHEAD))
        finish(t, pending.pop(0))


def _col_max(parts):
    m = jnp.max(parts[0], axis=0, keepdims=True)
    for s in parts[1:]:
        m = jnp.maximum(m, jnp.max(s, axis=0, keepdims=True))
    return m


def _diff_kernel(lam_ref, g_ref, q_ref, k_ref, v_ref, o_ref, vt_ref, *, tq, lam_init):
    seq = k_ref.shape[0]
    _store_v_transposed(v_ref, vt_ref, tq)
    lf = lam_ref[...]
    lam = (jnp.exp(jnp.sum(lf[0:1] * lf[1:2], axis=1, keepdims=True))
           - jnp.exp(jnp.sum(lf[2:3] * lf[3:4], axis=1, keepdims=True)) + lam_init)
    lane = lax.broadcasted_iota(jnp.int32, (tq, LANES), 1)
    causal = (lax.broadcasted_iota(jnp.int32, (tq, 2 * tq), 0)
              <= lax.broadcasted_iota(jnp.int32, (tq, 2 * tq), 1) % tq)
    def scores(qi):
        lo, hi = qi * tq, (qi + 1) * tq
        q = q_ref[lo:hi, :].astype(F32)
        qq = jnp.concatenate([jnp.where(lane < DIFF_QK_DIM, q, 0.0),
                              jnp.where(lane >= DIFF_QK_DIM, q, 0.0)], axis=0).astype(BF16)
        parts = [jnp.where(causal, _dot_nt(k_ref[lo:hi, :], qq), NEG_INF)]
        if qi > 0:
            parts.append(_dot_nt(k_ref[0:lo, :], qq))
        return parts

    def finish(qi, parts):
        lo, hi = qi * tq, (qi + 1) * tq
        m = _col_max(parts)
        acc = _dot(vt_ref[:, lo:hi], jnp.exp2(parts[0] - m).astype(BF16))
        if qi > 0:
            acc = acc + _dot(vt_ref[:, 0:lo], jnp.exp2(parts[1] - m).astype(BF16))
        o_t = acc[0:HEAD_DIM] / acc[HEAD_DIM:HEAD_DIM + 1]
        o = (o_t[:, :tq] - lam * o_t[:, tq:]).T
        o = o * lax.rsqrt(jnp.mean(o * o, axis=-1, keepdims=True) + LN_EPS) * g_ref[...] * (1.0 - lam_init)
        o_ref[lo:hi, :] = o.astype(BF16)

    _run_tiles(seq // tq, scores, finish)


def _with_hosted_casts(kernel_fn, n_in, n_out, n_hosted):
    def wrapped(*refs):
        ins, rest = refs[:n_in], refs[n_in:]
        srcs, rest = rest[:n_hosted], rest[n_hosted:]
        outs, rest = rest[:n_out], rest[n_out:]
        dsts, scratch = rest[:n_hosted], rest[n_hosted:]
        for src, dst in zip(srcs, dsts):
            dst[...] = src[...].astype(BF16)
        kernel_fn(*ins, *outs, *scratch)
    return wrapped


def _hosted_specs(hosted, n_inner=None):
    step = (lambda i: i) if n_inner is None else (lambda b, h: b * n_inner + h)
    ins, outs, shapes = [], [], []
    for arr, layer, nblk in hosted:
        _, r, c = arr.shape
        assert r % nblk == 0
        blk = lambda *g, nblk=nblk: jnp.minimum(step(*g), nblk - 1)
        ins.append(pl.BlockSpec((1, r // nblk, c), lambda *g, layer=layer, blk=blk: (layer, blk(*g), 0)))
        outs.append(pl.BlockSpec((1, r // nblk, c), lambda *g, blk=blk: (0, blk(*g), 0)))
        shapes.append(jax.ShapeDtypeStruct((1, r, c), BF16))
    return ins, outs, shapes, [arr for arr, _, _ in hosted]


def _diff_attention(pa, pc, lam, g, batch, seq, layer, hosted=()):
    tq = 256
    lam_init = 0.8 - 0.6 * math.exp(-0.3 * layer)
    col = lambda c0: pl.BlockSpec((seq, LANES), lambda b, h: (b, c0 + h))
    h_in, h_out, h_shape, h_args = _hosted_specs(hosted, DIFF_HEADS)
    assert batch * DIFF_HEADS >= max([n for _, _, n in hosted], default=0)
    return pl.pallas_call(
        _with_hosted_casts(functools.partial(_diff_kernel, tq=tq, lam_init=lam_init), 5, 1, len(hosted)),
        grid=(batch, DIFF_HEADS),
        in_specs=[pl.BlockSpec((4, DIFF_QK_DIM), lambda b, h: (0, 0)),
                  pl.BlockSpec((1, LANES), lambda b, h: (0, 0)),
                  col(0), col(DIFF_HEADS), col(0)] + h_in,
        out_specs=[pl.BlockSpec((seq, LANES), lambda b, h: (b, h))] + h_out,
        out_shape=[jax.ShapeDtypeStruct((batch * seq, DIFF_HEADS * LANES), BF16)] + h_shape,
        scratch_shapes=[pltpu.VMEM((V_ROWS, seq), BF16)],
        compiler_params=_cparams(("arbitrary", "arbitrary")),
        name="diff_attn",
    )(lam, g, pa, pa, pc, *h_args)


def _moba_kernel(q_ref, k_ref, v_ref, o_ref, vt_ref, *, nb):
    tq = MOBA_BLOCK
    _store_v_transposed(v_ref, vt_ref, tq)
    nbp = -(-nb // SUBLANES) * SUBLANES
    rows = [jnp.mean(k_ref[n * tq:(n + 1) * tq, :].astype(F32), axis=0, keepdims=True) for n in range(nb)]
    if nbp > nb:
        rows.append(jnp.zeros((nbp - nb, LANES), F32))
    km = jnp.concatenate(rows, axis=0)
    km_hi = km.astype(BF16)
    km_lo = (km - km_hi.astype(F32)).astype(BF16)
    causal = (lax.broadcasted_iota(jnp.int32, (tq, tq), 0) <= lax.broadcasted_iota(jnp.int32, (tq, tq), 1))
    def scores(qi):
        lo, hi = qi * tq, (qi + 1) * tq
        q = q_ref[lo:hi, :]
        parts = [jnp.where(causal, _dot_nt(k_ref[lo:hi, :], q), NEG_INF)]
        if qi > 0:
            s_past = _dot_nt(k_ref[0:lo, :], q)
            if qi > MOBA_TOPK:
                gate = _dot_nt(km_hi, q) + _dot_nt(km_lo, q)
                n_idx = lax.broadcasted_iota(jnp.int32, gate.shape, 0)
                rank = jnp.zeros(gate.shape, jnp.int32)
                for mb in range(qi):
                    gm = gate[mb:mb + 1, :]
                    rank = rank + jnp.where((gm > gate) | ((gm == gate) & (mb < n_idx)), 1, 0)
                bias = jnp.where(rank < MOBA_TOPK, 0.0, NEG_INF)
                s_past = jnp.concatenate([s_past[n * tq:(n + 1) * tq] + bias[n:n + 1, :] for n in range(qi)], axis=0)
            parts.append(s_past)
        return parts

    def finish(qi, parts):
        lo, hi = qi * tq, (qi + 1) * tq
        m = _col_max(parts)
        acc = _dot(vt_ref[:, lo:hi], jnp.exp2(parts[0] - m).astype(BF16))
        if qi > 0:
            acc = acc + _dot(vt_ref[:, 0:lo], jnp.exp2(parts[1] - m).astype(BF16))
        o_ref[lo:hi, :] = (acc[0:HEAD_DIM] / acc[HEAD_DIM:HEAD_DIM + 1]).T.astype(BF16)

    _run_tiles(nb, scores, finish)


def _moba_attention(pb, pc, batch, seq, hosted=()):
    assert seq % MOBA_BLOCK == 0
    col = lambda c0: pl.BlockSpec((seq, LANES), lambda b, h: (b, c0 + h))
    h_in, h_out, h_shape, h_args = _hosted_specs(hosted, MOBA_HEADS)
    assert batch * MOBA_HEADS >= max([n for _, _, n in hosted], default=0)
    return pl.pallas_call(
        _with_hosted_casts(functools.partial(_moba_kernel, nb=seq // MOBA_BLOCK), 3, 1, len(hosted)),
        grid=(batch, MOBA_HEADS),
        in_specs=[col(0), col(MOBA_HEADS), col(COL_MV - COL_DV)] + h_in,
        out_specs=[pl.BlockSpec((seq, LANES), lambda b, h: (b, h))] + h_out,
        out_shape=[jax.ShapeDtypeStruct((batch * seq, MOBA_HEADS * LANES), BF16)] + h_shape,
        scratch_shapes=[pltpu.VMEM((V_ROWS, seq), BF16)],
        compiler_params=_cparams(("arbitrary", "arbitrary")),
        name="moba_attn",
    )(pb, pb, pc, *h_args)


def _gla_constants():
    c = GLA_CHUNK
    idx = np.arange(c)
    mats = [np.tril(np.ones((c, c), np.float32))]
    bm = []
    for lvl in range(6):
        m = 32 >> lvl
        parent, half = idx // (2 * m), (idx // m) % 2
        ref = parent * 2 * m + m - 1
        t = idx[None, :]
        second = (half == 1)[:, None] & (t > ref[:, None]) & (t <= idx[:, None])
        first = (half == 0)[:, None] & (t > idx[:, None]) & (t <= ref[:, None])
        mats.append((second | first).astype(np.float32))
        bm.append(((parent[:, None] == parent[None, :]) & (half == 1)[:, None] & (half == 0)[None, :]).astype(np.float32))
    bm.append(np.eye(c, dtype=np.float32))
    cmat = np.concatenate(mats, axis=0)
    return jnp.asarray(cmat, BF16), jnp.asarray(np.concatenate(bm, axis=0), F32)


def _split2(x):
    hi = x.astype(BF16)
    return hi, (x - hi.astype(F32)).astype(BF16)


def _gla_kernel(cmat_ref, bm_ref, wgg_ref, up_ref, bias_ref, gn_ref, x_ref, q_ref, k_ref, v_ref, r_ref, o_ref,
                la_ref, st_ref, *, seq, heads):
    c = GLA_CHUNK
    wg = wgg_ref[...].astype(BF16)
    up_hi, up_lo = _split2(up_ref[...])
    rows = 512 if seq % 512 == 0 else seq
    for r0 in range(0, seq, rows):
        g_hi, g_lo = _split2(_dot(x_ref[r0:r0 + rows, :], wg))
        z = _dot(g_hi, up_hi) + _dot(g_hi, up_lo) + _dot(g_lo, up_hi) + bias_ref[...]
        la_ref[r0:r0 + rows, :] = (jnp.minimum(z, 0.0) - jnp.log(1.0 + jnp.exp(-jnp.abs(z)))) * (1.0 / GLA_TAU)

    st_ref[...] = jnp.zeros_like(st_ref)
    cmat = cmat_ref[...]
    lane = lax.broadcasted_iota(jnp.int32, (c, LANES), 1)
    own = [lane < GLA_DK, lane >= GLA_DK]

    def halves(t):
        return [jnp.where(m, t, 0.0).astype(BF16) for m in own]

    def chunk_group(cp, carry):
        sls = [pl.ds(pl.multiple_of((cp * GLA_UNROLL + u) * c, c), c) for u in range(GLA_UNROLL)]
        e_alls = []
        for sl in sls:
            g_hi, g_lo = _split2(la_ref[sl, :])
            e_alls.append(_dot(cmat, g_hi) + _dot(cmat, g_lo))
        ids = [(sl, p, e_all[:, _head_cols(p)]) for sl, e_all in zip(sls, e_alls) for p in range(heads // 2)]
        qs = [q_ref[sl, _head_cols(p)].astype(F32) for sl, p, _ in ids]
        ks = [k_ref[sl, _head_cols(p)].astype(F32) for sl, p, _ in ids]
        atts = []
        for q, k in zip(qs, ks):
            kb = k.astype(BF16)
            atts.append([bm_ref[6 * c:7 * c, :] * _dot_nt(qh, kb) for qh in halves(q)])
        for lvl in range(6):
            for i, (_, _, e) in enumerate(ids):
                w = jnp.exp(e[(1 + lvl) * c:(2 + lvl) * c])
                kw = (ks[i] * w).astype(BF16)
                for hf, qh in enumerate(halves(qs[i] * w)):
                    atts[i][hf] = atts[i][hf] + bm_ref[lvl * c:(lvl + 1) * c, :] * _dot_nt(qh, kw)
        streams = []
        for (sl, p, e), q, k, att in zip(ids, qs, ks, atts):
            cum = e[0:c]
            last = cum[c - 1:c, :]
            qes, kds = halves(q * jnp.exp(cum)), halves(k * jnp.exp(last - cum))
            for hf in range(2):
                h = 2 * p + hf
                v = v_ref[sl, _head_cols(h)]
                o_intra = _dot(att[hf].astype(BF16), v)
                upd = _dot(v.astype(F32).T.astype(BF16), kds[hf])
                streams.append((sl, h, qes[hf], o_intra, upd, jnp.exp(last)))
        for args in streams:
            recur(*args)
        return carry

    def recur(sl, h, qe, o_intra, upd, decay):
        hc = _head_cols(h)
        st = st_ref[h]
        o = _dot_nt(qe, st.astype(BF16)) + o_intra
        st_ref[h] = st * decay + upd
        o = o * lax.rsqrt(jnp.mean(o * o, axis=-1, keepdims=True) + LN_EPS) * gn_ref[...]
        r = r_ref[sl, hc].astype(F32)
        o_ref[sl, hc] = (o * (r * jax.nn.sigmoid(r))).astype(BF16)

    assert (seq // c) % GLA_UNROLL == 0 and heads % 2 == 0
    lax.fori_loop(0, seq // c // GLA_UNROLL, chunk_group, 0)


def _gla(xb, pc, w_gg, up, bias, gn, batch, seq):
    cmat, bm = _gla_constants()
    heads = GLA_HEADS
    d = xb.shape[1]
    qk_w, v_w = heads * GLA_DK, heads * GLA_DV
    wgg_p = jnp.pad(w_gg, ((0, 0), (0, GG_PAD - GLA_GATE_RANK)))
    up_p = jnp.pad(up, ((0, GG_PAD - GLA_GATE_RANK), (0, 0)))
    const = lambda shape: pl.BlockSpec(shape, lambda b: (0, 0))
    col = lambda c0, width: pl.BlockSpec((seq, width), lambda b: (b, (c0 - COL_DV) * LANES // width))
    return pl.pallas_call(
        functools.partial(_gla_kernel, seq=seq, heads=heads),
        grid=(batch,),
        in_specs=[const(cmat.shape), const(bm.shape), const((d, GG_PAD)), const((GG_PAD, qk_w)), const((1, qk_w)),
                  const((1, LANES)), pl.BlockSpec((seq, d), lambda b: (b, 0)),
                  col(COL_GQ, qk_w), col(COL_GK, qk_w), col(COL_GV, v_w), col(COL_GR, v_w)],
        out_specs=pl.BlockSpec((seq, v_w), lambda b: (b, 0)),
        out_shape=jax.ShapeDtypeStruct((batch * seq, v_w), BF16),
        scratch_shapes=[pltpu.VMEM((seq, qk_w), F32), pltpu.VMEM((heads, GLA_DV, LANES), F32)],
        compiler_params=_cparams(("parallel",)),
        name="gla",
    )(cmat, bm, wgg_p, up_p, bias.reshape(1, qk_w), gn, xb, pc, pc, pc, pc)


def _emit_stream(of_ref, ob_ref, y, g_ref, b_ref, rows=slice(None)):
    out = _layer_norm(y, g_ref[...], b_ref[...])
    of_ref[rows, :] = out
    ob_ref[rows, :] = out.astype(BF16)


def _row_chunks(tm):
    rc = STREAM_ROW_CHUNK if tm % STREAM_ROW_CHUNK == 0 else tm
    return [slice(r0, r0 + rc) for r0 in range(0, tm, rc)]


def _out_kernel(od_ref, og_ref, om_ref, w_ref, xf_ref, g_ref, b_ref, of_ref, ob_ref):
    k0 = od_ref.shape[1]
    k1 = k0 + og_ref.shape[1]
    for rows in _row_chunks(xf_ref.shape[0]):
        y = (_dot(od_ref[rows, :], w_ref[0, 0:k0, :]) + _dot(og_ref[rows, :], w_ref[0, k0:k1, :])
             + _dot(om_ref[rows, :], w_ref[0, k1:, :]))
        _emit_stream(of_ref, ob_ref, DEEPNORM_ALPHA * xf_ref[rows, :] + y, g_ref, b_ref, rows)


def _row_tile(t):
    return 512 if t % 512 == 0 else t


def _out_proj(o_diff, o_gla, o_moba, w, layer, xf, g, b):
    t, d = xf.shape
    tm = _row_tile(t)
    row = pl.BlockSpec((tm, d), lambda i: (i, 0))
    vec = pl.BlockSpec((1, d), lambda i: (0, 0))
    part = lambda a: pl.BlockSpec((tm, a.shape[1]), lambda i: (i, 0))
    return pl.pallas_call(
        _out_kernel,
        grid=(t // tm,),
        in_specs=[part(o_diff), part(o_gla), part(o_moba),
                  pl.BlockSpec((1,) + w.shape[1:], lambda i: (layer, 0, 0)),
                  row, vec, vec],
        out_specs=[row, row],
        out_shape=[jax.ShapeDtypeStruct((t, d), F32), jax.ShapeDtypeStruct((t, d), BF16)],
        compiler_params=_cparams(("parallel",)),
        name="out_proj",
    )(o_diff, o_gla, o_moba, w, xf, g, b)


def _ple_kernel(xb_ref, wg_ref, pb_ref, we_ref, xf_ref, g_ref, b_ref, of_ref, ob_ref):
    for rows in _row_chunks(xf_ref.shape[0]):
        e = _dot(pb_ref[0, rows, :], we_ref[0]) * jax.nn.sigmoid(_dot(xb_ref[rows, :], wg_ref[0]))
        _emit_stream(of_ref, ob_ref, DEEPNORM_ALPHA * xf_ref[rows, :] + e, g_ref, b_ref, rows)


def _ple(xf, xb, pb, w_pe, pe_layer, w_pg, g, b):
    t, d = xf.shape
    tm = _row_tile(t)
    row = pl.BlockSpec((tm, d), lambda i: (i, 0))
    vec = pl.BlockSpec((1, d), lambda i: (0, 0))
    whole = lambda a, layer: pl.BlockSpec((1,) + a.shape[1:], lambda i: (layer, 0, 0))
    return pl.pallas_call(
        _ple_kernel,
        grid=(t // tm,),
        in_specs=[row, whole(w_pg, 0),
                  pl.BlockSpec((1, tm, pb.shape[2]), lambda i: (0, i, 0)),
                  whole(w_pe, pe_layer), row, vec, vec],
        out_specs=[row, row],
        out_shape=[jax.ShapeDtypeStruct((t, d), F32), jax.ShapeDtypeStruct((t, d), BF16)],
        compiler_params=_cparams(("parallel",)),
        name="ple",
    )(xb, w_pg, pb, w_pe, xf, g, b)


def kernel(x, p, positions, w_in, w_out, diff_lambda, diff_norm_g, gla_gate_up, gla_gate_b, gla_norm_g,
           ffn1_gate, ffn1_up, ffn1_down, ffn2_gate, ffn2_up, ffn2_down, w_pe, w_pg, ln_g, ln_b):
    batch, seq, d = x.shape
    t = batch * seq
    assert d == (DIFF_HEADS + GLA_HEADS + MOBA_HEADS) * HEAD_DIM and seq % MOBA_BLOCK == 0
    xf = x.reshape(t, d)
    rope_blocks = max(n for n in (1, 2, 4, 8, 16) if n <= t // MOBA_BLOCK)
    *tables, w_g, w_u, w_d = _rope_tables(positions.reshape(t, 1), steps=rope_blocks,
                                          hosted=[(a, 0, rope_blocks) for a in (ffn1_gate, ffn1_up, ffn1_down)])
    w_ffn1 = (w_g, w_u, w_d)
    vec = lambda v: v.reshape(1, -1)
    w_in_b, w_pe_b = _reorder_w_in(w_in), _cast_bf16(w_pe)
    p3 = p.reshape(DEPTH, t, -1)
    side_blocks = max(n for n in (1, 2, 4, 8, 16) if n <= batch * min(DIFF_HEADS, MOBA_HEADS))
    side = lambda arrs, layer: [(a, layer, side_blocks) for a in arrs]
    for i in range(DEPTH):
        more = i + 1 < DEPTH
        xf, xb = _ffn(xf, w_ffn1, vec(ln_g[i, 0]), vec(ln_b[i, 0]))
        pa, pb, pc, w_gu2 = _proj(xb, w_in_b, i, tables,
                                  hosted=[(a, i, min(side_blocks, _proj_plain_steps(t))) for a in (ffn2_gate, ffn2_up)])
        o_diff, w_out_b, p_b, w_d2, *w_down_next = _diff_attention(
            pa, pc, diff_lambda[i], vec(diff_norm_g[i]), batch, seq, i,
            hosted=side((w_out, p3, ffn2_down), i) + (side((ffn1_down,), i + 1) if more else []))
        w_ffn2 = tuple(w_gu2) + (w_d2,)
        gg0 = _IN_OFFS[7]
        o_gla = _gla(xb, pc, w_in[i, :, gg0:gg0 + GLA_GATE_RANK], gla_gate_up[i], gla_gate_b[i],
                     vec(gla_norm_g[i]), batch, seq)
        o_moba, w_pg_b, *w_gu_next = _moba_attention(
            pb, pc, batch, seq, hosted=side((w_pg,), i) + (side((ffn1_gate, ffn1_up), i + 1) if more else []))
        xf, xb = _out_proj(o_diff, o_gla, o_moba, w_out_b, 0, xf, vec(ln_g[i, 1]), vec(ln_b[i, 1]))
        xf, xb = _ffn(xf, w_ffn2, vec(ln_g[i, 2]), vec(ln_b[i, 2]))
        xf, xb = _ple(xf, xb, p_b, w_pe_b, i, w_pg_b, vec(ln_g[i, 3]), vec(ln_b[i, 3]))
        w_ffn1 = tuple(w_gu_next) + tuple(w_down_next)
    return xf.reshape(batch, seq, d)
```

```python
import functools
import math

import numpy as np
import jax
import jax.numpy as jnp
from jax import lax
from jax.experimental import pallas as pl
from jax.experimental.pallas import tpu as pltpu

F32 = jnp.float32
BF16 = jnp.bfloat16

DEPTH = 2
HEAD_DIM = 128
DIFF_HEADS = 6
GLA_HEADS = 4
MOBA_HEADS = 6
DIFF_QK_DIM = 64
GLA_DK = 64
GLA_DV = 128
GLA_GATE_RANK = 16
GLA_TAU = 16.0
GLA_CHUNK = 64
GLA_UNROLL = 4
MOBA_BLOCK = 256
MOBA_TOPK = 3
ROPE_THETA = 10000.0
LN_EPS = 1e-5
DEEPNORM_ALPHA = (2 * DEPTH) ** 0.25

LANES = 128
SUBLANES = 8
VMEM_LIMIT_BYTES = 56 * 1024 * 1024

PROJ_TN = 768
PROJ_ROW_CHUNK = 256
STREAM_ROW_CHUNK = 128
_CB = LANES
COL_DQ = 0
COL_DK = COL_DQ + DIFF_HEADS
COL_MQ = COL_DK + DIFF_HEADS
COL_MK = COL_MQ + MOBA_HEADS
COL_DV = COL_MK + MOBA_HEADS
COL_MV = COL_DV + DIFF_HEADS
COL_GQ = COL_MV + MOBA_HEADS
COL_GK = COL_GQ + GLA_HEADS * GLA_DK // LANES
COL_GV = COL_GK + GLA_HEADS * GLA_DK // LANES
COL_GR = COL_GV + GLA_HEADS
N_COL_BLOCKS = COL_GR + GLA_HEADS
PROJ_WIDTH = N_COL_BLOCKS * _CB
assert PROJ_WIDTH % PROJ_TN == 0
GG_PAD = LANES

NEG_INF = float("-inf")
LOG2_E = math.log2(math.e)
ATTN_LOOKAHEAD = 2
V_ROWS = HEAD_DIM + 16


def _cparams(sem):
    return pltpu.CompilerParams(dimension_semantics=sem, vmem_limit_bytes=VMEM_LIMIT_BYTES)


def _layer_norm(y, g, b):
    mu = jnp.mean(y, axis=-1, keepdims=True)
    d = y - mu
    var = jnp.mean(d * d, axis=-1, keepdims=True)
    return d * lax.rsqrt(var + LN_EPS) * g + b


def _dot_nt(a, b):
    return lax.dot_general(a, b, (((1,), (1,)), ((), ())), preferred_element_type=F32)


def _dot(a, b):
    return jnp.dot(a, b, preferred_element_type=F32)


def _head_cols(g):
    return slice(g * LANES, (g + 1) * LANES)


def _rope_table_kernel(pos_ref, inv_a_ref, inv_b_ref, sgn_a_ref, sgn_b_ref,
                       cos_a_ref, sin_a_ref, cos_b_ref, sin_b_ref):
    pos = pos_ref[...].astype(F32)
    ang_a = pos * inv_a_ref[...]
    ang_b = pos * inv_b_ref[...]
    cos_a_ref[...] = jnp.cos(ang_a)
    sin_a_ref[...] = jnp.sin(ang_a) * sgn_a_ref[...]
    cos_b_ref[...] = jnp.cos(ang_b)
    sin_b_ref[...] = jnp.sin(ang_b) * sgn_b_ref[...]


def _rope_tables(pos, hosted=(), steps=None):
    t = pos.shape[0]
    tm = next(c for c in (1024, 512, MOBA_BLOCK) if t % c == 0 and (steps is None or t // c >= steps))

    def pattern(d):
        inv = ROPE_THETA ** (-jnp.arange(0, d, 2, dtype=F32) / d)
        inv = jnp.tile(inv, 2 * LANES // d)[None, :]
        sgn = jnp.tile(jnp.concatenate([-jnp.ones(d // 2, F32), jnp.ones(d // 2, F32)]), LANES // d)[None, :]
        return inv, sgn

    inv_a, sgn_a = pattern(DIFF_QK_DIM)
    inv_b, sgn_b = pattern(HEAD_DIM)
    row = pl.BlockSpec((1, LANES), lambda i: (0, 0))
    out = pl.BlockSpec((tm, LANES), lambda i: (i, 0))
    h_in, h_out, h_shape, h_args = _hosted_specs(hosted)
    assert t // tm >= max([n for _, _, n in hosted], default=0)
    return pl.pallas_call(
        _with_hosted_casts(_rope_table_kernel, 5, 4, len(hosted)),
        grid=(t // tm,),
        in_specs=[pl.BlockSpec((tm, 1), lambda i: (i, 0)), row, row, row, row] + h_in,
        out_specs=[out, out, out, out] + h_out,
        out_shape=[jax.ShapeDtypeStruct((t, LANES), F32)] * 4 + h_shape,
        compiler_params=_cparams(("arbitrary",)),
        name="rope_tables",
    )(pos, inv_a, inv_b, sgn_a, sgn_b, *h_args)


def _ffn_kernel(xf_ref, wg_ref, wu_ref, wd_ref, g_ref, b_ref, of_ref, ob_ref, acc_ref, *, nf):
    j = pl.program_id(1)

    @pl.when(j == 0)
    def _():
        acc_ref[...] = jnp.zeros_like(acc_ref)

    x = xf_ref[...].astype(BF16)
    gt = _dot(x, wg_ref[0])
    ut = _dot(x, wu_ref[0])
    h = (gt * jax.nn.sigmoid(gt) * ut).astype(BF16)
    acc_ref[...] += _dot(h, wd_ref[0])

    @pl.when(j == nf - 1)
    def _():
        _emit_stream(of_ref, ob_ref, DEEPNORM_ALPHA * xf_ref[...] + 0.5 * acc_ref[...], g_ref, b_ref)


def _ffn_tiles(t, f):
    tm = 512 if t % 512 == 0 else t
    tf = 512 if f % 512 == 0 else f
    return tm, tf


def _ffn(xf, w, g, b):
    wg, wu, wd = w
    t, d = xf.shape
    f = wg.shape[2]
    tm, tf = _ffn_tiles(t, f)
    nf = f // tf
    row = pl.BlockSpec((tm, d), lambda i, j: (i, 0))
    vec = pl.BlockSpec((1, d), lambda i, j: (0, 0))
    return pl.pallas_call(
        functools.partial(_ffn_kernel, nf=nf),
        grid=(t // tm, nf),
        in_specs=[row,
                  pl.BlockSpec((1, d, tf), lambda i, j: (0, 0, j)),
                  pl.BlockSpec((1, d, tf), lambda i, j: (0, 0, j)),
                  pl.BlockSpec((1, tf, d), lambda i, j: (0, j, 0)),
                  vec, vec],
        out_specs=[row, row],
        out_shape=[jax.ShapeDtypeStruct((t, d), F32), jax.ShapeDtypeStruct((t, d), BF16)],
        scratch_shapes=[pltpu.VMEM((tm, d), F32)],
        compiler_params=_cparams(("parallel", "arbitrary")),
        name="ffn",
    )(xf, wg, wu, wd, g, b)


def _proj_rope_kernel(xb_ref, w_ref, cos_ref, sin_ref, p_ref, *, rot_dim, q_scale, q_tiles):
    j = pl.program_id(1)
    scale = jnp.where(j < q_tiles, q_scale, 1.0).astype(F32)
    half = rot_dim // 2
    tm = xb_ref.shape[0]
    rc = PROJ_ROW_CHUNK if tm % PROJ_ROW_CHUNK == 0 else tm
    for r0 in range(0, tm, rc):
        rows = slice(r0, r0 + rc)
        acc = _dot_nt(xb_ref[rows, :], w_ref[0])
        cos, sin = cos_ref[rows, :], sin_ref[rows, :]
        for hh in range(PROJ_TN // LANES):
            t = acc[:, _head_cols(hh)]
            if rot_dim == LANES:
                rot = pltpu.roll(t, half, 1)
            else:
                lane = lax.broadcasted_iota(jnp.int32, t.shape, 1)
                rot = jnp.where((lane // half) % 2 == 0, pltpu.roll(t, LANES - half, 1), pltpu.roll(t, half, 1))
            p_ref[rows, _head_cols(hh)] = ((t * cos + rot * sin) * scale).astype(BF16)


def _proj_plain_kernel(xb_ref, w_ref, p_ref, *, gq_cols):
    j = pl.program_id(1)
    col = j * PROJ_TN + lax.broadcasted_iota(jnp.int32, (1, PROJ_TN), 1)
    scale = jnp.where((col >= gq_cols[0]) & (col < gq_cols[1]), GLA_DK ** -0.5, 1.0).astype(F32)
    tm = xb_ref.shape[0]
    rc = PROJ_ROW_CHUNK if tm % PROJ_ROW_CHUNK == 0 else tm
    for r0 in range(0, tm, rc):
        rows = slice(r0, r0 + rc)
        p_ref[rows, :] = (_dot_nt(xb_ref[rows, :], w_ref[0]) * scale).astype(BF16)


def _proj_row_tile(t):
    return 2048 if t % 2048 == 0 else t


def _proj_plain_steps(t):
    return (t // _proj_row_tile(t)) * ((N_COL_BLOCKS - COL_DV) * LANES // PROJ_TN)


def _proj(xb, w, layer, tables, hosted=()):
    t, d = xb.shape
    tm = _proj_row_tile(t)

    def tiles(c):
        assert (c * LANES) % PROJ_TN == 0
        return c * LANES // PROJ_TN

    cos_a, sin_a, cos_b, sin_b = tables
    x_spec = pl.BlockSpec((tm, d), lambda i, j: (i, 0))
    tab = pl.BlockSpec((tm, LANES), lambda i, j: (i, 0))
    out = pl.BlockSpec((tm, PROJ_TN), lambda i, j: (i, j))
    w_spec = lambda first: pl.BlockSpec((1, PROJ_TN, d), lambda i, j: (layer, first + j, 0))

    def rope_group(first_col, q_heads, n_heads, rot_dim, q_scale, cos, sin, name):
        nj = tiles(n_heads)
        return pl.pallas_call(
            functools.partial(_proj_rope_kernel, rot_dim=rot_dim, q_scale=q_scale, q_tiles=tiles(q_heads)),
            grid=(t // tm, nj),
            in_specs=[x_spec, w_spec(tiles(first_col)), tab, tab],
            out_specs=out,
            out_shape=jax.ShapeDtypeStruct((t, nj * PROJ_TN), BF16),
            compiler_params=_cparams(("parallel", "arbitrary")),
            name=name,
        )(xb, w, cos, sin)

    pa = rope_group(COL_DQ, DIFF_HEADS, 2 * DIFF_HEADS, DIFF_QK_DIM, DIFF_QK_DIM ** -0.5 * LOG2_E, cos_a, sin_a,
                    "in_proj_diff")
    pb = rope_group(COL_MQ, MOBA_HEADS, 2 * MOBA_HEADS, HEAD_DIM, HEAD_DIM ** -0.5 * LOG2_E, cos_b, sin_b,
                    "in_proj_moba")
    nj = tiles(N_COL_BLOCKS - COL_DV)
    h_in, h_out, h_shape, h_args = _hosted_specs(hosted, nj)
    assert (t // tm) * nj >= max([n for _, _, n in hosted], default=0)
    pc, *side = pl.pallas_call(
        _with_hosted_casts(functools.partial(_proj_plain_kernel, gq_cols=((COL_GQ - COL_DV) * LANES,
                                                                           (COL_GK - COL_DV) * LANES)),
                           2, 1, len(hosted)),
        grid=(t // tm, nj),
        in_specs=[x_spec, w_spec(tiles(COL_DV))] + h_in,
        out_specs=[out] + h_out,
        out_shape=[jax.ShapeDtypeStruct((t, nj * PROJ_TN), BF16)] + h_shape,
        compiler_params=_cparams(("arbitrary", "arbitrary")),
        name="in_proj_plain",
    )(xb, w, *h_args)
    return pa, pb, pc, side


_IN_WIDTHS = (DIFF_HEADS * 128, DIFF_HEADS * 128, DIFF_HEADS * 128, GLA_HEADS * GLA_DK, GLA_HEADS * GLA_DK,
              GLA_HEADS * GLA_DV, GLA_HEADS * GLA_DV, GLA_GATE_RANK, MOBA_HEADS * 128, MOBA_HEADS * 128,
              MOBA_HEADS * 128)
_IN_OFFS = tuple(int(v) for v in np.concatenate([[0], np.cumsum(_IN_WIDTHS)]))
D_IN = _IN_OFFS[-1]


def _reorder_w_in_kernel(w_ref, o_ref):
    src = dict(zip(("dq", "dk", "dv", "gq", "gk", "gv", "gr", "gg", "mq", "mk", "mv"), zip(_IN_OFFS[:-1], _IN_WIDTHS)))
    for name, col in (("dq", COL_DQ), ("dk", COL_DK), ("mq", COL_MQ), ("mk", COL_MK), ("dv", COL_DV), ("mv", COL_MV),
                      ("gq", COL_GQ), ("gk", COL_GK), ("gv", COL_GV), ("gr", COL_GR)):
        s0, width = src[name]
        o_ref[0, col * LANES:col * LANES + width, :] = w_ref[0, s0:s0 + width, :].astype(BF16)


def _reorder_w_in(w_in):
    depth, d, d_in = w_in.shape
    assert d_in == D_IN and all(o % 16 == 0 for o in _IN_OFFS)
    wt = jnp.swapaxes(w_in, 1, 2)
    tc = 128 if d % 128 == 0 else d
    return pl.pallas_call(
        _reorder_w_in_kernel,
        grid=(depth, d // tc),
        in_specs=[pl.BlockSpec((1, d_in, tc), lambda l, c: (l, 0, c))],
        out_specs=pl.BlockSpec((1, PROJ_WIDTH, tc), lambda l, c: (l, 0, c)),
        out_shape=jax.ShapeDtypeStruct((depth, PROJ_WIDTH, d), BF16),
        compiler_params=_cparams(("parallel", "parallel")),
        name="reorder_w_in",
    )(wt)


def _cast_kernel(x_ref, o_ref):
    o_ref[...] = x_ref[...].astype(o_ref.dtype)


def _cast_bf16(x):
    l, r, c = x.shape
    tr = next((t for t in (1024, 512, 256, 128) if r % t == 0 and t * c * 4 <= (4 << 20)), r)
    spec = pl.BlockSpec((1, tr, c), lambda i, j: (i, j, 0))
    return pl.pallas_call(
        _cast_kernel,
        grid=(l, r // tr),
        in_specs=[spec],
        out_specs=spec,
        out_shape=jax.ShapeDtypeStruct(x.shape, BF16),
        compiler_params=_cparams(("parallel", "parallel")),
        name="cast_bf16",
    )(x)


def _store_v_transposed(v_ref, vt_ref, tk):
    seq = v_ref.shape[0]
    for blk in range(seq // tk):
        vt_ref[0:HEAD_DIM, blk * tk:(blk + 1) * tk] = v_ref[blk * tk:(blk + 1) * tk, :].astype(F32).T.astype(BF16)
    vt_ref[HEAD_DIM:V_ROWS, :] = jnp.ones((V_ROWS - HEAD_DIM, seq), BF16)


def _run_tiles(n, scores, finish):
    pending = [scores(t) for t in range(min(ATTN_LOOKAHEAD, n))]
    for t in range(n):
        if t + ATTN_LOOKAHEAD < n:
            pending.append(scores(t + ATTN_LOOKAHEAD))
        finish(t, pending.pop(0))


def _col_max(parts):
    m = jnp.max(parts[0], axis=0, keepdims=True)
    for s in parts[1:]:
        m = jnp.maximum(m, jnp.max(s, axis=0, keepdims=True))
    return m


def _diff_kernel(lam_ref, g_ref, q_ref, k_ref, v_ref, o_ref, vt_ref, *, tq, lam_init):
    seq = k_ref.shape[0]
    _store_v_transposed(v_ref, vt_ref, tq)
    lf = lam_ref[...]
    lam = (jnp.exp(jnp.sum(lf[0:1] * lf[1:2], axis=1, keepdims=True))
           - jnp.exp(jnp.sum(lf[2:3] * lf[3:4], axis=1, keepdims=True)) + lam_init)
    lane = lax.broadcasted_iota(jnp.int32, (tq, LANES), 1)
    causal = (lax.broadcasted_iota(jnp.int32, (tq, 2 * tq), 0)
              <= lax.broadcasted_iota(jnp.int32, (tq, 2 * tq), 1) % tq)
    def scores(qi):
        lo, hi = qi * tq, (qi + 1) * tq
        q = q_ref[lo:hi, :].astype(F32)
        qq = jnp.concatenate([jnp.where(lane < DIFF_QK_DIM, q, 0.0),
                              jnp.where(lane >= DIFF_QK_DIM, q, 0.0)], axis=0).astype(BF16)
        parts = [jnp.where(causal, _dot_nt(k_ref[lo:hi, :], qq), NEG_INF)]
        if qi > 0:
            parts.append(_dot_nt(k_ref[0:lo, :], qq))
        return parts

    def finish(qi, parts):
        lo, hi = qi * tq, (qi + 1) * tq
        m = _col_max(parts)
        acc = _dot(vt_ref[:, lo:hi], jnp.exp2(parts[0] - m).astype(BF16))
        if qi > 0:
            acc = acc + _dot(vt_ref[:, 0:lo], jnp.exp2(parts[1] - m).astype(BF16))
        o_t = acc[0:HEAD_DIM] / acc[HEAD_DIM:HEAD_DIM + 1]
        o = (o_t[:, :tq] - lam * o_t[:, tq:]).T
        o = o * lax.rsqrt(jnp.mean(o * o, axis=-1, keepdims=True) + LN_EPS) * g_ref[...] * (1.0 - lam_init)
        o_ref[lo:hi, :] = o.astype(BF16)

    _run_tiles(seq // tq, scores, finish)


def _with_hosted_casts(kernel_fn, n_in, n_out, n_hosted):
    def wrapped(*refs):
        ins, rest = refs[:n_in], refs[n_in:]
        srcs, rest = rest[:n_hosted], rest[n_hosted:]
        outs, rest = rest[:n_out], rest[n_out:]
        dsts, scratch = rest[:n_hosted], rest[n_hosted:]
        for src, dst in zip(srcs, dsts):
            dst[...] = src[...].astype(BF16)
        kernel_fn(*ins, *outs, *scratch)
    return wrapped


def _hosted_specs(hosted, n_inner=None):
    step = (lambda i: i) if n_inner is None else (lambda b, h: b * n_inner + h)
    ins, outs, shapes = [], [], []
    for arr, layer, nblk in hosted:
        _, r, c = arr.shape
        assert r % nblk == 0
        blk = lambda *g, nblk=nblk: jnp.minimum(step(*g), nblk - 1)
        ins.append(pl.BlockSpec((1, r // nblk, c), lambda *g, layer=layer, blk=blk: (layer, blk(*g), 0)))
        outs.append(pl.BlockSpec((1, r // nblk, c), lambda *g, blk=blk: (0, blk(*g), 0)))
        shapes.append(jax.ShapeDtypeStruct((1, r, c), BF16))
    return ins, outs, shapes, [arr for arr, _, _ in hosted]


def _diff_attention(pa, pc, lam, g, batch, seq, layer, hosted=()):
    tq = 256
    lam_init = 0.8 - 0.6 * math.exp(-0.3 * layer)
    col = lambda c0: pl.BlockSpec((seq, LANES), lambda b, h: (b, c0 + h))
    h_in, h_out, h_shape, h_args = _hosted_specs(hosted, DIFF_HEADS)
    assert batch * DIFF_HEADS >= max([n for _, _, n in hosted], default=0)
    return pl.pallas_call(
        _with_hosted_casts(functools.partial(_diff_kernel, tq=tq, lam_init=lam_init), 5, 1, len(hosted)),
        grid=(batch, DIFF_HEADS),
        in_specs=[pl.BlockSpec((4, DIFF_QK_DIM), lambda b, h: (0, 0)),
                  pl.BlockSpec((1, LANES), lambda b, h: (0, 0)),
                  col(0), col(DIFF_HEADS), col(0)] + h_in,
        out_specs=[pl.BlockSpec((seq, LANES), lambda b, h: (b, h))] + h_out,
        out_shape=[jax.ShapeDtypeStruct((batch * seq, DIFF_HEADS * LANES), BF16)] + h_shape,
        scratch_shapes=[pltpu.VMEM((V_ROWS, seq), BF16)],
        compiler_params=_cparams(("arbitrary", "arbitrary")),
        name="diff_attn",
    )(lam, g, pa, pa, pc, *h_args)


def _moba_kernel(q_ref, k_ref, v_ref, o_ref, vt_ref, *, nb):
    tq = MOBA_BLOCK
    _store_v_transposed(v_ref, vt_ref, tq)
    nbp = -(-nb // SUBLANES) * SUBLANES
    rows = [jnp.mean(k_ref[n * tq:(n + 1) * tq, :].astype(F32), axis=0, keepdims=True) for n in range(nb)]
    if nbp > nb:
        rows.append(jnp.zeros((nbp - nb, LANES), F32))
    km = jnp.concatenate(rows, axis=0)
    km_hi = km.astype(BF16)
    km_lo = (km - km_hi.astype(F32)).astype(BF16)
    causal = (lax.broadcasted_iota(jnp.int32, (tq, tq), 0) <= lax.broadcasted_iota(jnp.int32, (tq, tq), 1))
    def scores(qi):
        lo, hi = qi * tq, (qi + 1) * tq
        q = q_ref[lo:hi, :]
        parts = [jnp.where(causal, _dot_nt(k_ref[lo:hi, :], q), NEG_INF)]
        if qi > 0:
            s_past = _dot_nt(k_ref[0:lo, :], q)
            if qi > MOBA_TOPK:
                gate = _dot_nt(km_hi, q) + _dot_nt(km_lo, q)
                n_idx = lax.broadcasted_iota(jnp.int32, gate.shape, 0)
                rank = jnp.zeros(gate.shape, jnp.int32)
                for mb in range(qi):
                    gm = gate[mb:mb + 1, :]
                    rank = rank + jnp.where((gm > gate) | ((gm == gate) & (mb < n_idx)), 1, 0)
                bias = jnp.where(rank < MOBA_TOPK, 0.0, NEG_INF)
                s_past = jnp.concatenate([s_past[n * tq:(n + 1) * tq] + bias[n:n + 1, :] for n in range(qi)], axis=0)
            parts.append(s_past)
        return parts

    def finish(qi, parts):
        lo, hi = qi * tq, (qi + 1) * tq
        m = _col_max(parts)
        acc = _dot(vt_ref[:, lo:hi], jnp.exp2(parts[0] - m).astype(BF16))
        if qi > 0:
            acc = acc + _dot(vt_ref[:, 0:lo], jnp.exp2(parts[1] - m).astype(BF16))
        o_ref[lo:hi, :] = (acc[0:HEAD_DIM] / acc[HEAD_DIM:HEAD_DIM + 1]).T.astype(BF16)

    _run_tiles(nb, scores, finish)


def _moba_attention(pb, pc, batch, seq, hosted=()):
    assert seq % MOBA_BLOCK == 0
    col = lambda c0: pl.BlockSpec((seq, LANES), lambda b, h: (b, c0 + h))
    h_in, h_out, h_shape, h_args = _hosted_specs(hosted, MOBA_HEADS)
    assert batch * MOBA_HEADS >= max([n for _, _, n in hosted], default=0)
    return pl.pallas_call(
        _with_hosted_casts(functools.partial(_moba_kernel, nb=seq // MOBA_BLOCK), 3, 1, len(hosted)),
        grid=(batch, MOBA_HEADS),
        in_specs=[col(0), col(MOBA_HEADS), col(COL_MV - COL_DV)] + h_in,
        out_specs=[pl.BlockSpec((seq, LANES), lambda b, h: (b, h))] + h_out,
        out_shape=[jax.ShapeDtypeStruct((batch * seq, MOBA_HEADS * LANES), BF16)] + h_shape,
        scratch_shapes=[pltpu.VMEM((V_ROWS, seq), BF16)],
        compiler_params=_cparams(("arbitrary", "arbitrary")),
        name="moba_attn",
    )(pb, pb, pc, *h_args)


def _gla_constants():
    c = GLA_CHUNK
    idx = np.arange(c)
    mats = [np.tril(np.ones((c, c), np.float32))]
    bm = []
    for lvl in range(6):
        m = 32 >> lvl
        parent, half = idx // (2 * m), (idx // m) % 2
        ref = parent * 2 * m + m - 1
        t = idx[None, :]
        second = (half == 1)[:, None] & (t > ref[:, None]) & (t <= idx[:, None])
        first = (half == 0)[:, None] & (t > idx[:, None]) & (t <= ref[:, None])
        mats.append((second | first).astype(np.float32))
        bm.append(((parent[:, None] == parent[None, :]) & (half == 1)[:, None] & (half == 0)[None, :]).astype(np.float32))
    bm.append(np.eye(c, dtype=np.float32))
    cmat = np.concatenate(mats, axis=0)
    return jnp.asarray(cmat, BF16), jnp.asarray(np.concatenate(bm, axis=0), F32)


def _split2(x):
    hi = x.astype(BF16)
    return hi, (x - hi.astype(F32)).astype(BF16)


def _gla_kernel(cmat_ref, bm_ref, wgg_ref, up_ref, bias_ref, gn_ref, x_ref, q_ref, k_ref, v_ref, r_ref, o_ref,
                la_ref, st_ref, *, seq, heads):
    c = GLA_CHUNK
    wg = wgg_ref[...].astype(BF16)
    up_hi, up_lo = _split2(up_ref[...])
    rows = 512 if seq % 512 == 0 else seq
    for r0 in range(0, seq, rows):
        g_hi, g_lo = _split2(_dot(x_ref[r0:r0 + rows, :], wg))
        z = _dot(g_hi, up_hi) + _dot(g_hi, up_lo) + _dot(g_lo, up_hi) + bias_ref[...]
        la_ref[r0:r0 + rows, :] = (jnp.minimum(z, 0.0) - jnp.log(1.0 + jnp.exp(-jnp.abs(z)))) * (1.0 / GLA_TAU)

    st_ref[...] = jnp.zeros_like(st_ref)
    cmat = cmat_ref[...]
    lane = lax.broadcasted_iota(jnp.int32, (c, LANES), 1)
    own = [lane < GLA_DK, lane >= GLA_DK]

    def halves(t):
        return [jnp.where(m, t, 0.0).astype(BF16) for m in own]

    def chunk_group(cp, carry):
        sls = [pl.ds(pl.multiple_of((cp * GLA_UNROLL + u) * c, c), c) for u in range(GLA_UNROLL)]
        e_alls = []
        for sl in sls:
            g_hi, g_lo = _split2(la_ref[sl, :])
            e_alls.append(_dot(cmat, g_hi) + _dot(cmat, g_lo))
        ids = [(sl, p, e_all[:, _head_cols(p)]) for sl, e_all in zip(sls, e_alls) for p in range(heads // 2)]
        qs = [q_ref[sl, _head_cols(p)].astype(F32) for sl, p, _ in ids]
        ks = [k_ref[sl, _head_cols(p)].astype(F32) for sl, p, _ in ids]
        atts = []
        for q, k in zip(qs, ks):
            kb = k.astype(BF16)
            atts.append([bm_ref[6 * c:7 * c, :] * _dot_nt(qh, kb) for qh in halves(q)])
        for lvl in range(6):
            for i, (_, _, e) in enumerate(ids):
                w = jnp.exp(e[(1 + lvl) * c:(2 + lvl) * c])
                kw = (ks[i] * w).astype(BF16)
                for hf, qh in enumerate(halves(qs[i] * w)):
                    atts[i][hf] = atts[i][hf] + bm_ref[lvl * c:(lvl + 1) * c, :] * _dot_nt(qh, kw)
        streams = []
        for (sl, p, e), q, k, att in zip(ids, qs, ks, atts):
            cum = e[0:c]
            last = cum[c - 1:c, :]
            qes, kds = halves(q * jnp.exp(cum)), halves(k * jnp.exp(last - cum))
            for hf in range(2):
                h = 2 * p + hf
                v = v_ref[sl, _head_cols(h)]
                o_intra = _dot(att[hf].astype(BF16), v)
                upd = _dot(v.astype(F32).T.astype(BF16), kds[hf])
                streams.append((sl, h, qes[hf], o_intra, upd, jnp.exp(last)))
        for args in streams:
            recur(*args)
        return carry

    def recur(sl, h, qe, o_intra, upd, decay):
        hc = _head_cols(h)
        st = st_ref[h]
        o = _dot_nt(qe, st.astype(BF16)) + o_intra
        st_ref[h] = st * decay + upd
        o = o * lax.rsqrt(jnp.mean(o * o, axis=-1, keepdims=True) + LN_EPS) * gn_ref[...]
        r = r_ref[sl, hc].astype(F32)
        o_ref[sl, hc] = (o * (r * jax.nn.sigmoid(r))).astype(BF16)

    assert (seq // c) % GLA_UNROLL == 0 and heads % 2 == 0
    lax.fori_loop(0, seq // c // GLA_UNROLL, chunk_group, 0)


def _gla(xb, pc, w_gg, up, bias, gn, batch, seq):
    cmat, bm = _gla_constants()
    heads = GLA_HEADS
    d = xb.shape[1]
    qk_w, v_w = heads * GLA_DK, heads * GLA_DV
    wgg_p = jnp.pad(w_gg, ((0, 0), (0, GG_PAD - GLA_GATE_RANK)))
    up_p = jnp.pad(up, ((0, GG_PAD - GLA_GATE_RANK), (0, 0)))
    const = lambda shape: pl.BlockSpec(shape, lambda b: (0, 0))
    col = lambda c0, width: pl.BlockSpec((seq, width), lambda b: (b, (c0 - COL_DV) * LANES // width))
    return pl.pallas_call(
        functools.partial(_gla_kernel, seq=seq, heads=heads),
        grid=(batch,),
        in_specs=[const(cmat.shape), const(bm.shape), const((d, GG_PAD)), const((GG_PAD, qk_w)), const((1, qk_w)),
                  const((1, LANES)), pl.BlockSpec((seq, d), lambda b: (b, 0)),
                  col(COL_GQ, qk_w), col(COL_GK, qk_w), col(COL_GV, v_w), col(COL_GR, v_w)],
        out_specs=pl.BlockSpec((seq, v_w), lambda b: (b, 0)),
        out_shape=jax.ShapeDtypeStruct((batch * seq, v_w), BF16),
        scratch_shapes=[pltpu.VMEM((seq, qk_w), F32), pltpu.VMEM((heads, GLA_DV, LANES), F32)],
        compiler_params=_cparams(("parallel",)),
        name="gla",
    )(cmat, bm, wgg_p, up_p, bias.reshape(1, qk_w), gn, xb, pc, pc, pc, pc)


def _emit_stream(of_ref, ob_ref, y, g_ref, b_ref, rows=slice(None)):
    out = _layer_norm(y, g_ref[...], b_ref[...])
    of_ref[rows, :] = out
    if ob_ref is not None:
        ob_ref[rows, :] = out.astype(BF16)


def _row_chunks(tm):
    rc = STREAM_ROW_CHUNK if tm % STREAM_ROW_CHUNK == 0 else tm
    return [slice(r0, r0 + rc) for r0 in range(0, tm, rc)]


def _out_kernel(od_ref, og_ref, om_ref, w_ref, xf_ref, g_ref, b_ref, of_ref):
    ob_ref = None
    k0 = od_ref.shape[1]
    k1 = k0 + og_ref.shape[1]
    for rows in _row_chunks(xf_ref.shape[0]):
        y = (_dot(od_ref[rows, :], w_ref[0, 0:k0, :]) + _dot(og_ref[rows, :], w_ref[0, k0:k1, :])
             + _dot(om_ref[rows, :], w_ref[0, k1:, :]))
        _emit_stream(of_ref, ob_ref, DEEPNORM_ALPHA * xf_ref[rows, :] + y, g_ref, b_ref, rows)


def _row_tile(t):
    return 512 if t % 512 == 0 else t


def _out_proj(o_diff, o_gla, o_moba, w, layer, xf, g, b):
    t, d = xf.shape
    tm = _row_tile(t)
    row = pl.BlockSpec((tm, d), lambda i: (i, 0))
    vec = pl.BlockSpec((1, d), lambda i: (0, 0))
    part = lambda a: pl.BlockSpec((tm, a.shape[1]), lambda i: (i, 0))
    return pl.pallas_call(
        _out_kernel,
        grid=(t // tm,),
        in_specs=[part(o_diff), part(o_gla), part(o_moba),
                  pl.BlockSpec((1,) + w.shape[1:], lambda i: (layer, 0, 0)),
                  row, vec, vec],
        out_specs=row,
        out_shape=jax.ShapeDtypeStruct((t, d), F32),
        compiler_params=_cparams(("parallel",)),
        name="out_proj",
    )(o_diff, o_gla, o_moba, w, xf, g, b)


def _ple_kernel(xb_ref, wg_ref, pb_ref, we_ref, xf_ref, g_ref, b_ref, of_ref):
    ob_ref = None
    for rows in _row_chunks(xf_ref.shape[0]):
        e = _dot(pb_ref[0, rows, :], we_ref[0]) * jax.nn.sigmoid(_dot(xb_ref[rows, :], wg_ref[0]))
        _emit_stream(of_ref, ob_ref, DEEPNORM_ALPHA * xf_ref[rows, :] + e, g_ref, b_ref, rows)


def _ple(xf, xb, pb, w_pe, pe_layer, w_pg, g, b):
    t, d = xf.shape
    tm = _row_tile(t)
    row = pl.BlockSpec((tm, d), lambda i: (i, 0))
    vec = pl.BlockSpec((1, d), lambda i: (0, 0))
    whole = lambda a, layer: pl.BlockSpec((1,) + a.shape[1:], lambda i: (layer, 0, 0))
    return pl.pallas_call(
        _ple_kernel,
        grid=(t // tm,),
        in_specs=[row, whole(w_pg, 0),
                  pl.BlockSpec((1, tm, pb.shape[2]), lambda i: (0, i, 0)),
                  whole(w_pe, pe_layer), row, vec, vec],
        out_specs=row,
        out_shape=jax.ShapeDtypeStruct((t, d), F32),
        compiler_params=_cparams(("parallel",)),
        name="ple",
    )(xb, w_pg, pb, w_pe, xf, g, b)


def kernel(x, p, positions, w_in, w_out, diff_lambda, diff_norm_g, gla_gate_up, gla_gate_b, gla_norm_g,
           ffn1_gate, ffn1_up, ffn1_down, ffn2_gate, ffn2_up, ffn2_down, w_pe, w_pg, ln_g, ln_b):
    batch, seq, d = x.shape
    t = batch * seq
    assert d == (DIFF_HEADS + GLA_HEADS + MOBA_HEADS) * HEAD_DIM and seq % MOBA_BLOCK == 0
    xf = x.reshape(t, d)
    rope_blocks = max(n for n in (1, 2, 4, 8, 16) if n <= t // MOBA_BLOCK)
    *tables, w_g, w_u, w_d = _rope_tables(positions.reshape(t, 1), steps=rope_blocks,
                                          hosted=[(a, 0, rope_blocks) for a in (ffn1_gate, ffn1_up, ffn1_down)])
    w_ffn1 = (w_g, w_u, w_d)
    vec = lambda v: v.reshape(1, -1)
    w_in_b, w_pe_b = _reorder_w_in(w_in), _cast_bf16(w_pe)
    p3 = p.reshape(DEPTH, t, -1)
    side_blocks = max(n for n in (1, 2, 4, 8, 16) if n <= batch * min(DIFF_HEADS, MOBA_HEADS))
    side = lambda arrs, layer: [(a, layer, side_blocks) for a in arrs]
    for i in range(DEPTH):
        more = i + 1 < DEPTH
        xf, xb = _ffn(xf, w_ffn1, vec(ln_g[i, 0]), vec(ln_b[i, 0]))
        pa, pb, pc, w_gu2 = _proj(xb, w_in_b, i, tables,
                                  hosted=[(a, i, min(side_blocks, _proj_plain_steps(t))) for a in (ffn2_gate, ffn2_up)])
        o_diff, w_out_b, p_b, w_d2, *w_down_next = _diff_attention(
            pa, pc, diff_lambda[i], vec(diff_norm_g[i]), batch, seq, i,
            hosted=side((w_out, p3, ffn2_down), i) + (side((ffn1_down,), i + 1) if more else []))
        w_ffn2 = tuple(w_gu2) + (w_d2,)
        gg0 = _IN_OFFS[7]
        o_gla = _gla(xb, pc, w_in[i, :, gg0:gg0 + GLA_GATE_RANK], gla_gate_up[i], gla_gate_b[i],
                     vec(gla_norm_g[i]), batch, seq)
        o_moba, w_pg_b, *w_gu_next = _moba_attention(
            pb, pc, batch, seq, hosted=side((w_pg,), i) + (side((ffn1_gate, ffn1_up), i + 1) if more else []))
        xf = _out_proj(o_diff, o_gla, o_moba, w_out_b, 0, xf, vec(ln_g[i, 1]), vec(ln_b[i, 1]))
        xf, xb = _ffn(xf, w_ffn2, vec(ln_g[i, 2]), vec(ln_b[i, 2]))
        xf = _ple(xf, xb, p_b, w_pe_b, i, w_pg_b, vec(ln_g[i, 3]), vec(ln_b[i, 3]))
        w_ffn1 = tuple(w_gu_next) + tuple(w_down_next)
    return xf.reshape(batch, seq, d)
```

```python
import functools
import math

import numpy as np
import jax
import jax.numpy as jnp
from jax import lax
from jax.experimental import pallas as pl
from jax.experimental.pallas import tpu as pltpu

F32 = jnp.float32
BF16 = jnp.bfloat16

DEPTH = 2
HEAD_DIM = 128
DIFF_HEADS = 6
GLA_HEADS = 4
MOBA_HEADS = 6
DIFF_QK_DIM = 64
GLA_DK = 64
GLA_DV = 128
GLA_GATE_RANK = 16
GLA_TAU = 16.0
GLA_CHUNK = 64
GLA_UNROLL = 4
MOBA_BLOCK = 256
MOBA_TOPK = 3
ROPE_THETA = 10000.0
LN_EPS = 1e-5
DEEPNORM_ALPHA = (2 * DEPTH) ** 0.25

LANES = 128
SUBLANES = 8
VMEM_LIMIT_BYTES = 56 * 1024 * 1024

PROJ_TN = 768
PROJ_ROW_CHUNK = 256
STREAM_ROW_CHUNK = 128
_CB = LANES
COL_DQ = 0
COL_DK = COL_DQ + DIFF_HEADS
COL_MQ = COL_DK + DIFF_HEADS
COL_MK = COL_MQ + MOBA_HEADS
COL_DV = COL_MK + MOBA_HEADS
COL_MV = COL_DV + DIFF_HEADS
COL_GQ = COL_MV + MOBA_HEADS
COL_GK = COL_GQ + GLA_HEADS * GLA_DK // LANES
COL_GV = COL_GK + GLA_HEADS * GLA_DK // LANES
COL_GR = COL_GV + GLA_HEADS
N_COL_BLOCKS = COL_GR + GLA_HEADS
PROJ_WIDTH = N_COL_BLOCKS * _CB
assert PROJ_WIDTH % PROJ_TN == 0
GG_PAD = LANES

NEG_INF = float("-inf")
LOG2_E = math.log2(math.e)
ATTN_LOOKAHEAD = 2
V_ROWS = HEAD_DIM + 16


def _cparams(sem):
    return pltpu.CompilerParams(dimension_semantics=sem, vmem_limit_bytes=VMEM_LIMIT_BYTES)


def _layer_norm(y, g, b):
    mu = jnp.mean(y, axis=-1, keepdims=True)
    d = y - mu
    var = jnp.mean(d * d, axis=-1, keepdims=True)
    return d * lax.rsqrt(var + LN_EPS) * g + b


def _dot_nt(a, b):
    return lax.dot_general(a, b, (((1,), (1,)), ((), ())), preferred_element_type=F32)


def _dot(a, b):
    return jnp.dot(a, b, preferred_element_type=F32)


def _head_cols(g):
    return slice(g * LANES, (g + 1) * LANES)


def _rope_table_kernel(pos_ref, inv_a_ref, inv_b_ref, sgn_a_ref, sgn_b_ref,
                       cos_a_ref, sin_a_ref, cos_b_ref, sin_b_ref):
    pos = pos_ref[...].astype(F32)
    ang_a = pos * inv_a_ref[...]
    ang_b = pos * inv_b_ref[...]
    cos_a_ref[...] = jnp.cos(ang_a)
    sin_a_ref[...] = jnp.sin(ang_a) * sgn_a_ref[...]
    cos_b_ref[...] = jnp.cos(ang_b)
    sin_b_ref[...] = jnp.sin(ang_b) * sgn_b_ref[...]


def _rope_tables(pos, hosted=(), steps=None):
    t = pos.shape[0]
    tm = next(c for c in (1024, 512, MOBA_BLOCK) if t % c == 0 and (steps is None or t // c >= steps))

    def pattern(d):
        inv = ROPE_THETA ** (-jnp.arange(0, d, 2, dtype=F32) / d)
        inv = jnp.tile(inv, 2 * LANES // d)[None, :]
        sgn = jnp.tile(jnp.concatenate([-jnp.ones(d // 2, F32), jnp.ones(d // 2, F32)]), LANES // d)[None, :]
        return inv, sgn

    inv_a, sgn_a = pattern(DIFF_QK_DIM)
    inv_b, sgn_b = pattern(HEAD_DIM)
    row = pl.BlockSpec((1, LANES), lambda i: (0, 0))
    out = pl.BlockSpec((tm, LANES), lambda i: (i, 0))
    h_in, h_out, h_shape, h_args = _hosted_specs(hosted)
    assert t // tm >= max([n for _, _, n in hosted], default=0)
    return pl.pallas_call(
        _with_hosted_casts(_rope_table_kernel, 5, 4, len(hosted)),
        grid=(t // tm,),
        in_specs=[pl.BlockSpec((tm, 1), lambda i: (i, 0)), row, row, row, row] + h_in,
        out_specs=[out, out, out, out] + h_out,
        out_shape=[jax.ShapeDtypeStruct((t, LANES), F32)] * 4 + h_shape,
        compiler_params=_cparams(("arbitrary",)),
        name="rope_tables",
    )(pos, inv_a, inv_b, sgn_a, sgn_b, *h_args)


def _ffn_kernel(*refs, nf, with_reorder):
    xf_ref, wg_ref, wu_ref, wd_ref, g_ref, b_ref = refs[:6]
    of_ref, ob_ref = refs[6 + with_reorder:8 + with_reorder]
    acc_ref = refs[-1]
    j = pl.program_id(1)

    @pl.when(j == 0)
    def _():
        acc_ref[...] = jnp.zeros_like(acc_ref)
        if with_reorder:
            _reorder_w_in_kernel(refs[6], refs[9])

    x = xf_ref[...].astype(BF16)
    gt = _dot(x, wg_ref[0])
    ut = _dot(x, wu_ref[0])
    h = (gt * jax.nn.sigmoid(gt) * ut).astype(BF16)
    acc_ref[...] += _dot(h, wd_ref[0])

    @pl.when(j == nf - 1)
    def _():
        _emit_stream(of_ref, ob_ref, DEEPNORM_ALPHA * xf_ref[...] + 0.5 * acc_ref[...], g_ref, b_ref)


def _ffn_tiles(t, f):
    tm = 512 if t % 512 == 0 else t
    tf = 512 if f % 512 == 0 else f
    return tm, tf


def _ffn(xf, w, g, b, reorder=None):
    wg, wu, wd = w
    t, d = xf.shape
    f = wg.shape[2]
    tm, tf = _ffn_tiles(t, f)
    nf, ni = f // tf, t // tm
    row = pl.BlockSpec((tm, d), lambda i, j: (i, 0))
    vec = pl.BlockSpec((1, d), lambda i, j: (0, 0))
    in_specs = [row,
                pl.BlockSpec((1, d, tf), lambda i, j: (0, 0, j)),
                pl.BlockSpec((1, d, tf), lambda i, j: (0, 0, j)),
                pl.BlockSpec((1, tf, d), lambda i, j: (0, j, 0)),
                vec, vec]
    out_specs = [row, row]
    out_shape = [jax.ShapeDtypeStruct((t, d), F32), jax.ShapeDtypeStruct((t, d), BF16)]
    args = [xf, wg, wu, wd, g, b]
    if reorder is not None:
        wt, layer = reorder
        tc = d // ni
        assert d % ni == 0 and tc % LANES == 0 and wt.shape[1] == D_IN
        in_specs.append(pl.BlockSpec((1, D_IN, tc), lambda i, j: (layer, 0, i)))
        out_specs.append(pl.BlockSpec((1, PROJ_WIDTH, tc), lambda i, j: (0, 0, i)))
        out_shape.append(jax.ShapeDtypeStruct((1, PROJ_WIDTH, d), BF16))
        args.append(wt)
    return pl.pallas_call(
        functools.partial(_ffn_kernel, nf=nf, with_reorder=reorder is not None),
        grid=(ni, nf),
        in_specs=in_specs,
        out_specs=out_specs,
        out_shape=out_shape,
        scratch_shapes=[pltpu.VMEM((tm, d), F32)],
        compiler_params=_cparams(("parallel", "arbitrary")),
        name="ffn",
    )(*args)


def _proj_rope_kernel(xb_ref, w_ref, cos_ref, sin_ref, p_ref, *, rot_dim, q_scale, q_tiles):
    j = pl.program_id(1)
    scale = jnp.where(j < q_tiles, q_scale, 1.0).astype(F32)
    half = rot_dim // 2
    tm = xb_ref.shape[0]
    rc = PROJ_ROW_CHUNK if tm % PROJ_ROW_CHUNK == 0 else tm
    for r0 in range(0, tm, rc):
        rows = slice(r0, r0 + rc)
        acc = _dot_nt(xb_ref[rows, :], w_ref[0])
        cos, sin = cos_ref[rows, :], sin_ref[rows, :]
        for hh in range(PROJ_TN // LANES):
            t = acc[:, _head_cols(hh)]
            if rot_dim == LANES:
                rot = pltpu.roll(t, half, 1)
            else:
                lane = lax.broadcasted_iota(jnp.int32, t.shape, 1)
                rot = jnp.where((lane // half) % 2 == 0, pltpu.roll(t, LANES - half, 1), pltpu.roll(t, half, 1))
            p_ref[rows, _head_cols(hh)] = ((t * cos + rot * sin) * scale).astype(BF16)


def _proj_plain_kernel(xb_ref, w_ref, p_ref, *, gq_cols):
    j = pl.program_id(1)
    col = j * PROJ_TN + lax.broadcasted_iota(jnp.int32, (1, PROJ_TN), 1)
    scale = jnp.where((col >= gq_cols[0]) & (col < gq_cols[1]), GLA_DK ** -0.5, 1.0).astype(F32)
    tm = xb_ref.shape[0]
    rc = PROJ_ROW_CHUNK if tm % PROJ_ROW_CHUNK == 0 else tm
    for r0 in range(0, tm, rc):
        rows = slice(r0, r0 + rc)
        p_ref[rows, :] = (_dot_nt(xb_ref[rows, :], w_ref[0]) * scale).astype(BF16)


def _proj_row_tile(t):
    return 2048 if t % 2048 == 0 else t


def _proj_plain_steps(t):
    return (t // _proj_row_tile(t)) * ((N_COL_BLOCKS - COL_DV) * LANES // PROJ_TN)


def _proj(xb, w, layer, tables, hosted=()):
    t, d = xb.shape
    tm = _proj_row_tile(t)

    def tiles(c):
        assert (c * LANES) % PROJ_TN == 0
        return c * LANES // PROJ_TN

    cos_a, sin_a, cos_b, sin_b = tables
    x_spec = pl.BlockSpec((tm, d), lambda i, j: (i, 0))
    tab = pl.BlockSpec((tm, LANES), lambda i, j: (i, 0))
    out = pl.BlockSpec((tm, PROJ_TN), lambda i, j: (i, j))
    w_spec = lambda first: pl.BlockSpec((1, PROJ_TN, d), lambda i, j: (layer, first + j, 0))

    def rope_group(first_col, q_heads, n_heads, rot_dim, q_scale, cos, sin, name):
        nj = tiles(n_heads)
        return pl.pallas_call(
            functools.partial(_proj_rope_kernel, rot_dim=rot_dim, q_scale=q_scale, q_tiles=tiles(q_heads)),
            grid=(t // tm, nj),
            in_specs=[x_spec, w_spec(tiles(first_col)), tab, tab],
            out_specs=out,
            out_shape=jax.ShapeDtypeStruct((t, nj * PROJ_TN), BF16),
            compiler_params=_cparams(("parallel", "arbitrary")),
            name=name,
        )(xb, w, cos, sin)

    pa = rope_group(COL_DQ, DIFF_HEADS, 2 * DIFF_HEADS, DIFF_QK_DIM, DIFF_QK_DIM ** -0.5 * LOG2_E, cos_a, sin_a,
                    "in_proj_diff")
    pb = rope_group(COL_MQ, MOBA_HEADS, 2 * MOBA_HEADS, HEAD_DIM, HEAD_DIM ** -0.5 * LOG2_E, cos_b, sin_b,
                    "in_proj_moba")
    nj = tiles(N_COL_BLOCKS - COL_DV)
    h_in, h_out, h_shape, h_args = _hosted_specs(hosted, nj)
    assert (t // tm) * nj >= max([n for _, _, n in hosted], default=0)
    pc, *side = pl.pallas_call(
        _with_hosted_casts(functools.partial(_proj_plain_kernel, gq_cols=((COL_GQ - COL_DV) * LANES,
                                                                           (COL_GK - COL_DV) * LANES)),
                           2, 1, len(hosted)),
        grid=(t // tm, nj),
        in_specs=[x_spec, w_spec(tiles(COL_DV))] + h_in,
        out_specs=[out] + h_out,
        out_shape=[jax.ShapeDtypeStruct((t, nj * PROJ_TN), BF16)] + h_shape,
        compiler_params=_cparams(("arbitrary", "arbitrary")),
        name="in_proj_plain",
    )(xb, w, *h_args)
    return pa, pb, pc, side


_IN_WIDTHS = (DIFF_HEADS * 128, DIFF_HEADS * 128, DIFF_HEADS * 128, GLA_HEADS * GLA_DK, GLA_HEADS * GLA_DK,
              GLA_HEADS * GLA_DV, GLA_HEADS * GLA_DV, GLA_GATE_RANK, MOBA_HEADS * 128, MOBA_HEADS * 128,
              MOBA_HEADS * 128)
_IN_OFFS = tuple(int(v) for v in np.concatenate([[0], np.cumsum(_IN_WIDTHS)]))
D_IN = _IN_OFFS[-1]


def _reorder_w_in_kernel(w_ref, o_ref):
    src = dict(zip(("dq", "dk", "dv", "gq", "gk", "gv", "gr", "gg", "mq", "mk", "mv"), zip(_IN_OFFS[:-1], _IN_WIDTHS)))
    for name, col in (("dq", COL_DQ), ("dk", COL_DK), ("mq", COL_MQ), ("mk", COL_MK), ("dv", COL_DV), ("mv", COL_MV),
                      ("gq", COL_GQ), ("gk", COL_GK), ("gv", COL_GV), ("gr", COL_GR)):
        s0, width = src[name]
        o_ref[0, col * LANES:col * LANES + width, :] = w_ref[0, s0:s0 + width, :].astype(BF16)


def _cast_kernel(x_ref, o_ref):
    o_ref[...] = x_ref[...].astype(o_ref.dtype)


def _cast_bf16(x):
    l, r, c = x.shape
    tr = next((t for t in (1024, 512, 256, 128) if r % t == 0 and t * c * 4 <= (4 << 20)), r)
    spec = pl.BlockSpec((1, tr, c), lambda i, j: (i, j, 0))
    return pl.pallas_call(
        _cast_kernel,
        grid=(l, r // tr),
        in_specs=[spec],
        out_specs=spec,
        out_shape=jax.ShapeDtypeStruct(x.shape, BF16),
        compiler_params=_cparams(("parallel", "parallel")),
        name="cast_bf16",
    )(x)


def _store_v_transposed(v_ref, vt_ref, tk):
    seq = v_ref.shape[0]
    for blk in range(seq // tk):
        vt_ref[0:HEAD_DIM, blk * tk:(blk + 1) * tk] = v_ref[blk * tk:(blk + 1) * tk, :].astype(F32).T.astype(BF16)
    vt_ref[HEAD_DIM:V_ROWS, :] = jnp.ones((V_ROWS - HEAD_DIM, seq), BF16)


def _run_tiles(n, scores, finish):
    pending = [scores(t) for t in range(min(ATTN_LOOKAHEAD, n))]
    for t in range(n):
        if t + ATTN_LOOKAHEAD < n:
            pending.append(scores(t + ATTN_LOOKAHEAD))
        finish(t, pending.pop(0))


def _col_max(parts):
    m = jnp.max(parts[0], axis=0, keepdims=True)
    for s in parts[1:]:
        m = jnp.maximum(m, jnp.max(s, axis=0, keepdims=True))
    return m


def _diff_kernel(lam_ref, g_ref, q_ref, k_ref, v_ref, o_ref, vt_ref, *, tq, lam_init):
    seq = k_ref.shape[0]
    _store_v_transposed(v_ref, vt_ref, tq)
    lf = lam_ref[...]
    lam = (jnp.exp(jnp.sum(lf[0:1] * lf[1:2], axis=1, keepdims=True))
           - jnp.exp(jnp.sum(lf[2:3] * lf[3:4], axis=1, keepdims=True)) + lam_init)
    lane = lax.broadcasted_iota(jnp.int32, (tq, LANES), 1)
    causal = (lax.broadcasted_iota(jnp.int32, (tq, 2 * tq), 0)
              <= lax.broadcasted_iota(jnp.int32, (tq, 2 * tq), 1) % tq)
    def scores(qi):
        lo, hi = qi * tq, (qi + 1) * tq
        q = q_ref[lo:hi, :].astype(F32)
        qq = jnp.concatenate([jnp.where(lane < DIFF_QK_DIM, q, 0.0),
                              jnp.where(lane >= DIFF_QK_DIM, q, 0.0)], axis=0).astype(BF16)
        parts = [jnp.where(causal, _dot_nt(k_ref[lo:hi, :], qq), NEG_INF)]
        if qi > 0:
            parts.append(_dot_nt(k_ref[0:lo, :], qq))
        return parts

    def finish(qi, parts):
        lo, hi = qi * tq, (qi + 1) * tq
        m = _col_max(parts)
        acc = _dot(vt_ref[:, lo:hi], jnp.exp2(parts[0] - m).astype(BF16))
        if qi > 0:
            acc = acc + _dot(vt_ref[:, 0:lo], jnp.exp2(parts[1] - m).astype(BF16))
        o_t = acc[0:HEAD_DIM] / acc[HEAD_DIM:HEAD_DIM + 1]
        o = (o_t[:, :tq] - lam * o_t[:, tq:]).T
        o = o * lax.rsqrt(jnp.mean(o * o, axis=-1, keepdims=True) + LN_EPS) * g_ref[...] * (1.0 - lam_init)
        o_ref[lo:hi, :] = o.astype(BF16)

    _run_tiles(seq // tq, scores, finish)


def _with_hosted_casts(kernel_fn, n_in, n_out, n_hosted):
    def wrapped(*refs):
        ins, rest = refs[:n_in], refs[n_in:]
        srcs, rest = rest[:n_hosted], rest[n_hosted:]
        outs, rest = rest[:n_out], rest[n_out:]
        dsts, scratch = rest[:n_hosted], rest[n_hosted:]
        for src, dst in zip(srcs, dsts):
            dst[...] = src[...].astype(BF16)
        kernel_fn(*ins, *outs, *scratch)
    return wrapped


def _hosted_specs(hosted, n_inner=None):
    step = (lambda i: i) if n_inner is None else (lambda b, h: b * n_inner + h)
    ins, outs, shapes = [], [], []
    for arr, layer, nblk in hosted:
        _, r, c = arr.shape
        assert r % nblk == 0
        blk = lambda *g, nblk=nblk: jnp.minimum(step(*g), nblk - 1)
        ins.append(pl.BlockSpec((1, r // nblk, c), lambda *g, layer=layer, blk=blk: (layer, blk(*g), 0)))
        outs.append(pl.BlockSpec((1, r // nblk, c), lambda *g, blk=blk: (0, blk(*g), 0)))
        shapes.append(jax.ShapeDtypeStruct((1, r, c), BF16))
    return ins, outs, shapes, [arr for arr, _, _ in hosted]


def _diff_attention(pa, pc, lam, g, batch, seq, layer, hosted=()):
    tq = 256
    lam_init = 0.8 - 0.6 * math.exp(-0.3 * layer)
    col = lambda c0: pl.BlockSpec((seq, LANES), lambda b, h: (b, c0 + h))
    h_in, h_out, h_shape, h_args = _hosted_specs(hosted, DIFF_HEADS)
    assert batch * DIFF_HEADS >= max([n for _, _, n in hosted], default=0)
    return pl.pallas_call(
        _with_hosted_casts(functools.partial(_diff_kernel, tq=tq, lam_init=lam_init), 5, 1, len(hosted)),
        grid=(batch, DIFF_HEADS),
        in_specs=[pl.BlockSpec((4, DIFF_QK_DIM), lambda b, h: (0, 0)),
                  pl.BlockSpec((1, LANES), lambda b, h: (0, 0)),
                  col(0), col(DIFF_HEADS), col(0)] + h_in,
        out_specs=[pl.BlockSpec((seq, LANES), lambda b, h: (b, h))] + h_out,
        out_shape=[jax.ShapeDtypeStruct((batch * seq, DIFF_HEADS * LANES), BF16)] + h_shape,
        scratch_shapes=[pltpu.VMEM((V_ROWS, seq), BF16)],
        compiler_params=_cparams(("arbitrary", "arbitrary")),
        name="diff_attn",
    )(lam, g, pa, pa, pc, *h_args)


def _moba_kernel(q_ref, k_ref, v_ref, o_ref, vt_ref, *, nb):
    tq = MOBA_BLOCK
    _store_v_transposed(v_ref, vt_ref, tq)
    nbp = -(-nb // SUBLANES) * SUBLANES
    rows = [jnp.mean(k_ref[n * tq:(n + 1) * tq, :].astype(F32), axis=0, keepdims=True) for n in range(nb)]
    if nbp > nb:
        rows.append(jnp.zeros((nbp - nb, LANES), F32))
    km = jnp.concatenate(rows, axis=0)
    km_hi = km.astype(BF16)
    km_lo = (km - km_hi.astype(F32)).astype(BF16)
    causal = (lax.broadcasted_iota(jnp.int32, (tq, tq), 0) <= lax.broadcasted_iota(jnp.int32, (tq, tq), 1))
    def scores(qi):
        lo, hi = qi * tq, (qi + 1) * tq
        q = q_ref[lo:hi, :]
        parts = [jnp.where(causal, _dot_nt(k_ref[lo:hi, :], q), NEG_INF)]
        if qi > 0:
            s_past = _dot_nt(k_ref[0:lo, :], q)
            if qi > MOBA_TOPK:
                gate = _dot_nt(km_hi, q) + _dot_nt(km_lo, q)
                n_idx = lax.broadcasted_iota(jnp.int32, gate.shape, 0)
                rank = jnp.zeros(gate.shape, jnp.int32)
                for mb in range(qi):
                    gm = gate[mb:mb + 1, :]
                    rank = rank + jnp.where((gm > gate) | ((gm == gate) & (mb < n_idx)), 1, 0)
                bias = jnp.where(rank < MOBA_TOPK, 0.0, NEG_INF)
                s_past = jnp.concatenate([s_past[n * tq:(n + 1) * tq] + bias[n:n + 1, :] for n in range(qi)], axis=0)
            parts.append(s_past)
        return parts

    def finish(qi, parts):
        lo, hi = qi * tq, (qi + 1) * tq
        m = _col_max(parts)
        acc = _dot(vt_ref[:, lo:hi], jnp.exp2(parts[0] - m).astype(BF16))
        if qi > 0:
            acc = acc + _dot(vt_ref[:, 0:lo], jnp.exp2(parts[1] - m).astype(BF16))
        o_ref[lo:hi, :] = (acc[0:HEAD_DIM] / acc[HEAD_DIM:HEAD_DIM + 1]).T.astype(BF16)

    _run_tiles(nb, scores, finish)


def _moba_attention(pb, pc, batch, seq, hosted=()):
    assert seq % MOBA_BLOCK == 0
    col = lambda c0: pl.BlockSpec((seq, LANES), lambda b, h: (b, c0 + h))
    h_in, h_out, h_shape, h_args = _hosted_specs(hosted, MOBA_HEADS)
    assert batch * MOBA_HEADS >= max([n for _, _, n in hosted], default=0)
    return pl.pallas_call(
        _with_hosted_casts(functools.partial(_moba_kernel, nb=seq // MOBA_BLOCK), 3, 1, len(hosted)),
        grid=(batch, MOBA_HEADS),
        in_specs=[col(0), col(MOBA_HEADS), col(COL_MV - COL_DV)] + h_in,
        out_specs=[pl.BlockSpec((seq, LANES), lambda b, h: (b, h))] + h_out,
        out_shape=[jax.ShapeDtypeStruct((batch * seq, MOBA_HEADS * LANES), BF16)] + h_shape,
        scratch_shapes=[pltpu.VMEM((V_ROWS, seq), BF16)],
        compiler_params=_cparams(("arbitrary", "arbitrary")),
        name="moba_attn",
    )(pb, pb, pc, *h_args)


def _gla_constants():
    c = GLA_CHUNK
    idx = np.arange(c)
    mats = [np.tril(np.ones((c, c), np.float32))]
    bm = []
    for lvl in range(6):
        m = 32 >> lvl
        parent, half = idx // (2 * m), (idx // m) % 2
        ref = parent * 2 * m + m - 1
        t = idx[None, :]
        second = (half == 1)[:, None] & (t > ref[:, None]) & (t <= idx[:, None])
        first = (half == 0)[:, None] & (t > idx[:, None]) & (t <= ref[:, None])
        mats.append((second | first).astype(np.float32))
        bm.append(((parent[:, None] == parent[None, :]) & (half == 1)[:, None] & (half == 0)[None, :]).astype(np.float32))
    bm.append(np.eye(c, dtype=np.float32))
    cmat = np.concatenate(mats, axis=0)
    return jnp.asarray(cmat, BF16), jnp.asarray(np.concatenate(bm, axis=0), F32)


def _split2(x):
    hi = x.astype(BF16)
    return hi, (x - hi.astype(F32)).astype(BF16)


def _gla_kernel(cmat_ref, bm_ref, wgg_ref, up_ref, bias_ref, gn_ref, x_ref, q_ref, k_ref, v_ref, r_ref, o_ref,
                la_ref, st_ref, *, seq, heads):
    c = GLA_CHUNK
    wg = wgg_ref[...].astype(BF16)
    up_hi, up_lo = _split2(up_ref[...])
    rows = 512 if seq % 512 == 0 else seq
    for r0 in range(0, seq, rows):
        g_hi, g_lo = _split2(_dot(x_ref[r0:r0 + rows, :], wg))
        z = _dot(g_hi, up_hi) + _dot(g_hi, up_lo) + _dot(g_lo, up_hi) + bias_ref[...]
        la_ref[r0:r0 + rows, :] = (jnp.minimum(z, 0.0) - jnp.log(1.0 + jnp.exp(-jnp.abs(z)))) * (1.0 / GLA_TAU)

    st_ref[...] = jnp.zeros_like(st_ref)
    cmat = cmat_ref[...]
    lane = lax.broadcasted_iota(jnp.int32, (c, LANES), 1)
    own = [lane < GLA_DK, lane >= GLA_DK]

    def halves(t):
        return [jnp.where(m, t, 0.0).astype(BF16) for m in own]

    def chunk_group(cp, carry):
        sls = [pl.ds(pl.multiple_of((cp * GLA_UNROLL + u) * c, c), c) for u in range(GLA_UNROLL)]
        e_alls = []
        for sl in sls:
            g_hi, g_lo = _split2(la_ref[sl, :])
            e_alls.append(_dot(cmat, g_hi) + _dot(cmat, g_lo))
        ids = [(sl, p, e_all[:, _head_cols(p)]) for sl, e_all in zip(sls, e_alls) for p in range(heads // 2)]
        qs = [q_ref[sl, _head_cols(p)].astype(F32) for sl, p, _ in ids]
        ks = [k_ref[sl, _head_cols(p)].astype(F32) for sl, p, _ in ids]
        atts = []
        for q, k in zip(qs, ks):
            kb = k.astype(BF16)
            atts.append([bm_ref[6 * c:7 * c, :] * _dot_nt(qh, kb) for qh in halves(q)])
        for lvl in range(6):
            for i, (_, _, e) in enumerate(ids):
                w = jnp.exp(e[(1 + lvl) * c:(2 + lvl) * c])
                kw = (ks[i] * w).astype(BF16)
                for hf, qh in enumerate(halves(qs[i] * w)):
                    atts[i][hf] = atts[i][hf] + bm_ref[lvl * c:(lvl + 1) * c, :] * _dot_nt(qh, kw)
        streams = []
        for (sl, p, e), q, k, att in zip(ids, qs, ks, atts):
            cum = e[0:c]
            last = cum[c - 1:c, :]
            qes, kds = halves(q * jnp.exp(cum)), halves(k * jnp.exp(last - cum))
            for hf in range(2):
                h = 2 * p + hf
                v = v_ref[sl, _head_cols(h)]
                o_intra = _dot(att[hf].astype(BF16), v)
                upd = _dot(v.astype(F32).T.astype(BF16), kds[hf])
                streams.append((sl, h, qes[hf], o_intra, upd, jnp.exp(last)))
        for args in streams:
            recur(*args)
        return carry

    def recur(sl, h, qe, o_intra, upd, decay):
        hc = _head_cols(h)
        st = st_ref[h]
        o = _dot_nt(qe, st.astype(BF16)) + o_intra
        st_ref[h] = st * decay + upd
        o = o * lax.rsqrt(jnp.mean(o * o, axis=-1, keepdims=True) + LN_EPS) * gn_ref[...]
        r = r_ref[sl, hc].astype(F32)
        o_ref[sl, hc] = (o * (r * jax.nn.sigmoid(r))).astype(BF16)

    assert (seq // c) % GLA_UNROLL == 0 and heads % 2 == 0
    lax.fori_loop(0, seq // c // GLA_UNROLL, chunk_group, 0)


def _gla(xb, pc, w_gg, up, bias, gn, batch, seq):
    cmat, bm = _gla_constants()
    heads = GLA_HEADS
    d = xb.shape[1]
    qk_w, v_w = heads * GLA_DK, heads * GLA_DV
    wgg_p = jnp.pad(w_gg, ((0, 0), (0, GG_PAD - GLA_GATE_RANK)))
    up_p = jnp.pad(up, ((0, GG_PAD - GLA_GATE_RANK), (0, 0)))
    const = lambda shape: pl.BlockSpec(shape, lambda b: (0, 0))
    col = lambda c0, width: pl.BlockSpec((seq, width), lambda b: (b, (c0 - COL_DV) * LANES // width))
    return pl.pallas_call(
        functools.partial(_gla_kernel, seq=seq, heads=heads),
        grid=(batch,),
        in_specs=[const(cmat.shape), const(bm.shape), const((d, GG_PAD)), const((GG_PAD, qk_w)), const((1, qk_w)),
                  const((1, LANES)), pl.BlockSpec((seq, d), lambda b: (b, 0)),
                  col(COL_GQ, qk_w), col(COL_GK, qk_w), col(COL_GV, v_w), col(COL_GR, v_w)],
        out_specs=pl.BlockSpec((seq, v_w), lambda b: (b, 0)),
        out_shape=jax.ShapeDtypeStruct((batch * seq, v_w), BF16),
        scratch_shapes=[pltpu.VMEM((seq, qk_w), F32), pltpu.VMEM((heads, GLA_DV, LANES), F32)],
        compiler_params=_cparams(("parallel",)),
        name="gla",
    )(cmat, bm, wgg_p, up_p, bias.reshape(1, qk_w), gn, xb, pc, pc, pc, pc)


def _emit_stream(of_ref, ob_ref, y, g_ref, b_ref, rows=slice(None)):
    out = _layer_norm(y, g_ref[...], b_ref[...])
    of_ref[rows, :] = out
    if ob_ref is not None:
        ob_ref[rows, :] = out.astype(BF16)


def _row_chunks(tm):
    rc = STREAM_ROW_CHUNK if tm % STREAM_ROW_CHUNK == 0 else tm
    return [slice(r0, r0 + rc) for r0 in range(0, tm, rc)]


def _out_kernel(od_ref, og_ref, om_ref, w_ref, xf_ref, g_ref, b_ref, of_ref):
    ob_ref = None
    k0 = od_ref.shape[1]
    k1 = k0 + og_ref.shape[1]
    for rows in _row_chunks(xf_ref.shape[0]):
        y = (_dot(od_ref[rows, :], w_ref[0, 0:k0, :]) + _dot(og_ref[rows, :], w_ref[0, k0:k1, :])
             + _dot(om_ref[rows, :], w_ref[0, k1:, :]))
        _emit_stream(of_ref, ob_ref, DEEPNORM_ALPHA * xf_ref[rows, :] + y, g_ref, b_ref, rows)


def _row_tile(t):
    return 512 if t % 512 == 0 else t


def _out_proj(o_diff, o_gla, o_moba, w, layer, xf, g, b):
    t, d = xf.shape
    tm = _row_tile(t)
    row = pl.BlockSpec((tm, d), lambda i: (i, 0))
    vec = pl.BlockSpec((1, d), lambda i: (0, 0))
    part = lambda a: pl.BlockSpec((tm, a.shape[1]), lambda i: (i, 0))
    return pl.pallas_call(
        _out_kernel,
        grid=(t // tm,),
        in_specs=[part(o_diff), part(o_gla), part(o_moba),
                  pl.BlockSpec((1,) + w.shape[1:], lambda i: (layer, 0, 0)),
                  row, vec, vec],
        out_specs=row,
        out_shape=jax.ShapeDtypeStruct((t, d), F32),
        compiler_params=_cparams(("parallel",)),
        name="out_proj",
    )(o_diff, o_gla, o_moba, w, xf, g, b)


def _ple_kernel(xb_ref, wg_ref, pb_ref, we_ref, xf_ref, g_ref, b_ref, of_ref):
    ob_ref = None
    for rows in _row_chunks(xf_ref.shape[0]):
        e = _dot(pb_ref[0, rows, :], we_ref[0]) * jax.nn.sigmoid(_dot(xb_ref[rows, :], wg_ref[0]))
        _emit_stream(of_ref, ob_ref, DEEPNORM_ALPHA * xf_ref[rows, :] + e, g_ref, b_ref, rows)


def _ple(xf, xb, pb, w_pe, pe_layer, w_pg, g, b):
    t, d = xf.shape
    tm = _row_tile(t)
    row = pl.BlockSpec((tm, d), lambda i: (i, 0))
    vec = pl.BlockSpec((1, d), lambda i: (0, 0))
    whole = lambda a, layer: pl.BlockSpec((1,) + a.shape[1:], lambda i: (layer, 0, 0))
    return pl.pallas_call(
        _ple_kernel,
        grid=(t // tm,),
        in_specs=[row, whole(w_pg, 0),
                  pl.BlockSpec((1, tm, pb.shape[2]), lambda i: (0, i, 0)),
                  whole(w_pe, pe_layer), row, vec, vec],
        out_specs=row,
        out_shape=jax.ShapeDtypeStruct((t, d), F32),
        compiler_params=_cparams(("parallel",)),
        name="ple",
    )(xb, w_pg, pb, w_pe, xf, g, b)


def kernel(x, p, positions, w_in, w_out, diff_lambda, diff_norm_g, gla_gate_up, gla_gate_b, gla_norm_g,
           ffn1_gate, ffn1_up, ffn1_down, ffn2_gate, ffn2_up, ffn2_down, w_pe, w_pg, ln_g, ln_b):
    batch, seq, d = x.shape
    t = batch * seq
    assert d == (DIFF_HEADS + GLA_HEADS + MOBA_HEADS) * HEAD_DIM and seq % MOBA_BLOCK == 0
    xf = x.reshape(t, d)
    rope_blocks = max(n for n in (1, 2, 4, 8, 16) if n <= t // MOBA_BLOCK)
    *tables, w_g, w_u, w_d = _rope_tables(positions.reshape(t, 1), steps=rope_blocks,
                                          hosted=[(a, 0, rope_blocks) for a in (ffn1_gate, ffn1_up, ffn1_down)])
    w_ffn1 = (w_g, w_u, w_d)
    vec = lambda v: v.reshape(1, -1)
    w_pe_b = _cast_bf16(w_pe)
    assert w_in.shape[2] == D_IN and all(o % 16 == 0 for o in _IN_OFFS)
    w_in_t = jnp.swapaxes(w_in, 1, 2)
    p3 = p.reshape(DEPTH, t, -1)
    side_blocks = max(n for n in (1, 2, 4, 8, 16) if n <= batch * min(DIFF_HEADS, MOBA_HEADS))
    side = lambda arrs, layer: [(a, layer, side_blocks) for a in arrs]
    for i in range(DEPTH):
        more = i + 1 < DEPTH
        xf, xb, w_in_b = _ffn(xf, w_ffn1, vec(ln_g[i, 0]), vec(ln_b[i, 0]), reorder=(w_in_t, i))
        pa, pb, pc, w_gu2 = _proj(xb, w_in_b, 0, tables,
                                  hosted=[(a, i, min(side_blocks, _proj_plain_steps(t))) for a in (ffn2_gate, ffn2_up)])
        o_diff, w_out_b, p_b, w_d2, *w_down_next = _diff_attention(
            pa, pc, diff_lambda[i], vec(diff_norm_g[i]), batch, seq, i,
            hosted=side((w_out, p3, ffn2_down), i) + (side((ffn1_down,), i + 1) if more else []))
        w_ffn2 = tuple(w_gu2) + (w_d2,)
        gg0 = _IN_OFFS[7]
        o_gla = _gla(xb, pc, w_in[i, :, gg0:gg0 + GLA_GATE_RANK], gla_gate_up[i], gla_gate_b[i],
                     vec(gla_norm_g[i]), batch, seq)
        o_moba, w_pg_b, *w_gu_next = _moba_attention(
            pb, pc, batch, seq, hosted=side((w_pg,), i) + (side((ffn1_gate, ffn1_up), i + 1) if more else []))
        xf = _out_proj(o_diff, o_gla, o_moba, w_out_b, 0, xf, vec(ln_g[i, 1]), vec(ln_b[i, 1]))
        xf, xb = _ffn(xf, w_ffn2, vec(ln_g[i, 2]), vec(ln_b[i, 2]))
        xf = _ple(xf, xb, p_b, w_pe_b, i, w_pg_b, vec(ln_g[i, 3]), vec(ln_b[i, 3]))
        w_ffn1 = tuple(w_gu_next) + tuple(w_down_next)
    return xf.reshape(batch, seq, d)
```
